```python
import math
import jax, jax.numpy as jnp
from jax import lax
import numpy as np

D_MODEL = 1024
BATCH = 8
SEQ = 2048
DEPTH = 1
DEC_BATCH = 128
DEC_SEQ = 4
PAST_LEN = 2048
PAGE_SIZE = 128

MIX_WIDTH = D_MODEL
DA_WIDTH = MIX_WIDTH // 2
DA_HEAD_DIM = 64
DA_HEADS = DA_WIDTH // (2 * DA_HEAD_DIM)
DA_V_DIM = 2 * DA_HEAD_DIM
GM_WIDTH = MIX_WIDTH - DA_WIDTH
GM_HEAD_DIM = 128
GM_HEADS = GM_WIDTH // GM_HEAD_DIM
CHUNK = 128
IN_WIDTH = 3 * DA_WIDTH + 2 * GM_WIDTH
ROPE_THETA = 500000.0
ROPE_DIM = DA_HEAD_DIM // 4
Q_BLOCK = 128
PEER_HEADS = 8
PEER_N_KEYS = 128
PEER_N_EXPERTS = PEER_N_KEYS * PEER_N_KEYS
PEER_QUERY_DIM = 256
PEER_HALF = PEER_QUERY_DIM // 2
PEER_TOPK = 16
PEER_BLOCK = 256
NORM_EPS = 1e-6

kernel_name = "hymba_gmlp_diffattn_peer_step"


def rms_norm(x, g):
    xf = x.astype(jnp.float32)
    y = xf * lax.rsqrt(jnp.mean(xf * xf, axis=-1, keepdims=True) + NORM_EPS)
    return (y * g.astype(jnp.float32)).astype(x.dtype)


def rope(x, pos):
    half = ROPE_DIM // 2
    inv = ROPE_THETA ** (-jnp.arange(half, dtype=jnp.float32) * 2.0 / ROPE_DIM)
    ang = pos[:, None] * inv[None, :]
    cos = jnp.cos(ang)[None, :, None, None, :]
    sin = jnp.sin(ang)[None, :, None, None, :]
    x1 = x[..., :half].astype(jnp.float32)
    x2 = x[..., half:ROPE_DIM].astype(jnp.float32)
    return jnp.concatenate([(x1 * cos - x2 * sin).astype(x.dtype),
                            (x2 * cos + x1 * sin).astype(x.dtype),
                            x[..., ROPE_DIM:]], axis=-1)


def project(xn, w_in, gm_g, pos):
    B, S, _ = xn.shape
    z = xn @ w_in
    q, k, v, u, g = jnp.split(z, [DA_WIDTH, 2 * DA_WIDTH, 3 * DA_WIDTH, 3 * DA_WIDTH + GM_WIDTH], axis=-1)
    q = rope(q.reshape(B, S, DA_HEADS, 2, DA_HEAD_DIM), pos)
    k = rope(k.reshape(B, S, DA_HEADS, 2, DA_HEAD_DIM), pos)
    v = v.reshape(B, S, DA_HEADS, DA_V_DIM)
    u = jax.nn.gelu(u).reshape(B, S, GM_HEADS, GM_HEAD_DIM)
    g = rms_norm(jax.nn.gelu(g).reshape(B, S, GM_HEADS, GM_HEAD_DIM), gm_g)
    return q, k, v, u, g


def diff_lambda(lq1, lk1, lq2, lk2, lam_init):
    f = lambda a, b: jnp.exp(jnp.sum(a.astype(jnp.float32) * b.astype(jnp.float32)))
    return f(lq1, lk1) - f(lq2, lk2) + lam_init


def diff_attend(q, k, v, q_pos, k_pos, lam):
    s = jnp.einsum('bqhcd,bkhcd->bchqk', q, k, preferred_element_type=jnp.float32)
    s = s * (1.0 / math.sqrt(DA_HEAD_DIM))
    mask = k_pos[None, :] <= q_pos[:, None]
    s = jnp.where(mask, s, -jnp.inf)
    p = jax.nn.softmax(s, axis=-1)
    a = p[:, 0] - lam * p[:, 1]
    return jnp.einsum('bhqk,bkhd->bqhd', a.astype(v.dtype), v, preferred_element_type=jnp.float32).astype(v.dtype)


def gm_mix(u, g, ws, b):
    L = u.shape[2]
    w = jnp.tril(ws[:, :L, :L])
    s = jnp.einsum('hij,bcjhd->bcihd', w, g) + jnp.transpose(b[:, :L])[None, None, :, :, None]
    return u * s


def merge(attn_o, gm_o, subln_g, lam_init, w_o):
    B, S = attn_o.shape[:2]
    a = (rms_norm(attn_o, subln_g) * (1.0 - lam_init)).reshape(B, S, DA_WIDTH)
    return jnp.concatenate([a, gm_o.reshape(B, S, GM_WIDTH).astype(a.dtype)], axis=-1) @ w_o


def peer(xn, w_pq, pk, pu, pv):
    shape = xn.shape
    t = xn.reshape(-1, D_MODEL)
    T = t.shape[0]
    nblk = -(-T // PEER_BLOCK)
    t = jnp.pad(t, ((0, nblk * PEER_BLOCK - T), (0, 0))).reshape(nblk, PEER_BLOCK, D_MODEL)

    def blk(tb):
        q = (tb @ w_pq).reshape(PEER_BLOCK, PEER_HEADS, 2, PEER_HALF).astype(jnp.float32)
        s = jnp.einsum('thcd,hckd->thck', q, pk.astype(jnp.float32))
        s1, i1 = lax.top_k(s[:, :, 0], PEER_TOPK)
        s2, i2 = lax.top_k(s[:, :, 1], PEER_TOPK)
        cand = (s1[..., :, None] + s2[..., None, :]).reshape(PEER_BLOCK, PEER_HEADS, PEER_TOPK * PEER_TOPK)
        cid = (i1[..., :, None] * PEER_N_KEYS + i2[..., None, :]).reshape(PEER_BLOCK, PEER_HEADS, PEER_TOPK * PEER_TOPK)
        sc, sel = lax.top_k(cand, PEER_TOPK)
        eid = jnp.take_along_axis(cid, sel, axis=-1)
        gate = jax.nn.softmax(sc, axis=-1)
        a = jnp.einsum('td,thed->the', tb, pu[eid])
        h = gate * jax.nn.gelu(a.astype(jnp.float32))
        return jnp.einsum('the,thed->td', h.astype(tb.dtype), pv[eid])

    out = lax.map(blk, t).reshape(-1, D_MODEL)[:T]
    return out.reshape(shape)


def setup_inputs(seed: int = 0) -> dict:
    key = jax.random.key(seed)
    ks = jax.random.split(key, 24)
    f32 = jnp.float32
    nrm = lambda k, shape, scale: jax.random.normal(k, shape, f32) * scale
    n_pages = PAST_LEN // PAGE_SIZE
    n_used = DEC_BATCH * n_pages
    n_phys = n_used + max(1, n_used // 4)
    page_table = jax.random.permutation(ks[0], n_phys)[:n_used].reshape(DEC_BATCH, n_pages).astype(jnp.int32)
    return {
        "x_prompt": nrm(ks[1], (BATCH, SEQ, D_MODEL), 1.0),
        "x_sample": nrm(ks[2], (DEC_BATCH, DEC_SEQ, D_MODEL), 1.0),
        "cache_k": nrm(ks[3], (DEPTH, n_phys, PAGE_SIZE, DA_HEADS, 2 * DA_HEAD_DIM), 1.0),
        "cache_v": nrm(ks[4], (DEPTH, n_phys, PAGE_SIZE, DA_HEADS, DA_V_DIM), 1.0),
        "page_table": page_table,
        "norm1_g": 1.0 + nrm(ks[5], (DEPTH, D_MODEL), 0.02),
        "w_in": nrm(ks[6], (DEPTH, D_MODEL, IN_WIDTH), D_MODEL ** -0.5),
        "gm_norm_g": 1.0 + nrm(ks[7], (DEPTH, GM_HEADS, GM_HEAD_DIM), 0.02),
        "gm_ws": nrm(ks[8], (DEPTH, GM_HEADS, CHUNK, CHUNK), CHUNK ** -0.5),
        "gm_b": 1.0 + nrm(ks[9], (DEPTH, GM_HEADS, CHUNK), 0.01),
        "da_lq1": nrm(ks[10], (DEPTH, DA_HEAD_DIM), 0.1),
        "da_lk1": nrm(ks[11], (DEPTH, DA_HEAD_DIM), 0.1),
        "da_lq2": nrm(ks[12], (DEPTH, DA_HEAD_DIM), 0.1),
        "da_lk2": nrm(ks[13], (DEPTH, DA_HEAD_DIM), 0.1),
        "da_subln_g": 1.0 + nrm(ks[14], (DEPTH, DA_V_DIM), 0.02),
        "w_o": nrm(ks[15], (DEPTH, MIX_WIDTH, D_MODEL), MIX_WIDTH ** -0.5),
        "norm2_g": 1.0 + nrm(ks[16], (DEPTH, D_MODEL), 0.02),
        "peer_wq": nrm(ks[17], (DEPTH, D_MODEL, PEER_HEADS * PEER_QUERY_DIM), D_MODEL ** -0.5),
        "peer_keys": nrm(ks[18], (DEPTH, PEER_HEADS, 2, PEER_N_KEYS, PEER_HALF), PEER_HALF ** -0.5),
        "peer_u": nrm(ks[19], (DEPTH, PEER_N_EXPERTS, D_MODEL), D_MODEL ** -0.5),
        "peer_v": nrm(ks[20], (DEPTH, PEER_N_EXPERTS, D_MODEL), PEER_HEADS ** -0.5),
        "final_g": 1.0 + nrm(ks[21], (D_MODEL,), 0.02),
    }


def reference(x_prompt, x_sample, cache_k, cache_v, page_table, norm1_g, w_in, gm_norm_g, gm_ws, gm_b,
              da_lq1, da_lk1, da_lq2, da_lk2, da_subln_g, w_o, norm2_g, peer_wq, peer_keys, peer_u,
              peer_v, final_g):
    B, S, _ = x_prompt.shape
    DB, DS, _ = x_sample.shape
    past = page_table.shape[1] * PAGE_SIZE
    pos_p = jnp.arange(S, dtype=jnp.float32)
    pos_s = past + jnp.arange(DS, dtype=jnp.float32)
    kpos_p = jnp.arange(S)
    qpos_s = past + jnp.arange(DS)
    kpos_s = jnp.arange(past + DS)
    xp, xs = x_prompt, x_sample
    nkp, nvp, nks, nvs, ngs = [], [], [], [], []
    for l in range(DEPTH):
        lam_init = 0.8 - 0.6 * math.exp(-0.3 * l)
        lam = diff_lambda(da_lq1[l], da_lk1[l], da_lq2[l], da_lk2[l], lam_init)

        q, k, v, u, g = project(rms_norm(xp, norm1_g[l]), w_in[l], gm_norm_g[l], pos_p)

        def qblock(i, q=q, k=k, v=v):
            qb = lax.dynamic_slice_in_dim(q, i * Q_BLOCK, Q_BLOCK, axis=1)
            return diff_attend(qb, k, v, i * Q_BLOCK + jnp.arange(Q_BLOCK), kpos_p, lam)

        ao = lax.map(qblock, jnp.arange(S // Q_BLOCK))
        ao = jnp.moveaxis(ao, 0, 1).reshape(B, S, DA_HEADS, DA_V_DIM)
        go = gm_mix(u.reshape(B, S // CHUNK, CHUNK, GM_HEADS, GM_HEAD_DIM),
                    g.reshape(B, S // CHUNK, CHUNK, GM_HEADS, GM_HEAD_DIM), gm_ws[l], gm_b[l])
        xp = xp + merge(ao, go.reshape(B, S, GM_HEADS, GM_HEAD_DIM), da_subln_g[l], lam_init, w_o[l]).astype(xp.dtype)
        xp = xp + peer(rms_norm(xp, norm2_g[l]), peer_wq[l], peer_keys[l], peer_u[l], peer_v[l]).astype(xp.dtype)
        nkp.append(k.reshape(B, S, DA_HEADS, 2 * DA_HEAD_DIM))
        nvp.append(v)

        qs, ks_, vs, us, gs = project(rms_norm(xs, norm1_g[l]), w_in[l], gm_norm_g[l], pos_s)
        past_k = cache_k[l][page_table].reshape(DB, past, DA_HEADS, 2, DA_HEAD_DIM)
        past_v = cache_v[l][page_table].reshape(DB, past, DA_HEADS, DA_V_DIM)
        k_all = jnp.concatenate([past_k.astype(ks_.dtype), ks_], axis=1)
        v_all = jnp.concatenate([past_v.astype(vs.dtype), vs], axis=1)
        aos = diff_attend(qs, k_all, v_all, qpos_s, kpos_s, lam)
        gos = gm_mix(us[:, None], gs[:, None], gm_ws[l], gm_b[l])[:, 0]
        xs = xs + merge(aos, gos, da_subln_g[l], lam_init, w_o[l]).astype(xs.dtype)
        xs = xs + peer(rms_norm(xs, norm2_g[l]), peer_wq[l], peer_keys[l], peer_u[l], peer_v[l]).astype(xs.dtype)
        nks.append(ks_.reshape(DB, DS, DA_HEADS, 2 * DA_HEAD_DIM))
        nvs.append(vs)
        ngs.append(gs)

    y_prompt = rms_norm(xp, final_g)
    y_sample = rms_norm(xs, final_g)
    new_k_prompt = jnp.stack(nkp)
    new_v_prompt = jnp.stack(nvp)
    new_k_sample = jnp.stack(nks)
    new_v_sample = jnp.stack(nvs)
    new_gm_v_sample = jnp.stack(ngs)
    return (y_prompt, y_sample, new_k_prompt, new_v_prompt, new_k_sample, new_v_sample, new_gm_v_sample)
```

```python
import functools
import math

import jax
import jax.numpy as jnp
from jax import lax
from jax.experimental import pallas as pl
from jax.experimental.pallas import tpu as pltpu

F32 = jnp.float32
BF16 = jnp.bfloat16

D_MODEL = 1024
DA_WIDTH = 512
DA_HEADS = 4
HEAD_LANES = 128
DA_HEAD_DIM = 64
GM_WIDTH = 512
GM_HEADS = 4
CHUNK = 128
IN_WIDTH = 3 * DA_WIDTH + 2 * GM_WIDTH
ROPE_THETA = 500000.0
ROPE_DIM = 16
PAGE_SIZE = 128
PEER_HEADS = 8
PEER_N_KEYS = 128
PEER_N_EXPERTS = PEER_N_KEYS * PEER_N_KEYS
PEER_HALF = 128
PEER_TOPK = 16
NORM_EPS = 1e-6
NEG_INF = float("-inf")

VMEM_LIMIT_BYTES = 56 * 1024 * 1024

ROW_TILE = 512
ATTN_Q_TILE = 256
TOPK_LANES = 128
PEER_EXPERT_TILE = 1024
PEER_TOKEN_TILE = 1024


def _params(*semantics):
    return pltpu.CompilerParams(dimension_semantics=semantics, vmem_limit_bytes=VMEM_LIMIT_BYTES)


def _rms(x, g):
    return x * lax.rsqrt(jnp.mean(x * x, axis=-1, keepdims=True) + NORM_EPS) * g


def _gelu(x):
    return x * (0.5 * (1.0 + jnp.tanh(math.sqrt(2.0 / math.pi) * (x + 0.044715 * (x * x * x)))))


def _const_spec(shape):
    zeros = (0,) * len(shape)
    return pl.BlockSpec(shape, lambda *_: zeros)


def _proj_kernel(x_ref, g1_ref, win_ref, cos_ref, sa_ref, sb_ref, gmg_ref, wmix_ref, bias_ref,
                 q1_ref, q2_ref, k32_ref, v32_ref, kb_ref, vb_ref, gmo_ref, gn_ref, *, mix_rows):
    rows = x_ref.shape[0]
    xn = _rms(x_ref[...], g1_ref[...]).astype(BF16)
    z = jnp.dot(xn, win_ref[...], preferred_element_type=F32)
    cos_t, sin_a, sin_b = cos_ref[...], sa_ref[...], sb_ref[...]
    first_half = lax.broadcasted_iota(jnp.int32, (rows, HEAD_LANES), 1) < DA_HEAD_DIM

    def rope(t):
        return t * cos_t + pltpu.roll(t, HEAD_LANES - 8, 1) * sin_a + pltpu.roll(t, 8, 1) * sin_b

    for b in range(DA_HEADS):
        sl = slice(b * HEAD_LANES, (b + 1) * HEAD_LANES)
        q = rope(z[:, sl]) * (1.0 / math.sqrt(DA_HEAD_DIM))
        q1_ref[:, sl] = jnp.where(first_half, q, 0.0).astype(BF16)
        q2_ref[:, sl] = jnp.where(first_half, 0.0, q).astype(BF16)
        k = rope(z[:, DA_WIDTH + b * HEAD_LANES:DA_WIDTH + (b + 1) * HEAD_LANES])
        k32_ref[:, sl] = k
        kb_ref[:, sl] = k.astype(BF16)
    v = z[:, 2 * DA_WIDTH:3 * DA_WIDTH]
    v32_ref[...] = v
    vb_ref[...] = v.astype(BF16)

    u = _gelu(z[:, 3 * DA_WIDTH:3 * DA_WIDTH + GM_WIDTH])
    g = _gelu(z[:, 3 * DA_WIDTH + GM_WIDTH:])
    for b in range(GM_HEADS):
        sl = slice(b * HEAD_LANES, (b + 1) * HEAD_LANES)
        gn = _rms(g[:, sl], gmg_ref[:, sl])
        gn_ref[:, sl] = gn
        gnb = gn.astype(BF16)
        for r in range(rows // mix_rows):
            rs = slice(r * mix_rows, (r + 1) * mix_rows)
            s = jnp.dot(wmix_ref[b], gnb[rs], preferred_element_type=F32) + bias_ref[:, sl]
            gmo_ref[rs, sl] = (u[rs, sl] * s).astype(BF16)


def _proj(x, g1, w_in, cos_t, sin_a, sin_b, gm_g, wmix, bias):
    rows = x.shape[0]
    tile = min(ROW_TILE, rows)
    mix_rows = wmix.shape[-1]
    row_spec = lambda w: pl.BlockSpec((tile, w), lambda i: (i, 0))
    out_shape = [jax.ShapeDtypeStruct((rows, DA_WIDTH), BF16),
                 jax.ShapeDtypeStruct((rows, DA_WIDTH), BF16),
                 jax.ShapeDtypeStruct((rows, DA_WIDTH), F32),
                 jax.ShapeDtypeStruct((rows, DA_WIDTH), F32),
                 jax.ShapeDtypeStruct((rows, DA_WIDTH), BF16),
                 jax.ShapeDtypeStruct((rows, DA_WIDTH), BF16),
                 jax.ShapeDtypeStruct((rows, GM_WIDTH), BF16),
                 jax.ShapeDtypeStruct((rows, GM_WIDTH), F32)]
    return pl.pallas_call(
        functools.partial(_proj_kernel, mix_rows=mix_rows),
        grid=(rows // tile,),
        in_specs=[row_spec(D_MODEL), _const_spec((1, D_MODEL)), _const_spec((D_MODEL, IN_WIDTH)),
                  row_spec(HEAD_LANES), row_spec(HEAD_LANES), row_spec(HEAD_LANES),
                  _const_spec((1, GM_WIDTH)), _const_spec(wmix.shape), _const_spec(bias.shape)],
        out_specs=[row_spec(DA_WIDTH)] * 6 + [row_spec(GM_WIDTH)] * 2,
        out_shape=out_shape,
        compiler_params=_params("parallel"),
        name="proj",
    )(x, g1, w_in, cos_t, sin_a, sin_b, gm_g, wmix, bias)


def _attn_kernel(lam_ref, q1_ref, q2_ref, k_ref, v_ref, o_ref):
    tq = q1_ref.shape[0]
    i = pl.program_id(2)
    q = jnp.concatenate([q1_ref[...], q2_ref[...]], axis=0)

    def step(j, carry, masked):
        m, l, acc = carry
        start = pl.multiple_of(j * tq, tq)
        kb = k_ref[pl.ds(start, tq), :]
        vb = v_ref[pl.ds(start, tq), :]
        s = lax.dot_general(q, kb, (((1,), (1,)), ((), ())), preferred_element_type=F32)
        if masked:
            row = lax.broadcasted_iota(jnp.int32, (2 * tq, tq), 0)
            col = lax.broadcasted_iota(jnp.int32, (2 * tq, tq), 1)
            row = jnp.where(row >= tq, row - tq, row)
            s = jnp.where(col <= row, s, NEG_INF)
        m_new = jnp.maximum(m, jnp.max(s, axis=-1, keepdims=True))
        alpha = jnp.exp(m - m_new)
        p = jnp.exp(s - m_new)
        l = alpha * l + jnp.sum(p, axis=-1, keepdims=True)
        acc = alpha * acc + jnp.dot(p.astype(BF16), vb, preferred_element_type=F32)
        return m_new, l, acc

    init = (jnp.full((2 * tq, 1), NEG_INF, F32), jnp.zeros((2 * tq, 1), F32),
            jnp.zeros((2 * tq, HEAD_LANES), F32))
    carry = lax.fori_loop(0, i, lambda j, c: step(j, c, False), init)
    _, l, acc = step(i, carry, True)
    o = acc / l
    o_ref[...] = o[:tq] - lam_ref[0] * o[tq:]


def _prompt_attention(lam, q1, q2, kb, vb, batch, seq):
    tq = min(ATTN_Q_TILE, seq)
    nq = seq // tq
    q_spec = pl.BlockSpec((tq, HEAD_LANES), lambda b, h, i: (b * nq + i, h))
    kv_spec = pl.BlockSpec((seq, HEAD_LANES), lambda b, h, i: (b, h))
    return pl.pallas_call(
        _attn_kernel,
        grid=(batch, DA_HEADS, nq),
        in_specs=[pl.BlockSpec(memory_space=pltpu.SMEM), q_spec, q_spec, kv_spec, kv_spec],
        out_specs=q_spec,
        out_shape=jax.ShapeDtypeStruct((batch * seq, DA_WIDTH), F32),
        compiler_params=_params("parallel", "parallel", "arbitrary"),
        name="prompt_attn",
    )(lam, q1, q2, kb, vb)


def _decode_attn_kernel(pt_ref, lam_ref, q_ref, kn_ref, vn_ref, *rest, n_pages, n_new):
    del pt_ref
    k_pages, v_pages = rest[:n_pages], rest[n_pages:2 * n_pages]
    o_ref, k_all, v_all = rest[2 * n_pages:]
    for p in range(n_pages):
        rs = slice(p * PAGE_SIZE, (p + 1) * PAGE_SIZE)
        k_all[rs, :] = k_pages[p][0].astype(BF16)
        v_all[rs, :] = v_pages[p][0].astype(BF16)
    q = q_ref[0]
    nrow = q.shape[0]
    contract_last = (((1,), (1,)), ((), ()))
    s_past = lax.dot_general(q, k_all[...], contract_last, preferred_element_type=F32)
    s_new = lax.dot_general(q, kn_ref[0], contract_last, preferred_element_type=F32)
    tok = lax.broadcasted_iota(jnp.int32, s_new.shape, 0) % n_new
    col = lax.broadcasted_iota(jnp.int32, s_new.shape, 1)
    s_new = jnp.where(col <= tok, s_new, NEG_INF)
    m = jnp.maximum(jnp.max(s_past, axis=-1, keepdims=True), jnp.max(s_new, axis=-1, keepdims=True))
    p_past = jnp.exp(s_past - m)
    p_new = jnp.exp(s_new - m)
    inv = 1.0 / (jnp.sum(p_past, axis=-1, keepdims=True) + jnp.sum(p_new, axis=-1, keepdims=True))
    half = nrow // 2
    lam = lam_ref[0]
    a_past = (p_past[:half] * inv[:half] - lam * (p_past[half:] * inv[half:])).astype(BF16)
    a_new = (p_new[:half] * inv[:half] - lam * (p_new[half:] * inv[half:])).astype(BF16)
    o = (jnp.dot(a_past, v_all[...], preferred_element_type=F32)
         + jnp.dot(a_new, vn_ref[0], preferred_element_type=F32))
    head_of_row = lax.broadcasted_iota(jnp.int32, o.shape, 0) // n_new
    head_of_lane = lax.broadcasted_iota(jnp.int32, o.shape, 1) // HEAD_LANES
    o = jnp.where(head_of_row == head_of_lane, o, 0.0)
    out = o[0:n_new]
    for h in range(1, DA_HEADS):
        out = out + o[h * n_new:(h + 1) * n_new]
    o_ref[0] = out


def _decode_attention(page_table, lam, q_rows, k_new, v_new, cache_k, cache_v):
    nb, n_pages = page_table.shape
    n_new = k_new.shape[1] // 2
    nrow = q_rows.shape[1]
    page_spec = lambda p: pl.BlockSpec((1, PAGE_SIZE, DA_WIDTH), lambda b, pt: (pt[b * n_pages + p], 0, 0))
    per_b = lambda r: pl.BlockSpec((1, r, DA_WIDTH), lambda b, pt: (b, 0, 0))
    grid_spec = pltpu.PrefetchScalarGridSpec(
        num_scalar_prefetch=1,
        grid=(nb,),
        in_specs=[pl.BlockSpec(memory_space=pltpu.SMEM), per_b(nrow), per_b(2 * n_new), per_b(2 * n_new)]
                 + [page_spec(p) for p in range(n_pages)] * 2,
        out_specs=per_b(n_new),
        scratch_shapes=[pltpu.VMEM((n_pages * PAGE_SIZE, DA_WIDTH), BF16)] * 2,
    )
    return pl.pallas_call(
        functools.partial(_decode_attn_kernel, n_pages=n_pages, n_new=n_new),
        grid_spec=grid_spec,
        out_shape=jax.ShapeDtypeStruct((nb, n_new, DA_WIDTH), F32),
        compiler_params=_params("arbitrary"),
        name="decode_attn",
    )(page_table.reshape(-1), lam, q_rows, k_new, v_new,
      *([cache_k] * n_pages), *([cache_v] * n_pages))


def _merge_kernel(ao_ref, gmo_ref, x_ref, sg_ref, wo_ref, g2_ref, wpq_ref, kbig_ref,
                  h_ref, tn_ref, s_ref, *, attn_scale):
    parts = []
    for b in range(DA_HEADS):
        sl = slice(b * HEAD_LANES, (b + 1) * HEAD_LANES)
        parts.append((_rms(ao_ref[:, sl], sg_ref[...]) * attn_scale).astype(BF16))
    cat = jnp.concatenate(parts + [gmo_ref[...]], axis=1)
    h = x_ref[...] + jnp.dot(cat, wo_ref[...], preferred_element_type=F32)
    h_ref[...] = h
    tn = _rms(h, g2_ref[...]).astype(BF16)
    tn_ref[...] = tn
    qp = jnp.dot(tn, wpq_ref[...], preferred_element_type=F32).astype(BF16)
    width = PEER_HEADS * PEER_HALF
    for c in range(2):
        s_ref[c] = lax.dot_general(kbig_ref[c], qp[:, c * width:(c + 1) * width],
                                   (((1,), (1,)), ((), ())), preferred_element_type=F32)


def _merge(ao, gmo, x, subln_g, w_o, g2, wpq, kbig, attn_scale):
    rows = x.shape[0]
    tile = min(ROW_TILE, rows)
    width = PEER_HEADS * PEER_HALF
    row_spec = lambda w: pl.BlockSpec((tile, w), lambda i: (i, 0))
    return pl.pallas_call(
        functools.partial(_merge_kernel, attn_scale=attn_scale),
        grid=(rows // tile,),
        in_specs=[row_spec(DA_WIDTH), row_spec(GM_WIDTH), row_spec(D_MODEL), _const_spec((1, HEAD_LANES)),
                  _const_spec((D_MODEL, D_MODEL)), _const_spec((1, D_MODEL)),
                  _const_spec((D_MODEL, 2 * width)), _const_spec((2, PEER_N_KEYS * PEER_HEADS, width))],
        out_specs=[row_spec(D_MODEL), row_spec(D_MODEL),
                   pl.BlockSpec((2, PEER_N_KEYS * PEER_HEADS, tile), lambda i: (0, 0, i))],
        out_shape=[jax.ShapeDtypeStruct((rows, D_MODEL), F32),
                   jax.ShapeDtypeStruct((rows, D_MODEL), BF16),
                   jax.ShapeDtypeStruct((2, PEER_N_KEYS * PEER_HEADS, rows), F32)],
        compiler_params=_params("parallel"),
        name="merge",
    )(ao, gmo, x, subln_g, w_o, g2, wpq, kbig)


def _young_candidates():
    return [(p, q) for p in range(PEER_TOPK) for q in range(PEER_TOPK) if (p + 1) * (q + 1) <= PEER_TOPK]


def _topk_kernel(s_ref, r2_ref, e2_ref, n_ref, c_ref, work_ref, top_ref, idx_ref):
    nk = PEER_N_KEYS
    tile = s_ref.shape[2:]

    def extract(c):
        for j in range(nk):
            work_ref[j] = s_ref[c, j]

        def round_(r, _):
            vals = [work_ref[j] for j in range(nk)]
            idxs = [None] * nk
            width = 1
            while len(vals) > 1:
                nv, ni = [], []
                for t in range(0, len(vals), 2):
                    a, b = vals[t], vals[t + 1]
                    take_b = b > a
                    ia = idxs[t] if idxs[t] is not None else float(t * width)
                    ib = idxs[t + 1] if idxs[t + 1] is not None else float((t + 1) * width)
                    nv.append(jnp.maximum(a, b))
                    ni.append(jnp.where(take_b, ib, ia))
                vals, idxs = nv, ni
                width *= 2
            best, best_idx = vals[0], idxs[0]
            top_ref[c, r] = best
            idx_ref[c, r] = best_idx
            for j in range(nk):
                work_ref[j] = jnp.where(best_idx == float(j), NEG_INF, work_ref[j])
            return 0

        lax.fori_loop(0, PEER_TOPK, round_, 0)

    extract(0)
    extract(1)

    a = [top_ref[0, p] for p in range(PEER_TOPK)]
    b = [top_ref[1, q] for q in range(PEER_TOPK)]
    cands = _young_candidates()
    cand = {pq: a[pq[0]] + b[pq[1]] for pq in cands}
    one, zero = jnp.ones(tile, F32), jnp.zeros(tile, F32)
    n_of_p = [zero] * PEER_TOPK
    ea = [jnp.exp(a[p] - a[0]) for p in range(PEER_TOPK)]
    eb = [jnp.exp(b[q] - b[0]) for q in range(PEER_TOPK)]
    z = zero
    for (p, q) in cands:
        fixed = 0
        beaten = zero
        for (p2, q2) in cands:
            if (p2, q2) == (p, q):
                continue
            if p2 <= p and q2 <= q:
                fixed += 1
            elif p2 >= p and q2 >= q:
                continue
            elif p2 * PEER_TOPK + q2 < p * PEER_TOPK + q:
                beaten = beaten + jnp.where(cand[(p2, q2)] >= cand[(p, q)], one, zero)
            else:
                beaten = beaten + jnp.where(cand[(p2, q2)] > cand[(p, q)], one, zero)
        sel = jnp.where(beaten + float(fixed) < float(PEER_TOPK), one, zero)
        n_of_p[p] = n_of_p[p] + sel
        z = z + sel * (ea[p] * eb[q])
    inv_z = 1.0 / z

    idx_a = [idx_ref[0, p] for p in range(PEER_TOPK)]
    idx_b = [idx_ref[1, q] for q in range(PEER_TOPK)]
    for j in range(nk):
        n_j = zero
        r_j = jnp.full(tile, float(nk - 1), F32)
        for p in range(PEER_TOPK):
            n_j = jnp.where(idx_a[p] == float(j), n_of_p[p], n_j)
            r_j = jnp.where(idx_b[p] == float(j), float(p), r_j)
        n_ref[j] = n_j
        r2_ref[j] = r_j
        c_ref[j] = jnp.exp(s_ref[0, j] - a[0]) * inv_z
        e2_ref[j] = jnp.exp(s_ref[1, j] - b[0])


def _topk(scores):
    tokens = scores.shape[-1]
    lanes = min(TOPK_LANES, tokens)
    tab = jax.ShapeDtypeStruct((PEER_N_KEYS, PEER_HEADS, tokens), F32)
    tab_spec = pl.BlockSpec((PEER_N_KEYS, PEER_HEADS, lanes), lambda i: (0, 0, i))
    return pl.pallas_call(
        _topk_kernel,
        grid=(tokens // lanes,),
        in_specs=[pl.BlockSpec((2, PEER_N_KEYS, PEER_HEADS, lanes), lambda i: (0, 0, 0, i))],
        out_specs=[tab_spec] * 4,
        out_shape=[tab] * 4,
        scratch_shapes=[pltpu.VMEM((PEER_N_KEYS, PEER_HEADS, lanes), F32),
                        pltpu.VMEM((2, PEER_TOPK, PEER_HEADS, lanes), F32),
                        pltpu.VMEM((2, PEER_TOPK, PEER_HEADS, lanes), F32)],
        compiler_params=_params("parallel"),
        name="peer_topk",
    )(scores)


def _peer_kernel(tn_ref, pu_ref, pvt_ref, r2_ref, e2_ref, n_ref, c_ref, h_ref, gf_ref,
                 y_ref, acc_ref, at_ref, hid_ref):
    e = pl.program_id(1)
    n_e = pl.num_programs(1)
    rows_per_key = PEER_N_KEYS
    keys_per_step = pu_ref.shape[0] // rows_per_key

    @pl.when(e == 0)
    def _():
        acc_ref[...] = jnp.zeros_like(acc_ref)

    at_ref[...] = lax.dot_general(pu_ref[...], tn_ref[...], (((1,), (1,)), ((), ())),
                                  preferred_element_type=F32)
    for ii in range(keys_per_step):
        rs = slice(ii * rows_per_key, (ii + 1) * rows_per_key)
        gate = jnp.zeros((rows_per_key, tn_ref.shape[0]), BF16)
        for h in range(PEER_HEADS):
            n = n_ref[h, ii:ii + 1, :].astype(BF16)
            c = c_ref[h, ii:ii + 1, :].astype(BF16)
            gate = gate + jnp.where(r2_ref[h] < n, e2_ref[h] * c, jnp.zeros_like(gate))
        hid_ref[rs, :] = gate * _gelu(at_ref[rs, :]).astype(BF16)
    acc_ref[...] += jnp.dot(pvt_ref[...], hid_ref[...], preferred_element_type=F32)

    @pl.when(e == n_e - 1)
    def _():
        y = h_ref[...] + jnp.transpose(acc_ref[...])
        y_ref[...] = _rms(y, gf_ref[...])


def _peer(tn, pu, pvt, r2, e2, ntab, ctab, h, final_g):
    tokens = tn.shape[0]
    tt = min(PEER_TOKEN_TILE, tokens)
    et = PEER_EXPERT_TILE
    keys_per_step = et // PEER_N_KEYS
    tab16 = pl.BlockSpec((PEER_HEADS, PEER_N_KEYS, tt), lambda t, e: (0, 0, t))
    tab32 = pl.BlockSpec((PEER_HEADS, keys_per_step, tt), lambda t, e: (0, e, t))
    tok_spec = pl.BlockSpec((tt, D_MODEL), lambda t, e: (t, 0))
    return pl.pallas_call(
        _peer_kernel,
        grid=(tokens // tt, PEER_N_EXPERTS // et),
        in_specs=[tok_spec,
                  pl.BlockSpec((et, D_MODEL), lambda t, e: (e, 0)),
                  pl.BlockSpec((D_MODEL, et), lambda t, e: (0, e)),
                  tab16, tab16, tab32, tab32, tok_spec, _const_spec((1, D_MODEL))],
        out_specs=tok_spec,
        out_shape=jax.ShapeDtypeStruct((tokens, D_MODEL), F32),
        scratch_shapes=[pltpu.VMEM((D_MODEL, tt), F32), pltpu.VMEM((et, tt), F32), pltpu.VMEM((et, tt), BF16)],
        compiler_params=_params("parallel", "arbitrary"),
        name="peer_experts",
    )(tn, pu, pvt, r2, e2, ntab, ctab, h, final_g)


def _rope_tables(pos):
    half = ROPE_DIM // 2
    inv = ROPE_THETA ** (-jnp.arange(half, dtype=F32) * 2.0 / ROPE_DIM)
    d = jnp.arange(HEAD_LANES) % DA_HEAD_DIM
    ang = pos[:, None] * inv[d % half][None, :]
    cos_t = jnp.where(d < ROPE_DIM, jnp.cos(ang), 1.0)
    sin_a = jnp.where(d < half, -jnp.sin(ang), 0.0)
    sin_b = jnp.where((d >= half) & (d < ROPE_DIM), jnp.sin(ang), 0.0)
    return cos_t, sin_a, sin_b


def _row_pipeline(x, rope_tabs, wmix, bias, attend, weights):
    (g1, w_in, gm_g, subln_g, w_o, g2, wpq, kbig, pu, pvt, final_g, attn_scale) = weights
    q1, q2, k32, v32, kb, vb, gmo, gn = _proj(x, g1, w_in, *rope_tabs, gm_g, wmix, bias)
    ao = attend(q1, q2, k32, v32, kb, vb)
    h, tn, scores = _merge(ao, gmo, x, subln_g, w_o, g2, wpq, kbig, attn_scale)
    rows = x.shape[0]
    r2, e2, ntab, ctab = _topk(scores.reshape(2, PEER_N_KEYS, PEER_HEADS, rows))
    to_head_major = lambda t, dt: jnp.transpose(t, (1, 0, 2)).astype(dt)
    y = _peer(tn, pu, pvt, to_head_major(r2, BF16), to_head_major(e2, BF16),
              to_head_major(ntab, F32), to_head_major(ctab, F32), h, final_g)
    return y, k32, v32, gn


def kernel(x_prompt, x_sample, cache_k, cache_v, page_table, norm1_g, w_in, gm_norm_g, gm_ws, gm_b,
           da_lq1, da_lk1, da_lq2, da_lk2, da_subln_g, w_o, norm2_g, peer_wq, peer_keys, peer_u,
           peer_v, final_g):
    batch, seq, _ = x_prompt.shape
    nb, n_new, _ = x_sample.shape
    n_pages = page_table.shape[1]
    past = n_pages * PAGE_SIZE
    assert w_in.shape[0] == 1 and seq % CHUNK == 0 and n_new <= 4

    lam_init = 0.8 - 0.6 * math.exp(-0.3 * 0)
    dots = lambda a, b: jnp.exp(jnp.sum(a.astype(F32) * b.astype(F32)))
    lam = (dots(da_lq1[0], da_lk1[0]) - dots(da_lq2[0], da_lk2[0]) + lam_init).reshape(1).astype(F32)

    eye = jnp.eye(PEER_HEADS, dtype=F32)
    kbig = jnp.einsum("hcjd,hg->cjhgd", peer_keys[0], eye).reshape(
        2, PEER_N_KEYS * PEER_HEADS, PEER_HEADS * PEER_HALF).astype(BF16)
    wpq = peer_wq[0].reshape(D_MODEL, PEER_HEADS, 2, PEER_HALF).transpose(0, 2, 1, 3).reshape(
        D_MODEL, 2 * PEER_HEADS * PEER_HALF).astype(BF16)
    weights = (norm1_g[0][None], w_in[0].astype(BF16), gm_norm_g[0].reshape(1, GM_WIDTH),
               da_subln_g[0][None], w_o[0].astype(BF16), norm2_g[0][None], wpq, kbig,
               peer_u[0].astype(BF16), peer_v[0].T.astype(BF16), final_g[None], 1.0 - lam_init)

    xp = x_prompt.reshape(batch * seq, D_MODEL)
    tabs_p = _rope_tables(jnp.tile(jnp.arange(seq, dtype=F32), batch))
    wmix_p = jnp.tril(gm_ws[0]).astype(BF16)
    bias_p = jnp.repeat(jnp.transpose(gm_b[0]), HEAD_LANES, axis=1)
    attend_p = lambda q1, q2, k32, v32, kb, vb: _prompt_attention(lam, q1, q2, kb, vb, batch, seq)
    y_p, k_p, v_p, _ = _row_pipeline(xp, tabs_p, wmix_p, bias_p, attend_p, weights)

    xs = x_sample.reshape(nb * n_new, D_MODEL)
    tabs_s = _rope_tables(jnp.tile(past + jnp.arange(n_new, dtype=F32), nb))
    rows_s = min(ROW_TILE, nb * n_new)
    w_new = jnp.tril(gm_ws[0][:, :n_new, :n_new])
    wmix_s = jnp.einsum("ab,hij->haibj", jnp.eye(rows_s // n_new, dtype=F32), w_new).reshape(
        GM_HEADS, rows_s, rows_s).astype(BF16)
    bias_s = jnp.tile(jnp.repeat(jnp.transpose(gm_b[0][:, :n_new]), HEAD_LANES, axis=1), (rows_s // n_new, 1))
    ck = cache_k[0].reshape(-1, PAGE_SIZE, DA_WIDTH)
    cv = cache_v[0].reshape(-1, PAGE_SIZE, DA_WIDTH)

    def attend_s(q1, q2, k32, v32, kb, vb):
        head_of_lane = jnp.arange(DA_WIDTH) // HEAD_LANES
        keep = (head_of_lane[None, :] == jnp.arange(DA_HEADS)[:, None])[None, None, :, None, :]
        q = jnp.stack([q1, q2], axis=0).reshape(2, nb, 1, n_new, DA_WIDTH)
        q = jnp.transpose(jnp.where(keep, q, 0), (1, 0, 2, 3, 4)).reshape(nb, 2 * DA_HEADS * n_new, DA_WIDTH)
        pad = lambda t: jnp.pad(t.reshape(nb, n_new, DA_WIDTH), ((0, 0), (0, n_new), (0, 0)))
        ao = _decode_attention(page_table, lam, q, pad(kb), pad(vb), ck, cv)
        return ao.reshape(nb * n_new, DA_WIDTH)

    y_s, k_s, v_s, gn_s = _row_pipeline(xs, tabs_s, wmix_s, bias_s, attend_s, weights)

    head_shape = lambda b, s: (1, b, s, DA_HEADS, HEAD_LANES)
    return (y_p.reshape(batch, seq, D_MODEL), y_s.reshape(nb, n_new, D_MODEL),
            k_p.reshape(head_shape(batch, seq)), v_p.reshape(head_shape(batch, seq)),
            k_s.reshape(head_shape(nb, n_new)), v_s.reshape(head_shape(nb, n_new)),
            gn_s.reshape(1, nb, n_new, GM_HEADS, HEAD_LANES))
```

```python
import functools
import math

import jax
import jax.numpy as jnp
from jax import lax
from jax.experimental import pallas as pl
from jax.experimental.pallas import tpu as pltpu

F32 = jnp.float32
BF16 = jnp.bfloat16

D_MODEL = 1024
DA_WIDTH = 512
DA_HEADS = 4
HEAD_LANES = 128
DA_HEAD_DIM = 64
GM_WIDTH = 512
GM_HEADS = 4
CHUNK = 128
IN_WIDTH = 3 * DA_WIDTH + 2 * GM_WIDTH
ROPE_THETA = 500000.0
ROPE_DIM = 16
PAGE_SIZE = 128
PEER_HEADS = 8
PEER_N_KEYS = 128
PEER_N_EXPERTS = PEER_N_KEYS * PEER_N_KEYS
PEER_HALF = 128
PEER_TOPK = 16
NORM_EPS = 1e-6
NEG_INF = float("-inf")

VMEM_LIMIT_BYTES = 56 * 1024 * 1024

ROW_TILE = 512
ATTN_Q_TILE = 256
TOPK_LANES = 128
PEER_EXPERT_TILE = 1024
PEER_TOKEN_TILE = 1024


def _params(*semantics):
    return pltpu.CompilerParams(dimension_semantics=semantics, vmem_limit_bytes=VMEM_LIMIT_BYTES)


def _rms(x, g):
    return x * lax.rsqrt(jnp.mean(x * x, axis=-1, keepdims=True) + NORM_EPS) * g


def _gelu(x):
    return x * (0.5 * (1.0 + jnp.tanh(math.sqrt(2.0 / math.pi) * (x + 0.044715 * (x * x * x)))))


def _const_spec(shape):
    zeros = (0,) * len(shape)
    return pl.BlockSpec(shape, lambda *_: zeros)


def _proj_kernel(x_ref, g1_ref, win_ref, cos_ref, sa_ref, sb_ref, gmg_ref, wmix_ref, bias_ref,
                 q1_ref, q2_ref, k32_ref, v32_ref, kb_ref, vb_ref, gmo_ref, gn_ref, *, mix_rows):
    rows = x_ref.shape[0]
    xn = _rms(x_ref[...], g1_ref[...]).astype(BF16)
    z = jnp.dot(xn, win_ref[...], preferred_element_type=F32)
    cos_t, sin_a, sin_b = cos_ref[...], sa_ref[...], sb_ref[...]
    first_half = lax.broadcasted_iota(jnp.int32, (rows, HEAD_LANES), 1) < DA_HEAD_DIM

    def rope(t):
        return t * cos_t + pltpu.roll(t, HEAD_LANES - 8, 1) * sin_a + pltpu.roll(t, 8, 1) * sin_b

    for b in range(DA_HEADS):
        sl = slice(b * HEAD_LANES, (b + 1) * HEAD_LANES)
        q = rope(z[:, sl]) * (1.0 / math.sqrt(DA_HEAD_DIM))
        q1_ref[:, sl] = jnp.where(first_half, q, 0.0).astype(BF16)
        q2_ref[:, sl] = jnp.where(first_half, 0.0, q).astype(BF16)
        k = rope(z[:, DA_WIDTH + b * HEAD_LANES:DA_WIDTH + (b + 1) * HEAD_LANES])
        v = z[:, 2 * DA_WIDTH + b * HEAD_LANES:2 * DA_WIDTH + (b + 1) * HEAD_LANES]
        k32_ref[pl.ds(b, rows, stride=DA_HEADS), :] = k
        v32_ref[pl.ds(b, rows, stride=DA_HEADS), :] = v
        kb_ref[:, sl] = k.astype(BF16)
        vb_ref[:, sl] = v.astype(BF16)

    u = _gelu(z[:, 3 * DA_WIDTH:3 * DA_WIDTH + GM_WIDTH])
    g = _gelu(z[:, 3 * DA_WIDTH + GM_WIDTH:])
    for b in range(GM_HEADS):
        sl = slice(b * HEAD_LANES, (b + 1) * HEAD_LANES)
        gn = _rms(g[:, sl], gmg_ref[:, sl])
        gn_ref[:, sl] = gn
        gnb = gn.astype(BF16)
        for r in range(rows // mix_rows):
            rs = slice(r * mix_rows, (r + 1) * mix_rows)
            s = jnp.dot(wmix_ref[b], gnb[rs], preferred_element_type=F32) + bias_ref[:, sl]
            gmo_ref[rs, sl] = (u[rs, sl] * s).astype(BF16)


def _proj(x, g1, w_in, cos_t, sin_a, sin_b, gm_g, wmix, bias):
    rows = x.shape[0]
    tile = min(ROW_TILE, rows)
    mix_rows = wmix.shape[-1]
    row_spec = lambda w: pl.BlockSpec((tile, w), lambda i: (i, 0))
    table_blocks = cos_t.shape[0] // tile
    table_spec = pl.BlockSpec((tile, HEAD_LANES), lambda i: (i % table_blocks, 0))
    head_rows_spec = pl.BlockSpec((tile * DA_HEADS, HEAD_LANES), lambda i: (i, 0))
    out_shape = [jax.ShapeDtypeStruct((rows, DA_WIDTH), BF16),
                 jax.ShapeDtypeStruct((rows, DA_WIDTH), BF16),
                 jax.ShapeDtypeStruct((rows * DA_HEADS, HEAD_LANES), F32),
                 jax.ShapeDtypeStruct((rows * DA_HEADS, HEAD_LANES), F32),
                 jax.ShapeDtypeStruct((rows, DA_WIDTH), BF16),
                 jax.ShapeDtypeStruct((rows, DA_WIDTH), BF16),
                 jax.ShapeDtypeStruct((rows, GM_WIDTH), BF16),
                 jax.ShapeDtypeStruct((rows, GM_WIDTH), F32)]
    return pl.pallas_call(
        functools.partial(_proj_kernel, mix_rows=mix_rows),
        grid=(rows // tile,),
        in_specs=[row_spec(D_MODEL), _const_spec((1, D_MODEL)), _const_spec((D_MODEL, IN_WIDTH)),
                  table_spec, table_spec, table_spec,
                  _const_spec((1, GM_WIDTH)), _const_spec(wmix.shape), _const_spec(bias.shape)],
        out_specs=[row_spec(DA_WIDTH)] * 2 + [head_rows_spec] * 2 + [row_spec(DA_WIDTH)] * 2
                  + [row_spec(GM_WIDTH)] * 2,
        out_shape=out_shape,
        compiler_params=_params("parallel"),
        name="proj",
    )(x, g1, w_in, cos_t, sin_a, sin_b, gm_g, wmix, bias)


def _attn_kernel(lam_ref, q1_ref, q2_ref, k_ref, v_ref, o_ref):
    tq = q1_ref.shape[0]
    i = pl.program_id(2)
    q = jnp.concatenate([q1_ref[...], q2_ref[...]], axis=0)

    def step(j, carry, masked):
        m, l, acc = carry
        start = pl.multiple_of(j * tq, tq)
        kb = k_ref[pl.ds(start, tq), :]
        vb = v_ref[pl.ds(start, tq), :]
        s = lax.dot_general(q, kb, (((1,), (1,)), ((), ())), preferred_element_type=F32)
        if masked:
            row = lax.broadcasted_iota(jnp.int32, (2 * tq, tq), 0)
            col = lax.broadcasted_iota(jnp.int32, (2 * tq, tq), 1)
            row = jnp.where(row >= tq, row - tq, row)
            s = jnp.where(col <= row, s, NEG_INF)
        m_new = jnp.maximum(m, jnp.max(s, axis=-1, keepdims=True))
        alpha = jnp.exp(m - m_new)
        p = jnp.exp(s - m_new)
        l = alpha * l + jnp.sum(p, axis=-1, keepdims=True)
        acc = alpha * acc + jnp.dot(p.astype(BF16), vb, preferred_element_type=F32)
        return m_new, l, acc

    init = (jnp.full((2 * tq, 1), NEG_INF, F32), jnp.zeros((2 * tq, 1), F32),
            jnp.zeros((2 * tq, HEAD_LANES), F32))
    carry = lax.fori_loop(0, i, lambda j, c: step(j, c, False), init)
    _, l, acc = step(i, carry, True)
    o = acc / l
    o_ref[...] = o[:tq] - lam_ref[0] * o[tq:]


def _prompt_attention(lam, q1, q2, kb, vb, batch, seq):
    tq = min(ATTN_Q_TILE, seq)
    nq = seq // tq
    q_spec = pl.BlockSpec((tq, HEAD_LANES), lambda b, h, i: (b * nq + i, h))
    kv_spec = pl.BlockSpec((seq, HEAD_LANES), lambda b, h, i: (b, h))
    return pl.pallas_call(
        _attn_kernel,
        grid=(batch, DA_HEADS, nq),
        in_specs=[pl.BlockSpec(memory_space=pltpu.SMEM), q_spec, q_spec, kv_spec, kv_spec],
        out_specs=q_spec,
        out_shape=jax.ShapeDtypeStruct((batch * seq, DA_WIDTH), F32),
        compiler_params=_params("parallel", "parallel", "arbitrary"),
        name="prompt_attn",
    )(lam, q1, q2, kb, vb)


def _decode_attn_kernel(pt_ref, lam_ref, q_ref, kn_ref, vn_ref, *rest, n_pages, n_new):
    del pt_ref
    k_pages, v_pages = rest[:n_pages], rest[n_pages:2 * n_pages]
    o_ref, k_all, v_all = rest[2 * n_pages:]
    page_rows = PAGE_SIZE * DA_HEADS
    for p in range(n_pages):
        rs = slice(p * page_rows, (p + 1) * page_rows)
        k_all[rs, :] = k_pages[p][0].astype(BF16)
        v_all[rs, :] = v_pages[p][0].astype(BF16)
    q = q_ref[0]
    nrow = q.shape[0]
    contract_last = (((1,), (1,)), ((), ()))
    s_past = lax.dot_general(q, k_all[...], contract_last, preferred_element_type=F32)
    s_new = lax.dot_general(q, kn_ref[0], contract_last, preferred_element_type=F32)

    def row_col(shape):
        row = lax.broadcasted_iota(jnp.int32, shape, 0)
        col = lax.broadcasted_iota(jnp.int32, shape, 1)
        return (row // n_new) % DA_HEADS, row % n_new, col % DA_HEADS, col // DA_HEADS

    q_head, _, k_head, _ = row_col(s_past.shape)
    s_past = jnp.where(q_head == k_head, s_past, NEG_INF)
    q_head, q_tok, k_head, k_tok = row_col(s_new.shape)
    s_new = jnp.where(q_head == k_head, jnp.where(k_tok <= q_tok, s_new, NEG_INF), NEG_INF)
    m = jnp.maximum(jnp.max(s_past, axis=-1, keepdims=True), jnp.max(s_new, axis=-1, keepdims=True))
    p_past = jnp.exp(s_past - m)
    p_new = jnp.exp(s_new - m)
    inv = 1.0 / (jnp.sum(p_past, axis=-1, keepdims=True) + jnp.sum(p_new, axis=-1, keepdims=True))
    half = nrow // 2
    lam = lam_ref[0]
    a_past = (p_past[:half] * inv[:half] - lam * (p_past[half:] * inv[half:])).astype(BF16)
    a_new = (p_new[:half] * inv[:half] - lam * (p_new[half:] * inv[half:])).astype(BF16)
    o_ref[0] = (jnp.dot(a_past, v_all[...], preferred_element_type=F32)
                + jnp.dot(a_new, vn_ref[0], preferred_element_type=F32))


def _decode_attention(page_table, lam, q_rows, k_new, v_new, cache_k, cache_v):
    nb, n_pages = page_table.shape
    n_new = k_new.shape[1] // DA_HEADS
    page_rows = PAGE_SIZE * DA_HEADS
    page_spec = lambda p: pl.BlockSpec((1, page_rows, HEAD_LANES), lambda b, pt: (pt[b * n_pages + p], 0, 0))
    per_b = lambda r: pl.BlockSpec((1, r, HEAD_LANES), lambda b, pt: (b, 0, 0))
    grid_spec = pltpu.PrefetchScalarGridSpec(
        num_scalar_prefetch=1,
        grid=(nb,),
        in_specs=[pl.BlockSpec(memory_space=pltpu.SMEM), per_b(q_rows.shape[1]),
                  per_b(n_new * DA_HEADS), per_b(n_new * DA_HEADS)]
                 + [page_spec(p) for p in range(n_pages)] * 2,
        out_specs=per_b(n_new * DA_HEADS),
        scratch_shapes=[pltpu.VMEM((n_pages * page_rows, HEAD_LANES), BF16)] * 2,
    )
    return pl.pallas_call(
        functools.partial(_decode_attn_kernel, n_pages=n_pages, n_new=n_new),
        grid_spec=grid_spec,
        out_shape=jax.ShapeDtypeStruct((nb, n_new * DA_HEADS, HEAD_LANES), F32),
        compiler_params=_params("arbitrary"),
        name="decode_attn",
    )(page_table.reshape(-1), lam, q_rows, k_new, v_new,
      *([cache_k] * n_pages), *([cache_v] * n_pages))


def _merge_kernel(ao_ref, gmo_ref, x_ref, sg_ref, wo_ref, g2_ref, wpq_ref, kbig_ref,
                  h_ref, tn_ref, s_ref, *, attn_scale):
    parts = []
    for b in range(DA_HEADS):
        sl = slice(b * HEAD_LANES, (b + 1) * HEAD_LANES)
        parts.append((_rms(ao_ref[:, sl], sg_ref[...]) * attn_scale).astype(BF16))
    cat = jnp.concatenate(parts + [gmo_ref[...]], axis=1)
    h = x_ref[...] + jnp.dot(cat, wo_ref[...], preferred_element_type=F32)
    h_ref[...] = h
    tn = _rms(h, g2_ref[...]).astype(BF16)
    tn_ref[...] = tn
    qp = jnp.dot(tn, wpq_ref[...], preferred_element_type=F32).astype(BF16)
    width = PEER_HEADS * PEER_HALF
    for c in range(2):
        s_ref[c] = lax.dot_general(kbig_ref[c], qp[:, c * width:(c + 1) * width],
                                   (((1,), (1,)), ((), ())), preferred_element_type=F32)


def _merge(ao, gmo, x, subln_g, w_o, g2, wpq, kbig, attn_scale):
    rows = x.shape[0]
    tile = min(ROW_TILE, rows)
    width = PEER_HEADS * PEER_HALF
    row_spec = lambda w: pl.BlockSpec((tile, w), lambda i: (i, 0))
    return pl.pallas_call(
        functools.partial(_merge_kernel, attn_scale=attn_scale),
        grid=(rows // tile,),
        in_specs=[row_spec(DA_WIDTH), row_spec(GM_WIDTH), row_spec(D_MODEL), _const_spec((1, HEAD_LANES)),
                  _const_spec((D_MODEL, D_MODEL)), _const_spec((1, D_MODEL)),
                  _const_spec((D_MODEL, 2 * width)), _const_spec((2, PEER_N_KEYS * PEER_HEADS, width))],
        out_specs=[row_spec(D_MODEL), row_spec(D_MODEL),
                   pl.BlockSpec((2, PEER_N_KEYS * PEER_HEADS, tile), lambda i: (0, 0, i))],
        out_shape=[jax.ShapeDtypeStruct((rows, D_MODEL), F32),
                   jax.ShapeDtypeStruct((rows, D_MODEL), BF16),
                   jax.ShapeDtypeStruct((2, PEER_N_KEYS * PEER_HEADS, rows), F32)],
        compiler_params=_params("parallel"),
        name="merge",
    )(ao, gmo, x, subln_g, w_o, g2, wpq, kbig)


def _young_candidates():
    return [(p, q) for p in range(PEER_TOPK) for q in range(PEER_TOPK) if (p + 1) * (q + 1) <= PEER_TOPK]


def _topk_kernel(s_ref, r2_ref, e2_ref, n_ref, c_ref, work_ref, top_ref, idx_ref):
    nk = PEER_N_KEYS
    tile = s_ref.shape[2:]

    def extract(c):
        for j in range(nk):
            work_ref[j] = s_ref[c, j]

        def round_(r, _):
            vals = [work_ref[j] for j in range(nk)]
            idxs = [None] * nk
            width = 1
            while len(vals) > 1:
                nv, ni = [], []
                for t in range(0, len(vals), 2):
                    a, b = vals[t], vals[t + 1]
                    take_b = b > a
                    ia = idxs[t] if idxs[t] is not None else float(t * width)
                    ib = idxs[t + 1] if idxs[t + 1] is not None else float((t + 1) * width)
                    nv.append(jnp.maximum(a, b))
                    ni.append(jnp.where(take_b, ib, ia))
                vals, idxs = nv, ni
                width *= 2
            best, best_idx = vals[0], idxs[0]
            top_ref[c, r] = best
            idx_ref[c, r] = best_idx
            for j in range(nk):
                work_ref[j] = jnp.where(best_idx == float(j), NEG_INF, work_ref[j])
            return 0

        lax.fori_loop(0, PEER_TOPK, round_, 0)

    extract(0)
    extract(1)

    a = [top_ref[0, p] for p in range(PEER_TOPK)]
    b = [top_ref[1, q] for q in range(PEER_TOPK)]
    cands = _young_candidates()
    cand = {pq: a[pq[0]] + b[pq[1]] for pq in cands}
    one, zero = jnp.ones(tile, F32), jnp.zeros(tile, F32)
    n_of_p = [zero] * PEER_TOPK
    ea = [jnp.exp(a[p] - a[0]) for p in range(PEER_TOPK)]
    eb = [jnp.exp(b[q] - b[0]) for q in range(PEER_TOPK)]
    z = zero
    for (p, q) in cands:
        fixed = 0
        beaten = zero
        for (p2, q2) in cands:
            if (p2, q2) == (p, q):
                continue
            if p2 <= p and q2 <= q:
                fixed += 1
            elif p2 >= p and q2 >= q:
                continue
            elif p2 * PEER_TOPK + q2 < p * PEER_TOPK + q:
                beaten = beaten + jnp.where(cand[(p2, q2)] >= cand[(p, q)], one, zero)
            else:
                beaten = beaten + jnp.where(cand[(p2, q2)] > cand[(p, q)], one, zero)
        sel = jnp.where(beaten + float(fixed) < float(PEER_TOPK), one, zero)
        n_of_p[p] = n_of_p[p] + sel
        z = z + sel * (ea[p] * eb[q])
    inv_z = 1.0 / z

    idx_a = [idx_ref[0, p] for p in range(PEER_TOPK)]
    idx_b = [idx_ref[1, q] for q in range(PEER_TOPK)]
    for j in range(nk):
        n_j = zero
        r_j = jnp.full(tile, float(nk - 1), F32)
        for p in range(PEER_TOPK):
            n_j = jnp.where(idx_a[p] == float(j), n_of_p[p], n_j)
            r_j = jnp.where(idx_b[p] == float(j), float(p), r_j)
        rows_of_key = pl.ds(j, PEER_HEADS, stride=nk)
        n_ref[rows_of_key, :] = n_j
        r2_ref[rows_of_key, :] = r_j
        c_ref[rows_of_key, :] = jnp.exp(s_ref[0, j] - a[0]) * inv_z
        e2_ref[rows_of_key, :] = jnp.exp(s_ref[1, j] - b[0])


def _topk(scores):
    tokens = scores.shape[-1]
    lanes = min(TOPK_LANES, tokens)
    tab = jax.ShapeDtypeStruct((PEER_HEADS * PEER_N_KEYS, tokens), F32)
    tab_spec = pl.BlockSpec((PEER_HEADS * PEER_N_KEYS, lanes), lambda i: (0, i))
    return pl.pallas_call(
        _topk_kernel,
        grid=(tokens // lanes,),
        in_specs=[pl.BlockSpec((2, PEER_N_KEYS, PEER_HEADS, lanes), lambda i: (0, 0, 0, i))],
        out_specs=[tab_spec] * 4,
        out_shape=[tab] * 4,
        scratch_shapes=[pltpu.VMEM((PEER_N_KEYS, PEER_HEADS, lanes), F32),
                        pltpu.VMEM((2, PEER_TOPK, PEER_HEADS, lanes), F32),
                        pltpu.VMEM((2, PEER_TOPK, PEER_HEADS, lanes), F32)],
        compiler_params=_params("parallel"),
        name="peer_topk",
    )(scores)


def _peer_kernel(tn_ref, pu_ref, pvt_ref, r2_ref, e2_ref, n_ref, c_ref, h_ref, gf_ref,
                 y_ref, acc_ref, at_ref, hid_ref):
    e = pl.program_id(1)
    n_e = pl.num_programs(1)
    rows_per_key = PEER_N_KEYS
    keys_per_step = pu_ref.shape[0] // rows_per_key

    @pl.when(e == 0)
    def _():
        acc_ref[...] = jnp.zeros_like(acc_ref)

    at_ref[...] = lax.dot_general(pu_ref[...], tn_ref[...], (((1,), (1,)), ((), ())),
                                  preferred_element_type=F32)
    for ii in range(keys_per_step):
        rs = slice(ii * rows_per_key, (ii + 1) * rows_per_key)
        gate = jnp.zeros((rows_per_key, tn_ref.shape[0]), BF16)
        for h in range(PEER_HEADS):
            n = n_ref[h, ii:ii + 1, :].astype(BF16)
            c = c_ref[h, ii:ii + 1, :].astype(BF16)
            gate = gate + jnp.where(r2_ref[h] < n, e2_ref[h] * c, jnp.zeros_like(gate))
        hid_ref[rs, :] = gate * _gelu(at_ref[rs, :]).astype(BF16)
    acc_ref[...] += jnp.dot(pvt_ref[...], hid_ref[...], preferred_element_type=F32)

    @pl.when(e == n_e - 1)
    def _():
        y = h_ref[...] + jnp.transpose(acc_ref[...])
        y_ref[...] = _rms(y, gf_ref[...])


def _peer(tn, pu, pvt, r2, e2, ntab, ctab, h, final_g):
    tokens = tn.shape[0]
    tt = min(PEER_TOKEN_TILE, tokens)
    et = PEER_EXPERT_TILE
    keys_per_step = et // PEER_N_KEYS
    tab16 = pl.BlockSpec((PEER_HEADS, PEER_N_KEYS, tt), lambda t, e: (0, 0, t))
    tab32 = pl.BlockSpec((PEER_HEADS, keys_per_step, tt), lambda t, e: (0, e, t))
    tok_spec = pl.BlockSpec((tt, D_MODEL), lambda t, e: (t, 0))
    return pl.pallas_call(
        _peer_kernel,
        grid=(tokens // tt, PEER_N_EXPERTS // et),
        in_specs=[tok_spec,
                  pl.BlockSpec((et, D_MODEL), lambda t, e: (e, 0)),
                  pl.BlockSpec((D_MODEL, et), lambda t, e: (0, e)),
                  tab16, tab16, tab32, tab32, tok_spec, _const_spec((1, D_MODEL))],
        out_specs=tok_spec,
        out_shape=jax.ShapeDtypeStruct((tokens, D_MODEL), F32),
        scratch_shapes=[pltpu.VMEM((D_MODEL, tt), F32), pltpu.VMEM((et, tt), F32), pltpu.VMEM((et, tt), BF16)],
        compiler_params=_params("parallel", "arbitrary"),
        name="peer_experts",
    )(tn, pu, pvt, r2, e2, ntab, ctab, h, final_g)


def _rope_tables(pos):
    half = ROPE_DIM // 2
    inv = ROPE_THETA ** (-jnp.arange(half, dtype=F32) * 2.0 / ROPE_DIM)
    d = jnp.arange(HEAD_LANES) % DA_HEAD_DIM
    ang = pos[:, None] * inv[d % half][None, :]
    cos_t = jnp.where(d < ROPE_DIM, jnp.cos(ang), 1.0)
    sin_a = jnp.where(d < half, -jnp.sin(ang), 0.0)
    sin_b = jnp.where((d >= half) & (d < ROPE_DIM), jnp.sin(ang), 0.0)
    return cos_t, sin_a, sin_b


def _row_pipeline(x, rope_tabs, wmix, bias, attend, weights):
    (g1, w_in, gm_g, subln_g, w_o, g2, wpq, kbig, pu, pvt, final_g, attn_scale) = weights
    q1, q2, k32, v32, kb, vb, gmo, gn = _proj(x, g1, w_in, *rope_tabs, gm_g, wmix, bias)
    ao = attend(q1, q2, k32, v32, kb, vb)
    h, tn, scores = _merge(ao, gmo, x, subln_g, w_o, g2, wpq, kbig, attn_scale)
    rows = x.shape[0]
    r2, e2, ntab, ctab = _topk(scores.reshape(2, PEER_N_KEYS, PEER_HEADS, rows))
    per_head = lambda t: t.reshape(PEER_HEADS, PEER_N_KEYS, rows)
    y = _peer(tn, pu, pvt, per_head(r2).astype(BF16), per_head(e2).astype(BF16),
              per_head(ntab), per_head(ctab), h, final_g)
    return y, k32, v32, gn


def kernel(x_prompt, x_sample, cache_k, cache_v, page_table, norm1_g, w_in, gm_norm_g, gm_ws, gm_b,
           da_lq1, da_lk1, da_lq2, da_lk2, da_subln_g, w_o, norm2_g, peer_wq, peer_keys, peer_u,
           peer_v, final_g):
    batch, seq, _ = x_prompt.shape
    nb, n_new, _ = x_sample.shape
    n_pages = page_table.shape[1]
    past = n_pages * PAGE_SIZE
    assert w_in.shape[0] == 1 and seq % CHUNK == 0 and seq % min(ROW_TILE, batch * seq) == 0

    lam_init = 0.8 - 0.6 * math.exp(-0.3 * 0)
    dots = lambda a, b: jnp.exp(jnp.sum(a.astype(F32) * b.astype(F32)))
    lam = (dots(da_lq1[0], da_lk1[0]) - dots(da_lq2[0], da_lk2[0]) + lam_init).reshape(1).astype(F32)

    eye = jnp.eye(PEER_HEADS, dtype=F32)
    kbig = jnp.einsum("hcjd,hg->cjhgd", peer_keys[0], eye).reshape(
        2, PEER_N_KEYS * PEER_HEADS, PEER_HEADS * PEER_HALF).astype(BF16)
    wpq = peer_wq[0].reshape(D_MODEL, PEER_HEADS, 2, PEER_HALF).transpose(0, 2, 1, 3).reshape(
        D_MODEL, 2 * PEER_HEADS * PEER_HALF).astype(BF16)
    weights = (norm1_g[0][None], w_in[0].astype(BF16), gm_norm_g[0].reshape(1, GM_WIDTH),
               da_subln_g[0][None], w_o[0].astype(BF16), norm2_g[0][None], wpq, kbig,
               peer_u[0].astype(BF16), peer_v[0].T.astype(BF16), final_g[None], 1.0 - lam_init)

    xp = x_prompt.reshape(batch * seq, D_MODEL)
    tabs_p = _rope_tables(jnp.arange(seq, dtype=F32))
    wmix_p = jnp.tril(gm_ws[0]).astype(BF16)
    bias_p = jnp.repeat(jnp.transpose(gm_b[0]), HEAD_LANES, axis=1)
    attend_p = lambda q1, q2, k32, v32, kb, vb: _prompt_attention(lam, q1, q2, kb, vb, batch, seq)
    y_p, k_p, v_p, _ = _row_pipeline(xp, tabs_p, wmix_p, bias_p, attend_p, weights)

    xs = x_sample.reshape(nb * n_new, D_MODEL)
    tabs_s = _rope_tables(jnp.tile(past + jnp.arange(n_new, dtype=F32), nb))
    rows_s = min(ROW_TILE, nb * n_new)
    w_new = jnp.tril(gm_ws[0][:, :n_new, :n_new])
    r = jnp.arange(rows_s)
    pick = (r[:, None] % n_new == jnp.arange(n_new)[None, :]).astype(F32)
    w_rows = jnp.einsum("ri,hij,cj->hrc", pick, w_new, pick, precision=lax.Precision.HIGHEST)
    wmix_s = jnp.where(r[:, None] // n_new == r[None, :] // n_new, w_rows, 0.0).astype(BF16)
    bias_s = jnp.tile(jnp.repeat(jnp.transpose(gm_b[0][:, :n_new]), HEAD_LANES, axis=1), (rows_s // n_new, 1))
    ck = cache_k[0].reshape(-1, PAGE_SIZE * DA_HEADS, HEAD_LANES)
    cv = cache_v[0].reshape(-1, PAGE_SIZE * DA_HEADS, HEAD_LANES)

    def attend_s(q1, q2, k32, v32, kb, vb):
        per_head = lambda t: jnp.transpose(t.reshape(nb, n_new, DA_HEADS, HEAD_LANES), (0, 2, 1, 3))
        q = jnp.stack([per_head(q1), per_head(q2)], axis=1).reshape(nb, 2 * DA_HEADS * n_new, HEAD_LANES)
        new_rows = lambda t: t.reshape(nb, n_new * DA_HEADS, HEAD_LANES).astype(BF16)
        ao = _decode_attention(page_table, lam, q, new_rows(k32), new_rows(v32), ck, cv)
        ao = jnp.transpose(ao.reshape(nb, DA_HEADS, n_new, HEAD_LANES), (0, 2, 1, 3))
        return ao.reshape(nb * n_new, DA_WIDTH)

    y_s, k_s, v_s, gn_s = _row_pipeline(xs, tabs_s, wmix_s, bias_s, attend_s, weights)

    head_shape = lambda b, s: (1, b, s, DA_HEADS, HEAD_LANES)
    return (y_p.reshape(batch, seq, D_MODEL), y_s.reshape(nb, n_new, D_MODEL),
            k_p.reshape(head_shape(batch, seq)), v_p.reshape(head_shape(batch, seq)),
            k_s.reshape(head_shape(nb, n_new)), v_s.reshape(head_shape(nb, n_new)),
            gn_s.reshape(1, nb, n_new, GM_HEADS, HEAD_LANES))
```

```python
import functools
import math

import jax
import jax.numpy as jnp
from jax import lax
from jax.experimental import pallas as pl
from jax.experimental.pallas import tpu as pltpu

F32 = jnp.float32
BF16 = jnp.bfloat16

D_MODEL = 1024
DA_WIDTH = 512
DA_HEADS = 4
HEAD_LANES = 128
DA_HEAD_DIM = 64
GM_WIDTH = 512
GM_HEADS = 4
CHUNK = 128
IN_WIDTH = 3 * DA_WIDTH + 2 * GM_WIDTH
ROPE_THETA = 500000.0
ROPE_DIM = 16
PAGE_SIZE = 128
PEER_HEADS = 8
PEER_N_KEYS = 128
PEER_N_EXPERTS = PEER_N_KEYS * PEER_N_KEYS
PEER_HALF = 128
PEER_TOPK = 16
NORM_EPS = 1e-6
NEG_INF = float("-inf")

VMEM_LIMIT_BYTES = 56 * 1024 * 1024

ROW_TILE = 512
ATTN_Q_TILE = 256
TOPK_LANES = 128
PEER_EXPERT_TILE = 1024
PEER_TOKEN_TILE = 1024


def _params(*semantics):
    return pltpu.CompilerParams(dimension_semantics=semantics, vmem_limit_bytes=VMEM_LIMIT_BYTES)


def _rms(x, g):
    return x * lax.rsqrt(jnp.mean(x * x, axis=-1, keepdims=True) + NORM_EPS) * g


def _gelu(x):
    return x * (0.5 * (1.0 + jnp.tanh(math.sqrt(2.0 / math.pi) * (x + 0.044715 * (x * x * x)))))


def _const_spec(shape):
    zeros = (0,) * len(shape)
    return pl.BlockSpec(shape, lambda *_: zeros)


def _proj_kernel(x_ref, g1_ref, win_ref, cos_ref, sa_ref, sb_ref, gmg_ref, wmix_ref, bias_ref,
                 q1_ref, q2_ref, k32_ref, v32_ref, kb_ref, vb_ref, gmo_ref, gn_ref, *, mix_rows):
    rows = x_ref.shape[0]
    xn = _rms(x_ref[...], g1_ref[...]).astype(BF16)
    z = jnp.dot(xn, win_ref[...], preferred_element_type=F32)
    cos_t, sin_a, sin_b = cos_ref[...], sa_ref[...], sb_ref[...]
    first_half = lax.broadcasted_iota(jnp.int32, (rows, HEAD_LANES), 1) < DA_HEAD_DIM

    def rope(t):
        return t * cos_t + pltpu.roll(t, HEAD_LANES - 8, 1) * sin_a + pltpu.roll(t, 8, 1) * sin_b

    for b in range(DA_HEADS):
        sl = slice(b * HEAD_LANES, (b + 1) * HEAD_LANES)
        q = rope(z[:, sl]) * (1.0 / math.sqrt(DA_HEAD_DIM))
        q1_ref[:, sl] = jnp.where(first_half, q, 0.0).astype(BF16)
        q2_ref[:, sl] = jnp.where(first_half, 0.0, q).astype(BF16)
        k = rope(z[:, DA_WIDTH + b * HEAD_LANES:DA_WIDTH + (b + 1) * HEAD_LANES])
        v = z[:, 2 * DA_WIDTH + b * HEAD_LANES:2 * DA_WIDTH + (b + 1) * HEAD_LANES]
        k32_ref[pl.ds(b, rows, stride=DA_HEADS), :] = k
        v32_ref[pl.ds(b, rows, stride=DA_HEADS), :] = v
        kb_ref[:, sl] = k.astype(BF16)
        vb_ref[:, sl] = v.astype(BF16)

    u = _gelu(z[:, 3 * DA_WIDTH:3 * DA_WIDTH + GM_WIDTH])
    g = _gelu(z[:, 3 * DA_WIDTH + GM_WIDTH:])
    for b in range(GM_HEADS):
        sl = slice(b * HEAD_LANES, (b + 1) * HEAD_LANES)
        gn = _rms(g[:, sl], gmg_ref[:, sl])
        gn_ref[:, sl] = gn
        gnb = gn.astype(BF16)
        for r in range(rows // mix_rows):
            rs = slice(r * mix_rows, (r + 1) * mix_rows)
            s = jnp.dot(wmix_ref[b], gnb[rs], preferred_element_type=F32) + bias_ref[:, sl]
            gmo_ref[rs, sl] = (u[rs, sl] * s).astype(BF16)


def _proj(x, g1, w_in, cos_t, sin_a, sin_b, gm_g, wmix, bias):
    rows = x.shape[0]
    tile = min(ROW_TILE, rows)
    mix_rows = wmix.shape[-1]
    row_spec = lambda w: pl.BlockSpec((tile, w), lambda i: (i, 0))
    table_blocks = cos_t.shape[0] // tile
    table_spec = pl.BlockSpec((tile, HEAD_LANES), lambda i: (i % table_blocks, 0))
    head_rows_spec = pl.BlockSpec((tile * DA_HEADS, HEAD_LANES), lambda i: (i, 0))
    out_shape = [jax.ShapeDtypeStruct((rows, DA_WIDTH), BF16),
                 jax.ShapeDtypeStruct((rows, DA_WIDTH), BF16),
                 jax.ShapeDtypeStruct((rows * DA_HEADS, HEAD_LANES), F32),
                 jax.ShapeDtypeStruct((rows * DA_HEADS, HEAD_LANES), F32),
                 jax.ShapeDtypeStruct((rows, DA_WIDTH), BF16),
                 jax.ShapeDtypeStruct((rows, DA_WIDTH), BF16),
                 jax.ShapeDtypeStruct((rows, GM_WIDTH), BF16),
                 jax.ShapeDtypeStruct((rows, GM_WIDTH), F32)]
    return pl.pallas_call(
        functools.partial(_proj_kernel, mix_rows=mix_rows),
        grid=(rows // tile,),
        in_specs=[row_spec(D_MODEL), _const_spec((1, D_MODEL)), _const_spec((D_MODEL, IN_WIDTH)),
                  table_spec, table_spec, table_spec,
                  _const_spec((1, GM_WIDTH)), _const_spec(wmix.shape), _const_spec(bias.shape)],
        out_specs=[row_spec(DA_WIDTH)] * 2 + [head_rows_spec] * 2 + [row_spec(DA_WIDTH)] * 2
                  + [row_spec(GM_WIDTH)] * 2,
        out_shape=out_shape,
        compiler_params=_params("parallel"),
        name="proj",
    )(x, g1, w_in, cos_t, sin_a, sin_b, gm_g, wmix, bias)


def _attn_kernel(lam_ref, q1_ref, q2_ref, k_ref, v_ref, o_ref):
    tq = q1_ref.shape[0]
    i = pl.program_id(2)
    q = jnp.concatenate([q1_ref[...], q2_ref[...]], axis=0)

    def step(j, carry, masked):
        m, l, acc = carry
        start = pl.multiple_of(j * tq, tq)
        kb = k_ref[pl.ds(start, tq), :]
        vb = v_ref[pl.ds(start, tq), :]
        s = lax.dot_general(q, kb, (((1,), (1,)), ((), ())), preferred_element_type=F32)
        if masked:
            row = lax.broadcasted_iota(jnp.int32, (2 * tq, tq), 0)
            col = lax.broadcasted_iota(jnp.int32, (2 * tq, tq), 1)
            row = jnp.where(row >= tq, row - tq, row)
            s = jnp.where(col <= row, s, NEG_INF)
        m_new = jnp.maximum(m, jnp.max(s, axis=-1, keepdims=True))
        alpha = jnp.exp(m - m_new)
        p = jnp.exp(s - m_new)
        l = alpha * l + jnp.sum(p, axis=-1, keepdims=True)
        acc = alpha * acc + jnp.dot(p.astype(BF16), vb, preferred_element_type=F32)
        return m_new, l, acc

    init = (jnp.full((2 * tq, 1), NEG_INF, F32), jnp.zeros((2 * tq, 1), F32),
            jnp.zeros((2 * tq, HEAD_LANES), F32))
    carry = lax.fori_loop(0, i, lambda j, c: step(j, c, False), init)
    _, l, acc = step(i, carry, True)
    o = acc / l
    o_ref[...] = o[:tq] - lam_ref[0] * o[tq:]


def _prompt_attention(lam, q1, q2, kb, vb, batch, seq):
    tq = min(ATTN_Q_TILE, seq)
    nq = seq // tq
    q_spec = pl.BlockSpec((tq, HEAD_LANES), lambda b, h, i: (b * nq + i, h))
    kv_spec = pl.BlockSpec((seq, HEAD_LANES), lambda b, h, i: (b, h))
    return pl.pallas_call(
        _attn_kernel,
        grid=(batch, DA_HEADS, nq),
        in_specs=[pl.BlockSpec(memory_space=pltpu.SMEM), q_spec, q_spec, kv_spec, kv_spec],
        out_specs=q_spec,
        out_shape=jax.ShapeDtypeStruct((batch * seq, DA_WIDTH), F32),
        compiler_params=_params("parallel", "parallel", "arbitrary"),
        name="prompt_attn",
    )(lam, q1, q2, kb, vb)


def _decode_attn_kernel(pt_ref, lam_ref, q_ref, kn_ref, vn_ref, *rest, n_pages, n_new):
    del pt_ref
    k_pages, v_pages = rest[:n_pages], rest[n_pages:2 * n_pages]
    o_ref, k_all, v_all = rest[2 * n_pages:]
    page_rows = PAGE_SIZE * DA_HEADS
    for p in range(n_pages):
        rs = slice(p * page_rows, (p + 1) * page_rows)
        k_all[rs, :] = k_pages[p][0].astype(BF16)
        v_all[rs, :] = v_pages[p][0].astype(BF16)
    q = q_ref[0]
    nrow = q.shape[0]
    contract_last = (((1,), (1,)), ((), ()))
    s_past = lax.dot_general(q, k_all[...], contract_last, preferred_element_type=F32)
    s_new = lax.dot_general(q, kn_ref[0], contract_last, preferred_element_type=F32)

    def row_col(shape):
        row = lax.broadcasted_iota(jnp.int32, shape, 0)
        col = lax.broadcasted_iota(jnp.int32, shape, 1)
        return (row // n_new) % DA_HEADS, row % n_new, col % DA_HEADS, col // DA_HEADS

    q_head, _, k_head, _ = row_col(s_past.shape)
    s_past = jnp.where(q_head == k_head, s_past, NEG_INF)
    q_head, q_tok, k_head, k_tok = row_col(s_new.shape)
    s_new = jnp.where(q_head == k_head, jnp.where(k_tok <= q_tok, s_new, NEG_INF), NEG_INF)
    m = jnp.maximum(jnp.max(s_past, axis=-1, keepdims=True), jnp.max(s_new, axis=-1, keepdims=True))
    p_past = jnp.exp(s_past - m)
    p_new = jnp.exp(s_new - m)
    inv = 1.0 / (jnp.sum(p_past, axis=-1, keepdims=True) + jnp.sum(p_new, axis=-1, keepdims=True))
    half = nrow // 2
    lam = lam_ref[0]
    a_past = (p_past[:half] * inv[:half] - lam * (p_past[half:] * inv[half:])).astype(BF16)
    a_new = (p_new[:half] * inv[:half] - lam * (p_new[half:] * inv[half:])).astype(BF16)
    o_ref[0] = (jnp.dot(a_past, v_all[...], preferred_element_type=F32)
                + jnp.dot(a_new, vn_ref[0], preferred_element_type=F32))


def _decode_attention(page_table, lam, q_rows, k_new, v_new, cache_k, cache_v):
    nb, n_pages = page_table.shape
    n_new = k_new.shape[1] // DA_HEADS
    page_rows = PAGE_SIZE * DA_HEADS
    page_spec = lambda p: pl.BlockSpec((1, page_rows, HEAD_LANES), lambda b, pt: (pt[b * n_pages + p], 0, 0))
    per_b = lambda r: pl.BlockSpec((1, r, HEAD_LANES), lambda b, pt: (b, 0, 0))
    grid_spec = pltpu.PrefetchScalarGridSpec(
        num_scalar_prefetch=1,
        grid=(nb,),
        in_specs=[pl.BlockSpec(memory_space=pltpu.SMEM), per_b(q_rows.shape[1]),
                  per_b(n_new * DA_HEADS), per_b(n_new * DA_HEADS)]
                 + [page_spec(p) for p in range(n_pages)] * 2,
        out_specs=per_b(n_new * DA_HEADS),
        scratch_shapes=[pltpu.VMEM((n_pages * page_rows, HEAD_LANES), BF16)] * 2,
    )
    return pl.pallas_call(
        functools.partial(_decode_attn_kernel, n_pages=n_pages, n_new=n_new),
        grid_spec=grid_spec,
        out_shape=jax.ShapeDtypeStruct((nb, n_new * DA_HEADS, HEAD_LANES), F32),
        compiler_params=_params("arbitrary"),
        name="decode_attn",
    )(page_table.reshape(-1), lam, q_rows, k_new, v_new,
      *([cache_k] * n_pages), *([cache_v] * n_pages))


def _merge_kernel(ao_ref, gmo_ref, x_ref, sg_ref, wo_ref, g2_ref, wpq_ref, kbig_ref,
                  h_ref, tn_ref, s_ref, *, attn_scale):
    parts = []
    for b in range(DA_HEADS):
        sl = slice(b * HEAD_LANES, (b + 1) * HEAD_LANES)
        parts.append((_rms(ao_ref[:, sl], sg_ref[...]) * attn_scale).astype(BF16))
    cat = jnp.concatenate(parts + [gmo_ref[...]], axis=1)
    h = x_ref[...] + jnp.dot(cat, wo_ref[...], preferred_element_type=F32)
    h_ref[...] = h
    tn = _rms(h, g2_ref[...]).astype(BF16)
    tn_ref[...] = tn
    qp = jnp.dot(tn, wpq_ref[...], preferred_element_type=F32).astype(BF16)
    width = PEER_HEADS * PEER_HALF
    for c in range(2):
        s_ref[c] = lax.dot_general(kbig_ref[c], qp[:, c * width:(c + 1) * width],
                                   (((1,), (1,)), ((), ())), preferred_element_type=F32)


def _merge(ao, gmo, x, subln_g, w_o, g2, wpq, kbig, attn_scale):
    rows = x.shape[0]
    tile = min(ROW_TILE, rows)
    width = PEER_HEADS * PEER_HALF
    row_spec = lambda w: pl.BlockSpec((tile, w), lambda i: (i, 0))
    return pl.pallas_call(
        functools.partial(_merge_kernel, attn_scale=attn_scale),
        grid=(rows // tile,),
        in_specs=[row_spec(DA_WIDTH), row_spec(GM_WIDTH), row_spec(D_MODEL), _const_spec((1, HEAD_LANES)),
                  _const_spec((D_MODEL, D_MODEL)), _const_spec((1, D_MODEL)),
                  _const_spec((D_MODEL, 2 * width)), _const_spec((2, PEER_N_KEYS * PEER_HEADS, width))],
        out_specs=[row_spec(D_MODEL), row_spec(D_MODEL),
                   pl.BlockSpec((2, PEER_N_KEYS * PEER_HEADS, tile), lambda i: (0, 0, i))],
        out_shape=[jax.ShapeDtypeStruct((rows, D_MODEL), F32),
                   jax.ShapeDtypeStruct((rows, D_MODEL), BF16),
                   jax.ShapeDtypeStruct((2, PEER_N_KEYS * PEER_HEADS, rows), F32)],
        compiler_params=_params("parallel"),
        name="merge",
    )(ao, gmo, x, subln_g, w_o, g2, wpq, kbig)


def _young_candidates():
    return [(p, q) for p in range(PEER_TOPK) for q in range(PEER_TOPK) if (p + 1) * (q + 1) <= PEER_TOPK]


def _topk_kernel(s_ref, r2_ref, e2_ref, n_ref, c_ref, work_ref, top_ref, idx_ref, young_ref):
    nk = PEER_N_KEYS
    tile = s_ref.shape[2:]
    one, zero = jnp.ones(tile, F32), jnp.zeros(tile, F32)

    def extract_distinct(c):
        def round_(r, bound):
            tree = [jnp.where(s_ref[c, j] < bound, s_ref[c, j], NEG_INF) for j in range(nk)]
            while len(tree) > 1:
                tree = [jnp.maximum(tree[t], tree[t + 1]) for t in range(0, len(tree), 2)]
            top_ref[c, r] = tree[0]
            return tree[0]

        last = lax.fori_loop(0, PEER_TOPK, round_, jnp.full(tile, float("inf"), F32))
        reached = zero
        for j in range(nk):
            reached = reached + jnp.where(s_ref[c, j] >= last, one, zero)
        return reached

    def extract(c):
        for j in range(nk):
            work_ref[j] = s_ref[c, j]

        def round_(r, _):
            vals = [work_ref[j] for j in range(nk)]
            idxs = [None] * nk
            width = 1
            while len(vals) > 1:
                nv, ni = [], []
                for t in range(0, len(vals), 2):
                    a, b = vals[t], vals[t + 1]
                    take_b = b > a
                    ia = idxs[t] if idxs[t] is not None else float(t * width)
                    ib = idxs[t + 1] if idxs[t + 1] is not None else float((t + 1) * width)
                    nv.append(jnp.maximum(a, b))
                    ni.append(jnp.where(take_b, ib, ia))
                vals, idxs = nv, ni
                width *= 2
            best, best_idx = vals[0], idxs[0]
            top_ref[c, r] = best
            idx_ref[c, r] = best_idx
            for j in range(nk):
                work_ref[j] = jnp.where(best_idx == float(j), NEG_INF, work_ref[j])
            return 0

        lax.fori_loop(0, PEER_TOPK, round_, 0)

    miscount = jnp.maximum(jnp.abs(extract_distinct(0) - float(PEER_TOPK)),
                           jnp.abs(extract_distinct(1) - float(PEER_TOPK)))
    tied = jnp.max(miscount) > 0.0

    @pl.when(tied)
    def _():
        extract(0)
        extract(1)

    a = [top_ref[0, p] for p in range(PEER_TOPK)]
    b = [top_ref[1, q] for q in range(PEER_TOPK)]
    cands = _young_candidates()
    cand = {pq: a[pq[0]] + b[pq[1]] for pq in cands}
    n_of_p = [zero] * PEER_TOPK
    ea = [jnp.exp(a[p] - a[0]) for p in range(PEER_TOPK)]
    eb = [jnp.exp(b[q] - b[0]) for q in range(PEER_TOPK)]
    z = zero
    for (p, q) in cands:
        fixed = 0
        beaten = zero
        for (p2, q2) in cands:
            if (p2, q2) == (p, q):
                continue
            if p2 <= p and q2 <= q:
                fixed += 1
            elif p2 >= p and q2 >= q:
                continue
            elif p2 * PEER_TOPK + q2 < p * PEER_TOPK + q:
                beaten = beaten + jnp.where(cand[(p2, q2)] >= cand[(p, q)], one, zero)
            else:
                beaten = beaten + jnp.where(cand[(p2, q2)] > cand[(p, q)], one, zero)
        sel = jnp.where(beaten + float(fixed) < float(PEER_TOPK), one, zero)
        n_of_p[p] = n_of_p[p] + sel
        z = z + sel * (ea[p] * eb[q])
    inv_z = 1.0 / z
    for p in range(PEER_TOPK):
        young_ref[p] = n_of_p[p]

    def scatter(hit_a, hit_b):
        for j in range(nk):
            n_j = zero
            r_j = jnp.full(tile, float(nk - 1), F32)
            for p in range(PEER_TOPK):
                n_j = jnp.where(hit_a(p, j), young_ref[p], n_j)
                r_j = jnp.where(hit_b(p, j), float(p), r_j)
            n_ref[j] = n_j
            r2_ref[j] = r_j

    @pl.when(jnp.logical_not(tied))
    def _():
        scatter(lambda p, j: top_ref[0, p] == s_ref[0, j], lambda p, j: top_ref[1, p] == s_ref[1, j])

    @pl.when(tied)
    def _():
        scatter(lambda p, j: idx_ref[0, p] == float(j), lambda p, j: idx_ref[1, p] == float(j))

    for j in range(nk):
        c_ref[j] = jnp.exp(s_ref[0, j] - a[0]) * inv_z
        e2_ref[j] = jnp.exp(s_ref[1, j] - b[0])


def _topk(scores):
    tokens = scores.shape[-1]
    lanes = TOPK_LANES
    tab = jax.ShapeDtypeStruct((tokens // lanes, PEER_N_KEYS, PEER_HEADS, lanes), F32)
    tab_spec = pl.BlockSpec((None, PEER_N_KEYS, PEER_HEADS, lanes), lambda i: (i, 0, 0, 0))
    return pl.pallas_call(
        _topk_kernel,
        grid=(tokens // lanes,),
        in_specs=[pl.BlockSpec((2, PEER_N_KEYS, PEER_HEADS, lanes), lambda i: (0, 0, 0, i))],
        out_specs=[tab_spec] * 4,
        out_shape=[tab] * 4,
        scratch_shapes=[pltpu.VMEM((PEER_N_KEYS, PEER_HEADS, lanes), F32),
                        pltpu.VMEM((2, PEER_TOPK, PEER_HEADS, lanes), F32),
                        pltpu.VMEM((2, PEER_TOPK, PEER_HEADS, lanes), F32),
                        pltpu.VMEM((PEER_TOPK, PEER_HEADS, lanes), F32)],
        compiler_params=_params("parallel"),
        name="peer_topk",
    )(scores)


def _peer_kernel(tn_ref, pu_ref, pvt_ref, r2_ref, e2_ref, n_ref, c_ref, h_ref, gf_ref,
                 y_ref, acc_ref, at_ref, gate_ref, hid_ref, r2s_ref, e2s_ref):
    e = pl.program_id(1)
    n_blocks = pl.num_programs(1) - 1
    tt = tn_ref.shape[0]
    nk = PEER_N_KEYS
    keys_per_step = pu_ref.shape[0] // nk
    lane_tiles = [(t, slice(t * HEAD_LANES, (t + 1) * HEAD_LANES)) for t in range(tt // HEAD_LANES)]
    pack = 16

    def first_matmul_and_gate(slot):
        at_ref[...] = lax.dot_general(pu_ref[...], tn_ref[...], (((1,), (1,)), ((), ())),
                                      preferred_element_type=F32).astype(BF16)
        for ii in range(keys_per_step):
            for t, lanes in lane_tiles:
                gate = jnp.zeros((nk, HEAD_LANES), BF16)
                for h in range(PEER_HEADS):
                    row = ii * PEER_HEADS + h
                    n = jnp.broadcast_to(n_ref[t, row:row + 1, :], (pack, HEAD_LANES)).astype(BF16)
                    c = jnp.broadcast_to(c_ref[t, row:row + 1, :], (pack, HEAD_LANES)).astype(BF16)
                    n = jnp.tile(n, (nk // pack, 1))
                    c = jnp.tile(c, (nk // pack, 1))
                    gate = gate + jnp.where(r2s_ref[h, :, lanes] < n, e2s_ref[h, :, lanes] * c,
                                            jnp.zeros_like(gate))
                gate_ref[ii * nk:(ii + 1) * nk, lanes] = gate
        hid_ref[slot] = gate_ref[...] * _gelu(at_ref[...])

    def second_matmul(slot):
        acc_ref[...] += jnp.dot(pvt_ref[...], hid_ref[slot], preferred_element_type=F32)

    @pl.when(e == 0)
    def _():
        for h in range(PEER_HEADS):
            for t, lanes in lane_tiles:
                r2s_ref[h, :, lanes] = r2_ref[t, pl.ds(h, nk, stride=PEER_HEADS), :].astype(BF16)
                e2s_ref[h, :, lanes] = e2_ref[t, pl.ds(h, nk, stride=PEER_HEADS), :].astype(BF16)
        acc_ref[...] = jnp.zeros_like(acc_ref)
        first_matmul_and_gate(0)

    @pl.when((e > 0) & (e < n_blocks))
    def _():
        slot = e % 2
        first_matmul_and_gate(slot)
        second_matmul(1 - slot)

    @pl.when(e == n_blocks)
    def _():
        second_matmul((n_blocks - 1) % 2)
        y = h_ref[...] + jnp.transpose(acc_ref[...])
        y_ref[...] = _rms(y, gf_ref[...])


def _peer(tn, pu, pvt, r2, e2, ntab, ctab, h, final_g):
    tokens = tn.shape[0]
    tt = min(PEER_TOKEN_TILE, tokens)
    et = PEER_EXPERT_TILE
    n_blocks = PEER_N_EXPERTS // et
    rows_per_step = (et // PEER_N_KEYS) * PEER_HEADS
    once = pl.Buffered(1)
    whole_tab = pl.BlockSpec((tt // HEAD_LANES, PEER_N_KEYS * PEER_HEADS, HEAD_LANES),
                             lambda t, e: (t, 0, 0), pipeline_mode=once)
    step_tab = pl.BlockSpec((tt // HEAD_LANES, rows_per_step, HEAD_LANES),
                            lambda t, e: (t, jnp.minimum(e, n_blocks - 1), 0))
    tok_spec = pl.BlockSpec((tt, D_MODEL), lambda t, e: (t, 0))
    return pl.pallas_call(
        _peer_kernel,
        grid=(tokens // tt, n_blocks + 1),
        in_specs=[tok_spec,
                  pl.BlockSpec((et, D_MODEL), lambda t, e: (jnp.minimum(e, n_blocks - 1), 0)),
                  pl.BlockSpec((D_MODEL, et), lambda t, e: (0, jnp.maximum(e - 1, 0))),
                  whole_tab, whole_tab, step_tab, step_tab,
                  pl.BlockSpec((tt, D_MODEL), lambda t, e: (t, 0), pipeline_mode=once),
                  _const_spec((1, D_MODEL))],
        out_specs=tok_spec,
        out_shape=jax.ShapeDtypeStruct((tokens, D_MODEL), F32),
        scratch_shapes=[pltpu.VMEM((D_MODEL, tt), F32),
                        pltpu.VMEM((et, tt), BF16), pltpu.VMEM((et, tt), BF16),
                        pltpu.VMEM((2, et, tt), BF16),
                        pltpu.VMEM((PEER_HEADS, PEER_N_KEYS, tt), BF16),
                        pltpu.VMEM((PEER_HEADS, PEER_N_KEYS, tt), BF16)],
        compiler_params=_params("parallel", "arbitrary"),
        name="peer_experts",
    )(tn, pu, pvt, r2, e2, ntab, ctab, h, final_g)


def _rope_tables(pos):
    half = ROPE_DIM // 2
    inv = ROPE_THETA ** (-jnp.arange(half, dtype=F32) * 2.0 / ROPE_DIM)
    d = jnp.arange(HEAD_LANES) % DA_HEAD_DIM
    ang = pos[:, None] * inv[d % half][None, :]
    cos_t = jnp.where(d < ROPE_DIM, jnp.cos(ang), 1.0)
    sin_a = jnp.where(d < half, -jnp.sin(ang), 0.0)
    sin_b = jnp.where((d >= half) & (d < ROPE_DIM), jnp.sin(ang), 0.0)
    return cos_t, sin_a, sin_b


def _row_pipeline(x, rope_tabs, wmix, bias, attend, weights):
    (g1, w_in, gm_g, subln_g, w_o, g2, wpq, kbig, pu, pvt, final_g, attn_scale) = weights
    q1, q2, k32, v32, kb, vb, gmo, gn = _proj(x, g1, w_in, *rope_tabs, gm_g, wmix, bias)
    ao = attend(q1, q2, k32, v32, kb, vb)
    h, tn, scores = _merge(ao, gmo, x, subln_g, w_o, g2, wpq, kbig, attn_scale)
    rows = x.shape[0]
    r2, e2, ntab, ctab = _topk(scores.reshape(2, PEER_N_KEYS, PEER_HEADS, rows))
    flat = lambda t: t.reshape(rows // TOPK_LANES, PEER_N_KEYS * PEER_HEADS, TOPK_LANES)
    y = _peer(tn, pu, pvt, flat(r2), flat(e2), flat(ntab), flat(ctab), h, final_g)
    return y, k32, v32, gn


def kernel(x_prompt, x_sample, cache_k, cache_v, page_table, norm1_g, w_in, gm_norm_g, gm_ws, gm_b,
           da_lq1, da_lk1, da_lq2, da_lk2, da_subln_g, w_o, norm2_g, peer_wq, peer_keys, peer_u,
           peer_v, final_g):
    batch, seq, _ = x_prompt.shape
    nb, n_new, _ = x_sample.shape
    n_pages = page_table.shape[1]
    past = n_pages * PAGE_SIZE
    assert w_in.shape[0] == 1 and seq % CHUNK == 0 and seq % min(ROW_TILE, batch * seq) == 0

    lam_init = 0.8 - 0.6 * math.exp(-0.3 * 0)
    dots = lambda a, b: jnp.exp(jnp.sum(a.astype(F32) * b.astype(F32)))
    lam = (dots(da_lq1[0], da_lk1[0]) - dots(da_lq2[0], da_lk2[0]) + lam_init).reshape(1).astype(F32)

    eye = jnp.eye(PEER_HEADS, dtype=F32)
    kbig = jnp.einsum("hcjd,hg->cjhgd", peer_keys[0], eye).reshape(
        2, PEER_N_KEYS * PEER_HEADS, PEER_HEADS * PEER_HALF).astype(BF16)
    wpq = peer_wq[0].reshape(D_MODEL, PEER_HEADS, 2, PEER_HALF).transpose(0, 2, 1, 3).reshape(
        D_MODEL, 2 * PEER_HEADS * PEER_HALF).astype(BF16)
    weights = (norm1_g[0][None], w_in[0].astype(BF16), gm_norm_g[0].reshape(1, GM_WIDTH),
               da_subln_g[0][None], w_o[0].astype(BF16), norm2_g[0][None], wpq, kbig,
               peer_u[0].astype(BF16), peer_v[0].T.astype(BF16), final_g[None], 1.0 - lam_init)

    xp = x_prompt.reshape(batch * seq, D_MODEL)
    tabs_p = _rope_tables(jnp.arange(seq, dtype=F32))
    wmix_p = jnp.tril(gm_ws[0]).astype(BF16)
    bias_p = jnp.repeat(jnp.transpose(gm_b[0]), HEAD_LANES, axis=1)
    attend_p = lambda q1, q2, k32, v32, kb, vb: _prompt_attention(lam, q1, q2, kb, vb, batch, seq)
    y_p, k_p, v_p, _ = _row_pipeline(xp, tabs_p, wmix_p, bias_p, attend_p, weights)

    xs = x_sample.reshape(nb * n_new, D_MODEL)
    tabs_s = _rope_tables(jnp.tile(past + jnp.arange(n_new, dtype=F32), nb))
    rows_s = min(ROW_TILE, nb * n_new)
    w_new = jnp.tril(gm_ws[0][:, :n_new, :n_new])
    r = jnp.arange(rows_s)
    pick = (r[:, None] % n_new == jnp.arange(n_new)[None, :]).astype(F32)
    w_rows = jnp.einsum("ri,hij,cj->hrc", pick, w_new, pick, precision=lax.Precision.HIGHEST)
    wmix_s = jnp.where(r[:, None] // n_new == r[None, :] // n_new, w_rows, 0.0).astype(BF16)
    bias_s = jnp.tile(jnp.repeat(jnp.transpose(gm_b[0][:, :n_new]), HEAD_LANES, axis=1), (rows_s // n_new, 1))
    ck = cache_k[0].reshape(-1, PAGE_SIZE * DA_HEADS, HEAD_LANES)
    cv = cache_v[0].reshape(-1, PAGE_SIZE * DA_HEADS, HEAD_LANES)

    def attend_s(q1, q2, k32, v32, kb, vb):
        per_head = lambda t: jnp.transpose(t.reshape(nb, n_new, DA_HEADS, HEAD_LANES), (0, 2, 1, 3))
        q = jnp.stack([per_head(q1), per_head(q2)], axis=1).reshape(nb, 2 * DA_HEADS * n_new, HEAD_LANES)
        new_rows = lambda t: t.reshape(nb, n_new * DA_HEADS, HEAD_LANES).astype(BF16)
        ao = _decode_attention(page_table, lam, q, new_rows(k32), new_rows(v32), ck, cv)
        ao = jnp.transpose(ao.reshape(nb, DA_HEADS, n_new, HEAD_LANES), (0, 2, 1, 3))
        return ao.reshape(nb * n_new, DA_WIDTH)

    y_s, k_s, v_s, gn_s = _row_pipeline(xs, tabs_s, wmix_s, bias_s, attend_s, weights)

    head_shape = lambda b, s: (1, b, s, DA_HEADS, HEAD_LANES)
    return (y_p.reshape(batch, seq, D_MODEL), y_s.reshape(nb, n_new, D_MODEL),
            k_p.reshape(head_shape(batch, seq)), v_p.reshape(head_shape(batch, seq)),
            k_s.reshape(head_shape(nb, n_new)), v_s.reshape(head_shape(nb, n_new)),
            gn_s.reshape(1, nb, n_new, GM_HEADS, HEAD_LANES))
```

```python
import functools
import math

import jax
import jax.numpy as jnp
from jax import lax
from jax.experimental import pallas as pl
from jax.experimental.pallas import tpu as pltpu

F32 = jnp.float32
BF16 = jnp.bfloat16

D_MODEL = 1024
DA_WIDTH = 512
DA_HEADS = 4
HEAD_LANES = 128
DA_HEAD_DIM = 64
GM_WIDTH = 512
GM_HEADS = 4
CHUNK = 128
IN_WIDTH = 3 * DA_WIDTH + 2 * GM_WIDTH
ROPE_THETA = 500000.0
ROPE_DIM = 16
PAGE_SIZE = 128
PEER_HEADS = 8
PEER_N_KEYS = 128
PEER_N_EXPERTS = PEER_N_KEYS * PEER_N_KEYS
PEER_HALF = 128
PEER_TOPK = 16
NORM_EPS = 1e-6
NEG_INF = float("-inf")

VMEM_LIMIT_BYTES = 56 * 1024 * 1024

ROW_TILE = 512
ATTN_Q_TILE = 256
TOPK_LANES = 128
PEER_EXPERT_TILE = 1024
PEER_TOKEN_TILE = 1024


def _params(*semantics):
    return pltpu.CompilerParams(dimension_semantics=semantics, vmem_limit_bytes=VMEM_LIMIT_BYTES)


def _rms(x, g):
    return x * lax.rsqrt(jnp.mean(x * x, axis=-1, keepdims=True) + NORM_EPS) * g


def _gelu(x):
    return x * (0.5 * (1.0 + jnp.tanh(math.sqrt(2.0 / math.pi) * (x + 0.044715 * (x * x * x)))))


def _const_spec(shape):
    zeros = (0,) * len(shape)
    return pl.BlockSpec(shape, lambda *_: zeros)


def _proj_kernel(x_ref, g1_ref, win_ref, cos_ref, sa_ref, sb_ref, gmg_ref, wmix_ref, bias_ref,
                 q1_ref, q2_ref, k32_ref, v32_ref, kb_ref, vb_ref, gmo_ref, gn_ref, *, mix_rows):
    rows = x_ref.shape[0]
    xn = _rms(x_ref[...], g1_ref[...]).astype(BF16)
    z = jnp.dot(xn, win_ref[...], preferred_element_type=F32)
    cos_t, sin_a, sin_b = cos_ref[...], sa_ref[...], sb_ref[...]
    first_half = lax.broadcasted_iota(jnp.int32, (rows, HEAD_LANES), 1) < DA_HEAD_DIM

    def rope(t):
        return t * cos_t + pltpu.roll(t, HEAD_LANES - 8, 1) * sin_a + pltpu.roll(t, 8, 1) * sin_b

    for b in range(DA_HEADS):
        sl = slice(b * HEAD_LANES, (b + 1) * HEAD_LANES)
        q = rope(z[:, sl]) * (1.0 / math.sqrt(DA_HEAD_DIM))
        q1_ref[:, sl] = jnp.where(first_half, q, 0.0).astype(BF16)
        q2_ref[:, sl] = jnp.where(first_half, 0.0, q).astype(BF16)
        k = rope(z[:, DA_WIDTH + b * HEAD_LANES:DA_WIDTH + (b + 1) * HEAD_LANES])
        v = z[:, 2 * DA_WIDTH + b * HEAD_LANES:2 * DA_WIDTH + (b + 1) * HEAD_LANES]
        k32_ref[pl.ds(b, rows, stride=DA_HEADS), :] = k
        v32_ref[pl.ds(b, rows, stride=DA_HEADS), :] = v
        kb_ref[:, sl] = k.astype(BF16)
        vb_ref[:, sl] = v.astype(BF16)

    u = _gelu(z[:, 3 * DA_WIDTH:3 * DA_WIDTH + GM_WIDTH])
    g = _gelu(z[:, 3 * DA_WIDTH + GM_WIDTH:])
    for b in range(GM_HEADS):
        sl = slice(b * HEAD_LANES, (b + 1) * HEAD_LANES)
        gn = _rms(g[:, sl], gmg_ref[:, sl])
        gn_ref[:, sl] = gn
        gnb = gn.astype(BF16)
        for r in range(rows // mix_rows):
            rs = slice(r * mix_rows, (r + 1) * mix_rows)
            s = jnp.dot(wmix_ref[b], gnb[rs], preferred_element_type=F32) + bias_ref[:, sl]
            gmo_ref[rs, sl] = (u[rs, sl] * s).astype(BF16)


def _proj(x, g1, w_in, cos_t, sin_a, sin_b, gm_g, wmix, bias):
    rows = x.shape[0]
    tile = min(ROW_TILE, rows)
    mix_rows = wmix.shape[-1]
    row_spec = lambda w: pl.BlockSpec((tile, w), lambda i: (i, 0))
    table_blocks = cos_t.shape[0] // tile
    table_spec = pl.BlockSpec((tile, HEAD_LANES), lambda i: (i % table_blocks, 0))
    head_rows_spec = pl.BlockSpec((tile * DA_HEADS, HEAD_LANES), lambda i: (i, 0))
    out_shape = [jax.ShapeDtypeStruct((rows, DA_WIDTH), BF16),
                 jax.ShapeDtypeStruct((rows, DA_WIDTH), BF16),
                 jax.ShapeDtypeStruct((rows * DA_HEADS, HEAD_LANES), F32),
                 jax.ShapeDtypeStruct((rows * DA_HEADS, HEAD_LANES), F32),
                 jax.ShapeDtypeStruct((rows, DA_WIDTH), BF16),
                 jax.ShapeDtypeStruct((rows, DA_WIDTH), BF16),
                 jax.ShapeDtypeStruct((rows, GM_WIDTH), BF16),
                 jax.ShapeDtypeStruct((rows, GM_WIDTH), F32)]
    return pl.pallas_call(
        functools.partial(_proj_kernel, mix_rows=mix_rows),
        grid=(rows // tile,),
        in_specs=[row_spec(D_MODEL), _const_spec((1, D_MODEL)), _const_spec((D_MODEL, IN_WIDTH)),
                  table_spec, table_spec, table_spec,
                  _const_spec((1, GM_WIDTH)), _const_spec(wmix.shape), _const_spec(bias.shape)],
        out_specs=[row_spec(DA_WIDTH)] * 2 + [head_rows_spec] * 2 + [row_spec(DA_WIDTH)] * 2
                  + [row_spec(GM_WIDTH)] * 2,
        out_shape=out_shape,
        compiler_params=_params("parallel"),
        name="proj",
    )(x, g1, w_in, cos_t, sin_a, sin_b, gm_g, wmix, bias)


def _attn_kernel(lam_ref, q1_ref, q2_ref, k_ref, v_ref, o_ref):
    tq = q1_ref.shape[0]
    i = pl.program_id(2)
    q = jnp.concatenate([q1_ref[...], q2_ref[...]], axis=0)

    def step(j, carry, masked):
        m, l, acc = carry
        start = pl.multiple_of(j * tq, tq)
        kb = k_ref[pl.ds(start, tq), :]
        vb = v_ref[pl.ds(start, tq), :]
        s = lax.dot_general(q, kb, (((1,), (1,)), ((), ())), preferred_element_type=F32)
        if masked:
            row = lax.broadcasted_iota(jnp.int32, (2 * tq, tq), 0)
            col = lax.broadcasted_iota(jnp.int32, (2 * tq, tq), 1)
            row = jnp.where(row >= tq, row - tq, row)
            s = jnp.where(col <= row, s, NEG_INF)
        m_new = jnp.maximum(m, jnp.max(s, axis=-1, keepdims=True))
        alpha = jnp.exp(m - m_new)
        p = jnp.exp(s - m_new)
        l = alpha * l + jnp.sum(p, axis=-1, keepdims=True)
        acc = alpha * acc + jnp.dot(p.astype(BF16), vb, preferred_element_type=F32)
        return m_new, l, acc

    init = (jnp.full((2 * tq, 1), NEG_INF, F32), jnp.zeros((2 * tq, 1), F32),
            jnp.zeros((2 * tq, HEAD_LANES), F32))
    carry = lax.fori_loop(0, i, lambda j, c: step(j, c, False), init)
    _, l, acc = step(i, carry, True)
    o = acc / l
    o_ref[...] = o[:tq] - lam_ref[0] * o[tq:]


def _prompt_attention(lam, q1, q2, kb, vb, batch, seq):
    tq = min(ATTN_Q_TILE, seq)
    nq = seq // tq
    q_spec = pl.BlockSpec((tq, HEAD_LANES), lambda b, h, i: (b * nq + i, h))
    kv_spec = pl.BlockSpec((seq, HEAD_LANES), lambda b, h, i: (b, h))
    return pl.pallas_call(
        _attn_kernel,
        grid=(batch, DA_HEADS, nq),
        in_specs=[pl.BlockSpec(memory_space=pltpu.SMEM), q_spec, q_spec, kv_spec, kv_spec],
        out_specs=q_spec,
        out_shape=jax.ShapeDtypeStruct((batch * seq, DA_WIDTH), F32),
        compiler_params=_params("parallel", "parallel", "arbitrary"),
        name="prompt_attn",
    )(lam, q1, q2, kb, vb)


def _decode_attn_kernel(pt_ref, lam_ref, q_ref, kn_ref, vn_ref, *rest, n_pages, n_new):
    del pt_ref
    k_pages, v_pages = rest[:n_pages], rest[n_pages:2 * n_pages]
    o_ref, k_all, v_all = rest[2 * n_pages:]
    page_rows = PAGE_SIZE * DA_HEADS
    for p in range(n_pages):
        rs = slice(p * page_rows, (p + 1) * page_rows)
        k_all[rs, :] = k_pages[p][0].astype(BF16)
        v_all[rs, :] = v_pages[p][0].astype(BF16)
    q = q_ref[0]
    nrow = q.shape[0]
    contract_last = (((1,), (1,)), ((), ()))
    s_past = lax.dot_general(q, k_all[...], contract_last, preferred_element_type=F32)
    s_new = lax.dot_general(q, kn_ref[0], contract_last, preferred_element_type=F32)

    def row_col(shape):
        row = lax.broadcasted_iota(jnp.int32, shape, 0)
        col = lax.broadcasted_iota(jnp.int32, shape, 1)
        return (row // n_new) % DA_HEADS, row % n_new, col % DA_HEADS, col // DA_HEADS

    q_head, _, k_head, _ = row_col(s_past.shape)
    s_past = jnp.where(q_head == k_head, s_past, NEG_INF)
    q_head, q_tok, k_head, k_tok = row_col(s_new.shape)
    s_new = jnp.where(q_head == k_head, jnp.where(k_tok <= q_tok, s_new, NEG_INF), NEG_INF)
    m = jnp.maximum(jnp.max(s_past, axis=-1, keepdims=True), jnp.max(s_new, axis=-1, keepdims=True))
    p_past = jnp.exp(s_past - m)
    p_new = jnp.exp(s_new - m)
    inv = 1.0 / (jnp.sum(p_past, axis=-1, keepdims=True) + jnp.sum(p_new, axis=-1, keepdims=True))
    half = nrow // 2
    lam = lam_ref[0]
    a_past = (p_past[:half] * inv[:half] - lam * (p_past[half:] * inv[half:])).astype(BF16)
    a_new = (p_new[:half] * inv[:half] - lam * (p_new[half:] * inv[half:])).astype(BF16)
    o_ref[0] = (jnp.dot(a_past, v_all[...], preferred_element_type=F32)
                + jnp.dot(a_new, vn_ref[0], preferred_element_type=F32))


def _decode_attention(page_table, lam, q_rows, k_new, v_new, cache_k, cache_v):
    nb, n_pages = page_table.shape
    n_new = k_new.shape[1] // DA_HEADS
    page_rows = PAGE_SIZE * DA_HEADS
    page_spec = lambda p: pl.BlockSpec((1, page_rows, HEAD_LANES), lambda b, pt: (pt[b * n_pages + p], 0, 0))
    per_b = lambda r: pl.BlockSpec((1, r, HEAD_LANES), lambda b, pt: (b, 0, 0))
    grid_spec = pltpu.PrefetchScalarGridSpec(
        num_scalar_prefetch=1,
        grid=(nb,),
        in_specs=[pl.BlockSpec(memory_space=pltpu.SMEM), per_b(q_rows.shape[1]),
                  per_b(n_new * DA_HEADS), per_b(n_new * DA_HEADS)]
                 + [page_spec(p) for p in range(n_pages)] * 2,
        out_specs=per_b(n_new * DA_HEADS),
        scratch_shapes=[pltpu.VMEM((n_pages * page_rows, HEAD_LANES), BF16)] * 2,
    )
    return pl.pallas_call(
        functools.partial(_decode_attn_kernel, n_pages=n_pages, n_new=n_new),
        grid_spec=grid_spec,
        out_shape=jax.ShapeDtypeStruct((nb, n_new * DA_HEADS, HEAD_LANES), F32),
        compiler_params=_params("arbitrary"),
        name="decode_attn",
    )(page_table.reshape(-1), lam, q_rows, k_new, v_new,
      *([cache_k] * n_pages), *([cache_v] * n_pages))


def _merge_kernel(ao_ref, gmo_ref, x_ref, sg_ref, wo_ref, g2_ref, wpq_ref, kbig_ref,
                  h_ref, tn_ref, s_ref, *, attn_scale):
    parts = []
    for b in range(DA_HEADS):
        sl = slice(b * HEAD_LANES, (b + 1) * HEAD_LANES)
        parts.append((_rms(ao_ref[:, sl], sg_ref[...]) * attn_scale).astype(BF16))
    cat = jnp.concatenate(parts + [gmo_ref[...]], axis=1)
    h = x_ref[...] + jnp.dot(cat, wo_ref[...], preferred_element_type=F32)
    h_ref[...] = h
    tn = _rms(h, g2_ref[...]).astype(BF16)
    tn_ref[...] = tn
    qp = jnp.dot(tn, wpq_ref[...], preferred_element_type=F32).astype(BF16)
    width = PEER_HEADS * PEER_HALF
    for c in range(2):
        s_ref[c] = lax.dot_general(kbig_ref[c], qp[:, c * width:(c + 1) * width],
                                   (((1,), (1,)), ((), ())), preferred_element_type=F32)


def _merge(ao, gmo, x, subln_g, w_o, g2, wpq, kbig, attn_scale):
    rows = x.shape[0]
    tile = min(ROW_TILE, rows)
    width = PEER_HEADS * PEER_HALF
    row_spec = lambda w: pl.BlockSpec((tile, w), lambda i: (i, 0))
    return pl.pallas_call(
        functools.partial(_merge_kernel, attn_scale=attn_scale),
        grid=(rows // tile,),
        in_specs=[row_spec(DA_WIDTH), row_spec(GM_WIDTH), row_spec(D_MODEL), _const_spec((1, HEAD_LANES)),
                  _const_spec((D_MODEL, D_MODEL)), _const_spec((1, D_MODEL)),
                  _const_spec((D_MODEL, 2 * width)), _const_spec((2, PEER_N_KEYS * PEER_HEADS, width))],
        out_specs=[row_spec(D_MODEL), row_spec(D_MODEL),
                   pl.BlockSpec((2, PEER_N_KEYS * PEER_HEADS, tile), lambda i: (0, 0, i))],
        out_shape=[jax.ShapeDtypeStruct((rows, D_MODEL), F32),
                   jax.ShapeDtypeStruct((rows, D_MODEL), BF16),
                   jax.ShapeDtypeStruct((2, PEER_N_KEYS * PEER_HEADS, rows), F32)],
        compiler_params=_params("parallel"),
        name="merge",
    )(ao, gmo, x, subln_g, w_o, g2, wpq, kbig)


def _young_candidates():
    return [(p, q) for p in range(PEER_TOPK) for q in range(PEER_TOPK) if (p + 1) * (q + 1) <= PEER_TOPK]


def _topk_kernel(s_ref, r2_ref, e2_ref, n_ref, c_ref, work_ref, top_ref, idx_ref, young_ref,
                 rank_ref, exp_ref):
    nk = PEER_N_KEYS
    tile = s_ref.shape[2:]
    one, zero = jnp.ones(tile, F32), jnp.zeros(tile, F32)

    def extract_distinct(c):
        def round_(r, bound):
            tree = [jnp.where(s_ref[c, j] < bound, s_ref[c, j], NEG_INF) for j in range(nk)]
            while len(tree) > 1:
                tree = [jnp.maximum(tree[t], tree[t + 1]) for t in range(0, len(tree), 2)]
            top_ref[c, r] = tree[0]
            return tree[0]

        last = lax.fori_loop(0, PEER_TOPK, round_, jnp.full(tile, float("inf"), F32))
        reached = zero
        for j in range(nk):
            reached = reached + jnp.where(s_ref[c, j] >= last, one, zero)
        return reached

    def extract(c):
        for j in range(nk):
            work_ref[j] = s_ref[c, j]

        def round_(r, _):
            vals = [work_ref[j] for j in range(nk)]
            idxs = [None] * nk
            width = 1
            while len(vals) > 1:
                nv, ni = [], []
                for t in range(0, len(vals), 2):
                    a, b = vals[t], vals[t + 1]
                    take_b = b > a
                    ia = idxs[t] if idxs[t] is not None else float(t * width)
                    ib = idxs[t + 1] if idxs[t + 1] is not None else float((t + 1) * width)
                    nv.append(jnp.maximum(a, b))
                    ni.append(jnp.where(take_b, ib, ia))
                vals, idxs = nv, ni
                width *= 2
            best, best_idx = vals[0], idxs[0]
            top_ref[c, r] = best
            idx_ref[c, r] = best_idx
            for j in range(nk):
                work_ref[j] = jnp.where(best_idx == float(j), NEG_INF, work_ref[j])
            return 0

        lax.fori_loop(0, PEER_TOPK, round_, 0)

    miscount = jnp.maximum(jnp.abs(extract_distinct(0) - float(PEER_TOPK)),
                           jnp.abs(extract_distinct(1) - float(PEER_TOPK)))
    tied = jnp.max(miscount) > 0.0

    @pl.when(tied)
    def _():
        extract(0)
        extract(1)

    a = [top_ref[0, p] for p in range(PEER_TOPK)]
    b = [top_ref[1, q] for q in range(PEER_TOPK)]
    cands = _young_candidates()
    cand = {pq: a[pq[0]] + b[pq[1]] for pq in cands}
    n_of_p = [zero] * PEER_TOPK
    ea = [jnp.exp(a[p] - a[0]) for p in range(PEER_TOPK)]
    eb = [jnp.exp(b[q] - b[0]) for q in range(PEER_TOPK)]
    z = zero
    for (p, q) in cands:
        fixed = 0
        beaten = zero
        for (p2, q2) in cands:
            if (p2, q2) == (p, q):
                continue
            if p2 <= p and q2 <= q:
                fixed += 1
            elif p2 >= p and q2 >= q:
                continue
            elif p2 * PEER_TOPK + q2 < p * PEER_TOPK + q:
                beaten = beaten + jnp.where(cand[(p2, q2)] >= cand[(p, q)], one, zero)
            else:
                beaten = beaten + jnp.where(cand[(p2, q2)] > cand[(p, q)], one, zero)
        sel = jnp.where(beaten + float(fixed) < float(PEER_TOPK), one, zero)
        n_of_p[p] = n_of_p[p] + sel
        z = z + sel * (ea[p] * eb[q])
    inv_z = 1.0 / z
    for p in range(PEER_TOPK):
        young_ref[p] = n_of_p[p]

    def scatter(hit_a, hit_b):
        for j in range(nk):
            n_j = zero
            r_j = jnp.full(tile, float(nk - 1), F32)
            for p in range(PEER_TOPK):
                n_j = jnp.where(hit_a(p, j), young_ref[p], n_j)
                r_j = jnp.where(hit_b(p, j), float(p), r_j)
            n_ref[j] = n_j
            rank_ref[j * PEER_HEADS:(j + 1) * PEER_HEADS, :] = r_j

    @pl.when(jnp.logical_not(tied))
    def _():
        scatter(lambda p, j: top_ref[0, p] == s_ref[0, j], lambda p, j: top_ref[1, p] == s_ref[1, j])

    @pl.when(tied)
    def _():
        scatter(lambda p, j: idx_ref[0, p] == float(j), lambda p, j: idx_ref[1, p] == float(j))

    for j in range(nk):
        c_ref[j] = jnp.exp(s_ref[0, j] - a[0]) * inv_z
        exp_ref[j * PEER_HEADS:(j + 1) * PEER_HEADS, :] = jnp.exp(s_ref[1, j] - b[0])
    for h in range(PEER_HEADS):
        r2_ref[h] = rank_ref[pl.ds(h, nk, stride=PEER_HEADS), :].astype(BF16)
        e2_ref[h] = exp_ref[pl.ds(h, nk, stride=PEER_HEADS), :].astype(BF16)


def _topk(scores):
    tokens = scores.shape[-1]
    lanes = TOPK_LANES
    tab2 = jax.ShapeDtypeStruct((PEER_HEADS, PEER_N_KEYS, tokens), BF16)
    tab2_spec = pl.BlockSpec((PEER_HEADS, PEER_N_KEYS, lanes), lambda i: (0, 0, i))
    tab1 = jax.ShapeDtypeStruct((tokens // lanes, PEER_N_KEYS, PEER_HEADS, lanes), F32)
    tab1_spec = pl.BlockSpec((None, PEER_N_KEYS, PEER_HEADS, lanes), lambda i: (i, 0, 0, 0))
    return pl.pallas_call(
        _topk_kernel,
        grid=(tokens // lanes,),
        in_specs=[pl.BlockSpec((2, PEER_N_KEYS, PEER_HEADS, lanes), lambda i: (0, 0, 0, i))],
        out_specs=[tab2_spec] * 2 + [tab1_spec] * 2,
        out_shape=[tab2] * 2 + [tab1] * 2,
        scratch_shapes=[pltpu.VMEM((PEER_N_KEYS, PEER_HEADS, lanes), F32),
                        pltpu.VMEM((2, PEER_TOPK, PEER_HEADS, lanes), F32),
                        pltpu.VMEM((2, PEER_TOPK, PEER_HEADS, lanes), F32),
                        pltpu.VMEM((PEER_TOPK, PEER_HEADS, lanes), F32),
                        pltpu.VMEM((PEER_N_KEYS * PEER_HEADS, lanes), F32),
                        pltpu.VMEM((PEER_N_KEYS * PEER_HEADS, lanes), F32)],
        compiler_params=_params("parallel"),
        name="peer_topk",
    )(scores)


def _peer_kernel(tn_ref, pu_ref, pvt_ref, r2_ref, e2_ref, n_ref, c_ref, acc_ref,
                 at_ref, gate_ref, hid_ref, r2s_ref, e2s_ref):
    e = pl.program_id(1)
    n_blocks = pl.num_programs(1) - 1
    tt = tn_ref.shape[0]
    nk = PEER_N_KEYS
    keys_per_step = pu_ref.shape[0] // nk
    lane_tiles = [(t, slice(t * HEAD_LANES, (t + 1) * HEAD_LANES)) for t in range(tt // HEAD_LANES)]
    pack = 16

    def first_matmul_and_gate(slot):
        at_ref[...] = lax.dot_general(pu_ref[...], tn_ref[...], (((1,), (1,)), ((), ())),
                                      preferred_element_type=F32).astype(BF16)
        for ii in range(keys_per_step):
            for t, lanes in lane_tiles:
                gate = jnp.zeros((nk, HEAD_LANES), BF16)
                for h in range(PEER_HEADS):
                    row = ii * PEER_HEADS + h
                    n = jnp.broadcast_to(n_ref[t, row:row + 1, :], (pack, HEAD_LANES)).astype(BF16)
                    c = jnp.broadcast_to(c_ref[t, row:row + 1, :], (pack, HEAD_LANES)).astype(BF16)
                    n = jnp.tile(n, (nk // pack, 1))
                    c = jnp.tile(c, (nk // pack, 1))
                    gate = gate + jnp.where(r2s_ref[h, :, lanes] < n, e2s_ref[h, :, lanes] * c,
                                            jnp.zeros_like(gate))
                gate_ref[ii * nk:(ii + 1) * nk, lanes] = gate
        hid_ref[slot] = gate_ref[...] * _gelu(at_ref[...])

    def second_matmul(slot):
        acc_ref[...] += jnp.dot(pvt_ref[...], hid_ref[slot], preferred_element_type=F32)

    @pl.when(e == 0)
    def _():
        r2s_ref[...] = r2_ref[...]
        e2s_ref[...] = e2_ref[...]
        acc_ref[...] = jnp.zeros_like(acc_ref)
        first_matmul_and_gate(0)

    @pl.when((e > 0) & (e < n_blocks))
    def _():
        slot = e % 2
        first_matmul_and_gate(slot)
        second_matmul(1 - slot)

    @pl.when(e == n_blocks)
    def _():
        second_matmul((n_blocks - 1) % 2)


def _peer(tn, pu, pvt, r2, e2, ntab, ctab):
    tokens = tn.shape[0]
    tt = min(PEER_TOKEN_TILE, tokens)
    et = PEER_EXPERT_TILE
    n_blocks = PEER_N_EXPERTS // et
    rows_per_step = (et // PEER_N_KEYS) * PEER_HEADS
    whole_tab = pl.BlockSpec((PEER_HEADS, PEER_N_KEYS, tt), lambda t, e: (0, 0, t))
    step_tab = pl.BlockSpec((tt // HEAD_LANES, rows_per_step, HEAD_LANES),
                            lambda t, e: (t, jnp.minimum(e, n_blocks - 1), 0))
    return pl.pallas_call(
        _peer_kernel,
        grid=(tokens // tt, n_blocks + 1),
        in_specs=[pl.BlockSpec((tt, D_MODEL), lambda t, e: (t, 0)),
                  pl.BlockSpec((et, D_MODEL), lambda t, e: (jnp.minimum(e, n_blocks - 1), 0)),
                  pl.BlockSpec((D_MODEL, et), lambda t, e: (0, jnp.maximum(e - 1, 0))),
                  whole_tab, whole_tab, step_tab, step_tab],
        out_specs=pl.BlockSpec((D_MODEL, tt), lambda t, e: (0, t)),
        out_shape=jax.ShapeDtypeStruct((D_MODEL, tokens), F32),
        scratch_shapes=[pltpu.VMEM((et, tt), BF16), pltpu.VMEM((et, tt), BF16),
                        pltpu.VMEM((2, et, tt), BF16),
                        pltpu.VMEM((PEER_HEADS, PEER_N_KEYS, tt), BF16),
                        pltpu.VMEM((PEER_HEADS, PEER_N_KEYS, tt), BF16)],
        compiler_params=_params("parallel", "arbitrary"),
        name="peer_experts",
    )(tn, pu, pvt, r2, e2, ntab, ctab)


def _final_kernel(h_ref, mix_ref, gf_ref, y_ref):
    y_ref[...] = _rms(h_ref[...] + jnp.transpose(mix_ref[...]), gf_ref[...])


def _final(h, mix_t, final_g):
    rows = h.shape[0]
    tile = min(ROW_TILE, rows)
    row_spec = pl.BlockSpec((tile, D_MODEL), lambda i: (i, 0))
    return pl.pallas_call(
        _final_kernel,
        grid=(rows // tile,),
        in_specs=[row_spec, pl.BlockSpec((D_MODEL, tile), lambda i: (0, i)), _const_spec((1, D_MODEL))],
        out_specs=row_spec,
        out_shape=jax.ShapeDtypeStruct((rows, D_MODEL), F32),
        compiler_params=_params("parallel"),
        name="final_norm",
    )(h, mix_t, final_g)


def _rope_tables(pos):
    half = ROPE_DIM // 2
    inv = ROPE_THETA ** (-jnp.arange(half, dtype=F32) * 2.0 / ROPE_DIM)
    d = jnp.arange(HEAD_LANES) % DA_HEAD_DIM
    ang = pos[:, None] * inv[d % half][None, :]
    cos_t = jnp.where(d < ROPE_DIM, jnp.cos(ang), 1.0)
    sin_a = jnp.where(d < half, -jnp.sin(ang), 0.0)
    sin_b = jnp.where((d >= half) & (d < ROPE_DIM), jnp.sin(ang), 0.0)
    return cos_t, sin_a, sin_b


def _row_pipeline(x, rope_tabs, wmix, bias, attend, weights):
    (g1, w_in, gm_g, subln_g, w_o, g2, wpq, kbig, pu, pvt, final_g, attn_scale) = weights
    q1, q2, k32, v32, kb, vb, gmo, gn = _proj(x, g1, w_in, *rope_tabs, gm_g, wmix, bias)
    ao = attend(q1, q2, k32, v32, kb, vb)
    h, tn, scores = _merge(ao, gmo, x, subln_g, w_o, g2, wpq, kbig, attn_scale)
    rows = x.shape[0]
    r2, e2, ntab, ctab = _topk(scores.reshape(2, PEER_N_KEYS, PEER_HEADS, rows))
    flat = lambda t: t.reshape(rows // TOPK_LANES, PEER_N_KEYS * PEER_HEADS, TOPK_LANES)
    y = _final(h, _peer(tn, pu, pvt, r2, e2, flat(ntab), flat(ctab)), final_g)
    return y, k32, v32, gn


def kernel(x_prompt, x_sample, cache_k, cache_v, page_table, norm1_g, w_in, gm_norm_g, gm_ws, gm_b,
           da_lq1, da_lk1, da_lq2, da_lk2, da_subln_g, w_o, norm2_g, peer_wq, peer_keys, peer_u,
           peer_v, final_g):
    batch, seq, _ = x_prompt.shape
    nb, n_new, _ = x_sample.shape
    n_pages = page_table.shape[1]
    past = n_pages * PAGE_SIZE
    assert w_in.shape[0] == 1 and seq % CHUNK == 0 and seq % min(ROW_TILE, batch * seq) == 0

    lam_init = 0.8 - 0.6 * math.exp(-0.3 * 0)
    dots = lambda a, b: jnp.exp(jnp.sum(a.astype(F32) * b.astype(F32)))
    lam = (dots(da_lq1[0], da_lk1[0]) - dots(da_lq2[0], da_lk2[0]) + lam_init).reshape(1).astype(F32)

    eye = jnp.eye(PEER_HEADS, dtype=F32)
    kbig = jnp.einsum("hcjd,hg->cjhgd", peer_keys[0], eye).reshape(
        2, PEER_N_KEYS * PEER_HEADS, PEER_HEADS * PEER_HALF).astype(BF16)
    wpq = peer_wq[0].reshape(D_MODEL, PEER_HEADS, 2, PEER_HALF).transpose(0, 2, 1, 3).reshape(
        D_MODEL, 2 * PEER_HEADS * PEER_HALF).astype(BF16)
    weights = (norm1_g[0][None], w_in[0].astype(BF16), gm_norm_g[0].reshape(1, GM_WIDTH),
               da_subln_g[0][None], w_o[0].astype(BF16), norm2_g[0][None], wpq, kbig,
               peer_u[0].astype(BF16), peer_v[0].T.astype(BF16), final_g[None], 1.0 - lam_init)

    xp = x_prompt.reshape(batch * seq, D_MODEL)
    tabs_p = _rope_tables(jnp.arange(seq, dtype=F32))
    wmix_p = jnp.tril(gm_ws[0]).astype(BF16)
    bias_p = jnp.repeat(jnp.transpose(gm_b[0]), HEAD_LANES, axis=1)
    attend_p = lambda q1, q2, k32, v32, kb, vb: _prompt_attention(lam, q1, q2, kb, vb, batch, seq)
    y_p, k_p, v_p, _ = _row_pipeline(xp, tabs_p, wmix_p, bias_p, attend_p, weights)

    xs = x_sample.reshape(nb * n_new, D_MODEL)
    tabs_s = _rope_tables(jnp.tile(past + jnp.arange(n_new, dtype=F32), nb))
    rows_s = min(ROW_TILE, nb * n_new)
    w_new = jnp.tril(gm_ws[0][:, :n_new, :n_new])
    r = jnp.arange(rows_s)
    pick = (r[:, None] % n_new == jnp.arange(n_new)[None, :]).astype(F32)
    w_rows = jnp.einsum("ri,hij,cj->hrc", pick, w_new, pick, precision=lax.Precision.HIGHEST)
    wmix_s = jnp.where(r[:, None] // n_new == r[None, :] // n_new, w_rows, 0.0).astype(BF16)
    bias_s = jnp.tile(jnp.repeat(jnp.transpose(gm_b[0][:, :n_new]), HEAD_LANES, axis=1), (rows_s // n_new, 1))
    ck = cache_k[0].reshape(-1, PAGE_SIZE * DA_HEADS, HEAD_LANES)
    cv = cache_v[0].reshape(-1, PAGE_SIZE * DA_HEADS, HEAD_LANES)

    def attend_s(q1, q2, k32, v32, kb, vb):
        per_head = lambda t: jnp.transpose(t.reshape(nb, n_new, DA_HEADS, HEAD_LANES), (0, 2, 1, 3))
        q = jnp.stack([per_head(q1), per_head(q2)], axis=1).reshape(nb, 2 * DA_HEADS * n_new, HEAD_LANES)
        new_rows = lambda t: t.reshape(nb, n_new * DA_HEADS, HEAD_LANES).astype(BF16)
        ao = _decode_attention(page_table, lam, q, new_rows(k32), new_rows(v32), ck, cv)
        ao = jnp.transpose(ao.reshape(nb, DA_HEADS, n_new, HEAD_LANES), (0, 2, 1, 3))
        return ao.reshape(nb * n_new, DA_WIDTH)

    y_s, k_s, v_s, gn_s = _row_pipeline(xs, tabs_s, wmix_s, bias_s, attend_s, weights)

    head_shape = lambda b, s: (1, b, s, DA_HEADS, HEAD_LANES)
    return (y_p.reshape(batch, seq, D_MODEL), y_s.reshape(nb, n_new, D_MODEL),
            k_p.reshape(head_shape(batch, seq)), v_p.reshape(head_shape(batch, seq)),
            k_s.reshape(head_shape(nb, n_new)), v_s.reshape(head_shape(nb, n_new)),
            gn_s.reshape(1, nb, n_new, GM_HEADS, HEAD_LANES))
```

```python
import functools
import math

import jax
import jax.numpy as jnp
from jax import lax
from jax.experimental import pallas as pl
from jax.experimental.pallas import tpu as pltpu

F32 = jnp.float32
BF16 = jnp.bfloat16

D_MODEL = 1024
DA_WIDTH = 512
DA_HEADS = 4
HEAD_LANES = 128
DA_HEAD_DIM = 64
GM_WIDTH = 512
GM_HEADS = 4
CHUNK = 128
IN_WIDTH = 3 * DA_WIDTH + 2 * GM_WIDTH
ROPE_THETA = 500000.0
ROPE_DIM = 16
PAGE_SIZE = 128
PEER_HEADS = 8
PEER_N_KEYS = 128
PEER_N_EXPERTS = PEER_N_KEYS * PEER_N_KEYS
PEER_HALF = 128
PEER_TOPK = 16
NORM_EPS = 1e-6
NEG_INF = float("-inf")

VMEM_LIMIT_BYTES = 56 * 1024 * 1024

ROW_TILE = 512
ATTN_Q_TILE = 256
TOPK_LANES = 128
PEER_EXPERT_TILE = 1024
PEER_TOKEN_TILE = 1024


def _params(*semantics):
    return pltpu.CompilerParams(dimension_semantics=semantics, vmem_limit_bytes=VMEM_LIMIT_BYTES)


def _rms(x, g):
    return x * lax.rsqrt(jnp.mean(x * x, axis=-1, keepdims=True) + NORM_EPS) * g


def _gelu(x):
    return x * (0.5 * (1.0 + jnp.tanh(math.sqrt(2.0 / math.pi) * (x + 0.044715 * (x * x * x)))))


def _const_spec(shape):
    zeros = (0,) * len(shape)
    return pl.BlockSpec(shape, lambda *_: zeros)


def _proj_kernel(x_ref, g1_ref, win_ref, cos_ref, sa_ref, sb_ref, gmg_ref, wmix_ref, bias_ref,
                 q1_ref, q2_ref, k32_ref, v32_ref, kb_ref, vb_ref, gmo_ref, gn_ref, *, mix_rows):
    rows = x_ref.shape[0]
    xn = _rms(x_ref[...], g1_ref[...]).astype(BF16)
    z = jnp.dot(xn, win_ref[...], preferred_element_type=F32)
    cos_t, sin_a, sin_b = cos_ref[...], sa_ref[...], sb_ref[...]
    first_half = lax.broadcasted_iota(jnp.int32, (rows, HEAD_LANES), 1) < DA_HEAD_DIM

    def rope(t):
        return t * cos_t + pltpu.roll(t, HEAD_LANES - 8, 1) * sin_a + pltpu.roll(t, 8, 1) * sin_b

    for b in range(DA_HEADS):
        sl = slice(b * HEAD_LANES, (b + 1) * HEAD_LANES)
        q = rope(z[:, sl]) * (1.0 / math.sqrt(DA_HEAD_DIM))
        q1_ref[:, sl] = jnp.where(first_half, q, 0.0).astype(BF16)
        q2_ref[:, sl] = jnp.where(first_half, 0.0, q).astype(BF16)
        k = rope(z[:, DA_WIDTH + b * HEAD_LANES:DA_WIDTH + (b + 1) * HEAD_LANES])
        v = z[:, 2 * DA_WIDTH + b * HEAD_LANES:2 * DA_WIDTH + (b + 1) * HEAD_LANES]
        k32_ref[pl.ds(b, rows, stride=DA_HEADS), :] = k
        v32_ref[pl.ds(b, rows, stride=DA_HEADS), :] = v
        kb_ref[:, sl] = k.astype(BF16)
        vb_ref[:, sl] = v.astype(BF16)

    u = _gelu(z[:, 3 * DA_WIDTH:3 * DA_WIDTH + GM_WIDTH])
    g = _gelu(z[:, 3 * DA_WIDTH + GM_WIDTH:])
    for b in range(GM_HEADS):
        sl = slice(b * HEAD_LANES, (b + 1) * HEAD_LANES)
        gn = _rms(g[:, sl], gmg_ref[:, sl])
        gn_ref[:, sl] = gn
        gnb = gn.astype(BF16)
        for r in range(rows // mix_rows):
            rs = slice(r * mix_rows, (r + 1) * mix_rows)
            s = jnp.dot(wmix_ref[b], gnb[rs], preferred_element_type=F32) + bias_ref[:, sl]
            gmo_ref[rs, sl] = (u[rs, sl] * s).astype(BF16)


def _proj(x, g1, w_in, cos_t, sin_a, sin_b, gm_g, wmix, bias):
    rows = x.shape[0]
    tile = min(ROW_TILE, rows)
    mix_rows = wmix.shape[-1]
    row_spec = lambda w: pl.BlockSpec((tile, w), lambda i: (i, 0))
    table_blocks = cos_t.shape[0] // tile
    table_spec = pl.BlockSpec((tile, HEAD_LANES), lambda i: (i % table_blocks, 0))
    head_rows_spec = pl.BlockSpec((tile * DA_HEADS, HEAD_LANES), lambda i: (i, 0))
    out_shape = [jax.ShapeDtypeStruct((rows, DA_WIDTH), BF16),
                 jax.ShapeDtypeStruct((rows, DA_WIDTH), BF16),
                 jax.ShapeDtypeStruct((rows * DA_HEADS, HEAD_LANES), F32),
                 jax.ShapeDtypeStruct((rows * DA_HEADS, HEAD_LANES), F32),
                 jax.ShapeDtypeStruct((rows, DA_WIDTH), BF16),
                 jax.ShapeDtypeStruct((rows, DA_WIDTH), BF16),
                 jax.ShapeDtypeStruct((rows, GM_WIDTH), BF16),
                 jax.ShapeDtypeStruct((rows, GM_WIDTH), F32)]
    return pl.pallas_call(
        functools.partial(_proj_kernel, mix_rows=mix_rows),
        grid=(rows // tile,),
        in_specs=[row_spec(D_MODEL), _const_spec((1, D_MODEL)), _const_spec((D_MODEL, IN_WIDTH)),
                  table_spec, table_spec, table_spec,
                  _const_spec((1, GM_WIDTH)), _const_spec(wmix.shape), _const_spec(bias.shape)],
        out_specs=[row_spec(DA_WIDTH)] * 2 + [head_rows_spec] * 2 + [row_spec(DA_WIDTH)] * 2
                  + [row_spec(GM_WIDTH)] * 2,
        out_shape=out_shape,
        compiler_params=_params("parallel"),
        name="proj",
    )(x, g1, w_in, cos_t, sin_a, sin_b, gm_g, wmix, bias)


def _attn_kernel(lam_ref, q1_ref, q2_ref, k_ref, v_ref, o_ref):
    tq = q1_ref.shape[0]
    i = pl.program_id(2)
    q = jnp.concatenate([q1_ref[...], q2_ref[...]], axis=0)

    def step(j, carry, masked):
        m, l, acc = carry
        start = pl.multiple_of(j * tq, tq)
        kb = k_ref[pl.ds(start, tq), :]
        vb = v_ref[pl.ds(start, tq), :]
        s = lax.dot_general(q, kb, (((1,), (1,)), ((), ())), preferred_element_type=F32)
        if masked:
            row = lax.broadcasted_iota(jnp.int32, (2 * tq, tq), 0)
            col = lax.broadcasted_iota(jnp.int32, (2 * tq, tq), 1)
            row = jnp.where(row >= tq, row - tq, row)
            s = jnp.where(col <= row, s, NEG_INF)
        m_new = jnp.maximum(m, jnp.max(s, axis=-1, keepdims=True))
        alpha = jnp.exp(m - m_new)
        p = jnp.exp(s - m_new)
        l = alpha * l + jnp.sum(p, axis=-1, keepdims=True)
        acc = alpha * acc + jnp.dot(p.astype(BF16), vb, preferred_element_type=F32)
        return m_new, l, acc

    init = (jnp.full((2 * tq, 1), NEG_INF, F32), jnp.zeros((2 * tq, 1), F32),
            jnp.zeros((2 * tq, HEAD_LANES), F32))
    carry = lax.fori_loop(0, i, lambda j, c: step(j, c, False), init)
    _, l, acc = step(i, carry, True)
    o = acc / l
    o_ref[...] = o[:tq] - lam_ref[0] * o[tq:]


def _prompt_attention(lam, q1, q2, kb, vb, batch, seq):
    tq = min(ATTN_Q_TILE, seq)
    nq = seq // tq
    q_spec = pl.BlockSpec((tq, HEAD_LANES), lambda b, h, i: (b * nq + i, h))
    kv_spec = pl.BlockSpec((seq, HEAD_LANES), lambda b, h, i: (b, h))
    return pl.pallas_call(
        _attn_kernel,
        grid=(batch, DA_HEADS, nq),
        in_specs=[pl.BlockSpec(memory_space=pltpu.SMEM), q_spec, q_spec, kv_spec, kv_spec],
        out_specs=q_spec,
        out_shape=jax.ShapeDtypeStruct((batch * seq, DA_WIDTH), F32),
        compiler_params=_params("parallel", "parallel", "arbitrary"),
        name="prompt_attn",
    )(lam, q1, q2, kb, vb)


def _decode_attn_kernel(pt_ref, lam_ref, q_ref, kn_ref, vn_ref, *rest, n_pages, n_new):
    del pt_ref
    k_pages, v_pages = rest[:n_pages], rest[n_pages:2 * n_pages]
    o_ref, k_all, v_all = rest[2 * n_pages:]
    page_rows = PAGE_SIZE * DA_HEADS
    for p in range(n_pages):
        rs = slice(p * page_rows, (p + 1) * page_rows)
        k_all[rs, :] = k_pages[p][0].astype(BF16)
        v_all[rs, :] = v_pages[p][0].astype(BF16)
    q = q_ref[0]
    nrow = q.shape[0]
    contract_last = (((1,), (1,)), ((), ()))
    s_past = lax.dot_general(q, k_all[...], contract_last, preferred_element_type=F32)
    s_new = lax.dot_general(q, kn_ref[0], contract_last, preferred_element_type=F32)

    def row_col(shape):
        row = lax.broadcasted_iota(jnp.int32, shape, 0)
        col = lax.broadcasted_iota(jnp.int32, shape, 1)
        return (row // n_new) % DA_HEADS, row % n_new, col % DA_HEADS, col // DA_HEADS

    q_head, _, k_head, _ = row_col(s_past.shape)
    s_past = jnp.where(q_head == k_head, s_past, NEG_INF)
    q_head, q_tok, k_head, k_tok = row_col(s_new.shape)
    s_new = jnp.where(q_head == k_head, jnp.where(k_tok <= q_tok, s_new, NEG_INF), NEG_INF)
    m = jnp.maximum(jnp.max(s_past, axis=-1, keepdims=True), jnp.max(s_new, axis=-1, keepdims=True))
    p_past = jnp.exp(s_past - m)
    p_new = jnp.exp(s_new - m)
    inv = 1.0 / (jnp.sum(p_past, axis=-1, keepdims=True) + jnp.sum(p_new, axis=-1, keepdims=True))
    half = nrow // 2
    lam = lam_ref[0]
    a_past = (p_past[:half] * inv[:half] - lam * (p_past[half:] * inv[half:])).astype(BF16)
    a_new = (p_new[:half] * inv[:half] - lam * (p_new[half:] * inv[half:])).astype(BF16)
    o_ref[0] = (jnp.dot(a_past, v_all[...], preferred_element_type=F32)
                + jnp.dot(a_new, vn_ref[0], preferred_element_type=F32))


def _decode_attention(page_table, lam, q_rows, k_new, v_new, cache_k, cache_v):
    nb, n_pages = page_table.shape
    n_new = k_new.shape[1] // DA_HEADS
    page_rows = PAGE_SIZE * DA_HEADS
    page_spec = lambda p: pl.BlockSpec((1, page_rows, HEAD_LANES), lambda b, pt: (pt[b * n_pages + p], 0, 0))
    per_b = lambda r: pl.BlockSpec((1, r, HEAD_LANES), lambda b, pt: (b, 0, 0))
    grid_spec = pltpu.PrefetchScalarGridSpec(
        num_scalar_prefetch=1,
        grid=(nb,),
        in_specs=[pl.BlockSpec(memory_space=pltpu.SMEM), per_b(q_rows.shape[1]),
                  per_b(n_new * DA_HEADS), per_b(n_new * DA_HEADS)]
                 + [page_spec(p) for p in range(n_pages)] * 2,
        out_specs=per_b(n_new * DA_HEADS),
        scratch_shapes=[pltpu.VMEM((n_pages * page_rows, HEAD_LANES), BF16)] * 2,
    )
    return pl.pallas_call(
        functools.partial(_decode_attn_kernel, n_pages=n_pages, n_new=n_new),
        grid_spec=grid_spec,
        out_shape=jax.ShapeDtypeStruct((nb, n_new * DA_HEADS, HEAD_LANES), F32),
        compiler_params=_params("arbitrary"),
        name="decode_attn",
    )(page_table.reshape(-1), lam, q_rows, k_new, v_new,
      *([cache_k] * n_pages), *([cache_v] * n_pages))


def _merge_kernel(ao_ref, gmo_ref, x_ref, sg_ref, wo_ref, g2_ref, wpq_ref, kbig_ref,
                  h_ref, tn_ref, s_ref, *, attn_scale):
    parts = []
    for b in range(DA_HEADS):
        sl = slice(b * HEAD_LANES, (b + 1) * HEAD_LANES)
        parts.append((_rms(ao_ref[:, sl], sg_ref[...]) * attn_scale).astype(BF16))
    cat = jnp.concatenate(parts + [gmo_ref[...]], axis=1)
    h = x_ref[...] + jnp.dot(cat, wo_ref[...], preferred_element_type=F32)
    h_ref[...] = h
    tn = _rms(h, g2_ref[...]).astype(BF16)
    tn_ref[...] = tn
    qp = jnp.dot(tn, wpq_ref[...], preferred_element_type=F32).astype(BF16)
    width = PEER_HEADS * PEER_HALF
    for c in range(2):
        s_ref[c] = lax.dot_general(kbig_ref[c], qp[:, c * width:(c + 1) * width],
                                   (((1,), (1,)), ((), ())), preferred_element_type=F32)


def _merge(ao, gmo, x, subln_g, w_o, g2, wpq, kbig, attn_scale):
    rows = x.shape[0]
    tile = min(ROW_TILE, rows)
    width = PEER_HEADS * PEER_HALF
    row_spec = lambda w: pl.BlockSpec((tile, w), lambda i: (i, 0))
    return pl.pallas_call(
        functools.partial(_merge_kernel, attn_scale=attn_scale),
        grid=(rows // tile,),
        in_specs=[row_spec(DA_WIDTH), row_spec(GM_WIDTH), row_spec(D_MODEL), _const_spec((1, HEAD_LANES)),
                  _const_spec((D_MODEL, D_MODEL)), _const_spec((1, D_MODEL)),
                  _const_spec((D_MODEL, 2 * width)), _const_spec((2, PEER_N_KEYS * PEER_HEADS, width))],
        out_specs=[row_spec(D_MODEL), row_spec(D_MODEL),
                   pl.BlockSpec((2, PEER_N_KEYS * PEER_HEADS, tile), lambda i: (0, 0, i))],
        out_shape=[jax.ShapeDtypeStruct((rows, D_MODEL), F32),
                   jax.ShapeDtypeStruct((rows, D_MODEL), BF16),
                   jax.ShapeDtypeStruct((2, PEER_N_KEYS * PEER_HEADS, rows), F32)],
        compiler_params=_params("parallel"),
        name="merge",
    )(ao, gmo, x, subln_g, w_o, g2, wpq, kbig)


def _young_candidates():
    return [(p, q) for p in range(PEER_TOPK) for q in range(PEER_TOPK) if (p + 1) * (q + 1) <= PEER_TOPK]


def _topk_kernel(s_ref, r2_ref, e2_ref, n_ref, c_ref, work_ref, top_ref, idx_ref, young_ref,
                 rank_ref, exp_ref):
    nk = PEER_N_KEYS
    tile = s_ref.shape[2:]
    one, zero = jnp.ones(tile, F32), jnp.zeros(tile, F32)

    def extract_distinct(c):
        def round_(r, bound):
            tree = [jnp.where(s_ref[c, j] < bound, s_ref[c, j], NEG_INF) for j in range(nk)]
            while len(tree) > 1:
                tree = [jnp.maximum(tree[t], tree[t + 1]) for t in range(0, len(tree), 2)]
            top_ref[c, r] = tree[0]
            return tree[0]

        last = lax.fori_loop(0, PEER_TOPK, round_, jnp.full(tile, float("inf"), F32))
        reached = zero
        for j in range(nk):
            reached = reached + jnp.where(s_ref[c, j] >= last, one, zero)
        return reached

    def extract(c):
        for j in range(nk):
            work_ref[j] = s_ref[c, j]

        def round_(r, _):
            vals = [work_ref[j] for j in range(nk)]
            idxs = [None] * nk
            width = 1
            while len(vals) > 1:
                nv, ni = [], []
                for t in range(0, len(vals), 2):
                    a, b = vals[t], vals[t + 1]
                    take_b = b > a
                    ia = idxs[t] if idxs[t] is not None else float(t * width)
                    ib = idxs[t + 1] if idxs[t + 1] is not None else float((t + 1) * width)
                    nv.append(jnp.maximum(a, b))
                    ni.append(jnp.where(take_b, ib, ia))
                vals, idxs = nv, ni
                width *= 2
            best, best_idx = vals[0], idxs[0]
            top_ref[c, r] = best
            idx_ref[c, r] = best_idx
            for j in range(nk):
                work_ref[j] = jnp.where(best_idx == float(j), NEG_INF, work_ref[j])
            return 0

        lax.fori_loop(0, PEER_TOPK, round_, 0)

    miscount = jnp.maximum(jnp.abs(extract_distinct(0) - float(PEER_TOPK)),
                           jnp.abs(extract_distinct(1) - float(PEER_TOPK)))
    tied = jnp.max(miscount) > 0.0

    @pl.when(tied)
    def _():
        extract(0)
        extract(1)

    a = [top_ref[0, p] for p in range(PEER_TOPK)]
    b = [top_ref[1, q] for q in range(PEER_TOPK)]
    cands = _young_candidates()
    cand = {pq: a[pq[0]] + b[pq[1]] for pq in cands}
    n_of_p = [zero] * PEER_TOPK
    ea = [jnp.exp(a[p] - a[0]) for p in range(PEER_TOPK)]
    eb = [jnp.exp(b[q] - b[0]) for q in range(PEER_TOPK)]
    z = zero
    for (p, q) in cands:
        fixed = 0
        beaten = zero
        for (p2, q2) in cands:
            if (p2, q2) == (p, q):
                continue
            if p2 <= p and q2 <= q:
                fixed += 1
            elif p2 >= p and q2 >= q:
                continue
            elif p2 * PEER_TOPK + q2 < p * PEER_TOPK + q:
                beaten = beaten + jnp.where(cand[(p2, q2)] >= cand[(p, q)], one, zero)
            else:
                beaten = beaten + jnp.where(cand[(p2, q2)] > cand[(p, q)], one, zero)
        sel = jnp.where(beaten + float(fixed) < float(PEER_TOPK), one, zero)
        n_of_p[p] = n_of_p[p] + sel
        z = z + sel * (ea[p] * eb[q])
    inv_z = 1.0 / z
    for p in range(PEER_TOPK):
        young_ref[p] = n_of_p[p]

    def scatter(hit_a, hit_b):
        for j in range(nk):
            n_j = zero
            r_j = jnp.full(tile, float(nk - 1), F32)
            for p in range(PEER_TOPK):
                n_j = jnp.where(hit_a(p, j), young_ref[p], n_j)
                r_j = jnp.where(hit_b(p, j), float(p), r_j)
            n_ref[j] = n_j
            rank_ref[j * PEER_HEADS:(j + 1) * PEER_HEADS, :] = r_j

    @pl.when(jnp.logical_not(tied))
    def _():
        scatter(lambda p, j: top_ref[0, p] == s_ref[0, j], lambda p, j: top_ref[1, p] == s_ref[1, j])

    @pl.when(tied)
    def _():
        scatter(lambda p, j: idx_ref[0, p] == float(j), lambda p, j: idx_ref[1, p] == float(j))

    for j in range(nk):
        c_ref[j] = jnp.exp(s_ref[0, j] - a[0]) * inv_z
        exp_ref[j * PEER_HEADS:(j + 1) * PEER_HEADS, :] = jnp.exp(s_ref[1, j] - b[0])
    for h in range(PEER_HEADS):
        r2_ref[h] = rank_ref[pl.ds(h, nk, stride=PEER_HEADS), :].astype(BF16)
        e2_ref[h] = exp_ref[pl.ds(h, nk, stride=PEER_HEADS), :].astype(BF16)


def _topk(scores):
    tokens = scores.shape[-1]
    lanes = TOPK_LANES
    tab2 = jax.ShapeDtypeStruct((PEER_HEADS, PEER_N_KEYS, tokens), BF16)
    tab2_spec = pl.BlockSpec((PEER_HEADS, PEER_N_KEYS, lanes), lambda i: (0, 0, i))
    tab1 = jax.ShapeDtypeStruct((tokens // lanes, PEER_N_KEYS, PEER_HEADS, lanes), F32)
    tab1_spec = pl.BlockSpec((None, PEER_N_KEYS, PEER_HEADS, lanes), lambda i: (i, 0, 0, 0))
    return pl.pallas_call(
        _topk_kernel,
        grid=(tokens // lanes,),
        in_specs=[pl.BlockSpec((2, PEER_N_KEYS, PEER_HEADS, lanes), lambda i: (0, 0, 0, i))],
        out_specs=[tab2_spec] * 2 + [tab1_spec] * 2,
        out_shape=[tab2] * 2 + [tab1] * 2,
        scratch_shapes=[pltpu.VMEM((PEER_N_KEYS, PEER_HEADS, lanes), F32),
                        pltpu.VMEM((2, PEER_TOPK, PEER_HEADS, lanes), F32),
                        pltpu.VMEM((2, PEER_TOPK, PEER_HEADS, lanes), F32),
                        pltpu.VMEM((PEER_TOPK, PEER_HEADS, lanes), F32),
                        pltpu.VMEM((PEER_N_KEYS * PEER_HEADS, lanes), F32),
                        pltpu.VMEM((PEER_N_KEYS * PEER_HEADS, lanes), F32)],
        compiler_params=_params("parallel"),
        name="peer_topk",
    )(scores)


def _peer_kernel(tn_ref, pu_ref, pvt_ref, r2_ref, e2_ref, n_ref, c_ref, acc_ref,
                 at_ref, gate_ref, hid_ref, r2s_ref, e2s_ref):
    e = pl.program_id(1)
    tt = tn_ref.shape[0]
    nk = PEER_N_KEYS
    keys_per_step = pu_ref.shape[0] // nk
    lane_tiles = [(t, slice(t * HEAD_LANES, (t + 1) * HEAD_LANES)) for t in range(tt // HEAD_LANES)]
    pack = 16

    def first_matmul_and_gate(slot):
        at_ref[...] = lax.dot_general(pu_ref[...], tn_ref[...], (((1,), (1,)), ((), ())),
                                      preferred_element_type=F32).astype(BF16)
        for ii in range(keys_per_step):
            for t, lanes in lane_tiles:
                gate = jnp.zeros((nk, HEAD_LANES), BF16)
                for h in range(PEER_HEADS):
                    row = ii * PEER_HEADS + h
                    n = jnp.broadcast_to(n_ref[t, row:row + 1, :], (pack, HEAD_LANES)).astype(BF16)
                    c = jnp.broadcast_to(c_ref[t, row:row + 1, :], (pack, HEAD_LANES)).astype(BF16)
                    n = jnp.tile(n, (nk // pack, 1))
                    c = jnp.tile(c, (nk // pack, 1))
                    gate = gate + jnp.where(r2s_ref[h, :, lanes] < n, e2s_ref[h, :, lanes] * c,
                                            jnp.zeros_like(gate))
                gate_ref[ii * nk:(ii + 1) * nk, lanes] = gate
        hid_ref[slot] = gate_ref[...] * _gelu(at_ref[...])

    def second_matmul(slot):
        acc_ref[...] += jnp.dot(pvt_ref[...], hid_ref[slot], preferred_element_type=F32)

    @pl.when(e == 0)
    def _():
        r2s_ref[...] = r2_ref[...]
        e2s_ref[...] = e2_ref[...]
        acc_ref[...] = jnp.zeros_like(acc_ref)
        hid_ref[1] = jnp.zeros(hid_ref.shape[1:], BF16)

    slot = e % 2
    first_matmul_and_gate(slot)
    second_matmul(1 - slot)


def _peer(tn, pu, pvt, r2, e2, ntab, ctab):
    tokens = tn.shape[0]
    tt = min(PEER_TOKEN_TILE, tokens)
    et = PEER_EXPERT_TILE
    n_blocks = PEER_N_EXPERTS // et
    rows_per_step = (et // PEER_N_KEYS) * PEER_HEADS
    whole_tab = pl.BlockSpec((PEER_HEADS, PEER_N_KEYS, tt), lambda t, e: (0, 0, t))
    step_tab = pl.BlockSpec((tt // HEAD_LANES, rows_per_step, HEAD_LANES),
                            lambda t, e: (t, jnp.minimum(e, n_blocks - 1), 0))
    return pl.pallas_call(
        _peer_kernel,
        grid=(tokens // tt, n_blocks + 1),
        in_specs=[pl.BlockSpec((tt, D_MODEL), lambda t, e: (t, 0)),
                  pl.BlockSpec((et, D_MODEL), lambda t, e: (jnp.minimum(e, n_blocks - 1), 0)),
                  pl.BlockSpec((D_MODEL, et), lambda t, e: (0, jnp.maximum(e - 1, 0))),
                  whole_tab, whole_tab, step_tab, step_tab],
        out_specs=pl.BlockSpec((D_MODEL, tt), lambda t, e: (0, t)),
        out_shape=jax.ShapeDtypeStruct((D_MODEL, tokens), F32),
        scratch_shapes=[pltpu.VMEM((et, tt), BF16), pltpu.VMEM((et, tt), BF16),
                        pltpu.VMEM((2, et, tt), BF16),
                        pltpu.VMEM((PEER_HEADS, PEER_N_KEYS, tt), BF16),
                        pltpu.VMEM((PEER_HEADS, PEER_N_KEYS, tt), BF16)],
        compiler_params=_params("parallel", "arbitrary"),
        name="peer_experts",
    )(tn, pu, pvt, r2, e2, ntab, ctab)


def _final_kernel(h_ref, mix_ref, gf_ref, y_ref):
    y_ref[...] = _rms(h_ref[...] + jnp.transpose(mix_ref[...]), gf_ref[...])


def _final(h, mix_t, final_g):
    rows = h.shape[0]
    tile = min(ROW_TILE, rows)
    row_spec = pl.BlockSpec((tile, D_MODEL), lambda i: (i, 0))
    return pl.pallas_call(
        _final_kernel,
        grid=(rows // tile,),
        in_specs=[row_spec, pl.BlockSpec((D_MODEL, tile), lambda i: (0, i)), _const_spec((1, D_MODEL))],
        out_specs=row_spec,
        out_shape=jax.ShapeDtypeStruct((rows, D_MODEL), F32),
        compiler_params=_params("parallel"),
        name="final_norm",
    )(h, mix_t, final_g)


def _rope_tables(pos):
    half = ROPE_DIM // 2
    inv = ROPE_THETA ** (-jnp.arange(half, dtype=F32) * 2.0 / ROPE_DIM)
    d = jnp.arange(HEAD_LANES) % DA_HEAD_DIM
    ang = pos[:, None] * inv[d % half][None, :]
    cos_t = jnp.where(d < ROPE_DIM, jnp.cos(ang), 1.0)
    sin_a = jnp.where(d < half, -jnp.sin(ang), 0.0)
    sin_b = jnp.where((d >= half) & (d < ROPE_DIM), jnp.sin(ang), 0.0)
    return cos_t, sin_a, sin_b


def _row_pipeline(x, rope_tabs, wmix, bias, attend, weights):
    (g1, w_in, gm_g, subln_g, w_o, g2, wpq, kbig, pu, pvt, final_g, attn_scale) = weights
    q1, q2, k32, v32, kb, vb, gmo, gn = _proj(x, g1, w_in, *rope_tabs, gm_g, wmix, bias)
    ao = attend(q1, q2, k32, v32, kb, vb)
    h, tn, scores = _merge(ao, gmo, x, subln_g, w_o, g2, wpq, kbig, attn_scale)
    rows = x.shape[0]
    r2, e2, ntab, ctab = _topk(scores.reshape(2, PEER_N_KEYS, PEER_HEADS, rows))
    flat = lambda t: t.reshape(rows // TOPK_LANES, PEER_N_KEYS * PEER_HEADS, TOPK_LANES)
    y = _final(h, _peer(tn, pu, pvt, r2, e2, flat(ntab), flat(ctab)), final_g)
    return y, k32, v32, gn


def kernel(x_prompt, x_sample, cache_k, cache_v, page_table, norm1_g, w_in, gm_norm_g, gm_ws, gm_b,
           da_lq1, da_lk1, da_lq2, da_lk2, da_subln_g, w_o, norm2_g, peer_wq, peer_keys, peer_u,
           peer_v, final_g):
    batch, seq, _ = x_prompt.shape
    nb, n_new, _ = x_sample.shape
    n_pages = page_table.shape[1]
    past = n_pages * PAGE_SIZE
    assert w_in.shape[0] == 1 and seq % CHUNK == 0 and seq % min(ROW_TILE, batch * seq) == 0

    lam_init = 0.8 - 0.6 * math.exp(-0.3 * 0)
    dots = lambda a, b: jnp.exp(jnp.sum(a.astype(F32) * b.astype(F32)))
    lam = (dots(da_lq1[0], da_lk1[0]) - dots(da_lq2[0], da_lk2[0]) + lam_init).reshape(1).astype(F32)

    eye = jnp.eye(PEER_HEADS, dtype=F32)
    kbig = jnp.einsum("hcjd,hg->cjhgd", peer_keys[0], eye).reshape(
        2, PEER_N_KEYS * PEER_HEADS, PEER_HEADS * PEER_HALF).astype(BF16)
    wpq = peer_wq[0].reshape(D_MODEL, PEER_HEADS, 2, PEER_HALF).transpose(0, 2, 1, 3).reshape(
        D_MODEL, 2 * PEER_HEADS * PEER_HALF).astype(BF16)
    weights = (norm1_g[0][None], w_in[0].astype(BF16), gm_norm_g[0].reshape(1, GM_WIDTH),
               da_subln_g[0][None], w_o[0].astype(BF16), norm2_g[0][None], wpq, kbig,
               peer_u[0].astype(BF16), peer_v[0].T.astype(BF16), final_g[None], 1.0 - lam_init)

    xp = x_prompt.reshape(batch * seq, D_MODEL)
    tabs_p = _rope_tables(jnp.arange(seq, dtype=F32))
    wmix_p = jnp.tril(gm_ws[0]).astype(BF16)
    bias_p = jnp.repeat(jnp.transpose(gm_b[0]), HEAD_LANES, axis=1)
    attend_p = lambda q1, q2, k32, v32, kb, vb: _prompt_attention(lam, q1, q2, kb, vb, batch, seq)
    y_p, k_p, v_p, _ = _row_pipeline(xp, tabs_p, wmix_p, bias_p, attend_p, weights)

    xs = x_sample.reshape(nb * n_new, D_MODEL)
    tabs_s = _rope_tables(jnp.tile(past + jnp.arange(n_new, dtype=F32), nb))
    rows_s = min(ROW_TILE, nb * n_new)
    w_new = jnp.tril(gm_ws[0][:, :n_new, :n_new])
    r = jnp.arange(rows_s)
    pick = (r[:, None] % n_new == jnp.arange(n_new)[None, :]).astype(F32)
    w_rows = jnp.einsum("ri,hij,cj->hrc", pick, w_new, pick, precision=lax.Precision.HIGHEST)
    wmix_s = jnp.where(r[:, None] // n_new == r[None, :] // n_new, w_rows, 0.0).astype(BF16)
    bias_s = jnp.tile(jnp.repeat(jnp.transpose(gm_b[0][:, :n_new]), HEAD_LANES, axis=1), (rows_s // n_new, 1))
    ck = cache_k[0].reshape(-1, PAGE_SIZE * DA_HEADS, HEAD_LANES)
    cv = cache_v[0].reshape(-1, PAGE_SIZE * DA_HEADS, HEAD_LANES)

    def attend_s(q1, q2, k32, v32, kb, vb):
        per_head = lambda t: jnp.transpose(t.reshape(nb, n_new, DA_HEADS, HEAD_LANES), (0, 2, 1, 3))
        q = jnp.stack([per_head(q1), per_head(q2)], axis=1).reshape(nb, 2 * DA_HEADS * n_new, HEAD_LANES)
        new_rows = lambda t: t.reshape(nb, n_new * DA_HEADS, HEAD_LANES).astype(BF16)
        ao = _decode_attention(page_table, lam, q, new_rows(k32), new_rows(v32), ck, cv)
        ao = jnp.transpose(ao.reshape(nb, DA_HEADS, n_new, HEAD_LANES), (0, 2, 1, 3))
        return ao.reshape(nb * n_new, DA_WIDTH)

    y_s, k_s, v_s, gn_s = _row_pipeline(xs, tabs_s, wmix_s, bias_s, attend_s, weights)

    head_shape = lambda b, s: (1, b, s, DA_HEADS, HEAD_LANES)
    return (y_p.reshape(batch, seq, D_MODEL), y_s.reshape(nb, n_new, D_MODEL),
            k_p.reshape(head_shape(batch, seq)), v_p.reshape(head_shape(batch, seq)),
            k_s.reshape(head_shape(nb, n_new)), v_s.reshape(head_shape(nb, n_new)),
            gn_s.reshape(1, nb, n_new, GM_HEADS, HEAD_LANES))
```

```python
import functools
import math

import jax
import jax.numpy as jnp
from jax import lax
from jax.experimental import pallas as pl
from jax.experimental.pallas import tpu as pltpu

F32 = jnp.float32
BF16 = jnp.bfloat16

D_MODEL = 1024
DA_WIDTH = 512
DA_HEADS = 4
HEAD_LANES = 128
BF16_TILE_ROWS = 16
DA_HEAD_DIM = 64
GM_WIDTH = 512
GM_HEADS = 4
CHUNK = 128
IN_WIDTH = 3 * DA_WIDTH + 2 * GM_WIDTH
ROPE_THETA = 500000.0
ROPE_DIM = 16
PAGE_SIZE = 128
PEER_HEADS = 8
PEER_N_KEYS = 128
PEER_N_EXPERTS = PEER_N_KEYS * PEER_N_KEYS
PEER_HALF = 128
PEER_TOPK = 16
NORM_EPS = 1e-6
NEG_INF = float("-inf")

VMEM_LIMIT_BYTES = 56 * 1024 * 1024

ROW_TILE = 512
ATTN_Q_TILE = 256
TOPK_LANES = 128
PEER_EXPERT_TILE = 1024
PEER_TOKEN_TILE = 1024


def _params(*semantics):
    return pltpu.CompilerParams(dimension_semantics=semantics, vmem_limit_bytes=VMEM_LIMIT_BYTES)


def _rms(x, g):
    return x * lax.rsqrt(jnp.mean(x * x, axis=-1, keepdims=True) + NORM_EPS) * g


def _gelu(x):
    return x * (0.5 * (1.0 + jnp.tanh(math.sqrt(2.0 / math.pi) * (x + 0.044715 * (x * x * x)))))


def _const_spec(shape):
    zeros = (0,) * len(shape)
    return pl.BlockSpec(shape, lambda *_: zeros)


def _proj_kernel(x_ref, g1_ref, win_ref, cos_ref, sa_ref, sb_ref, gmg_ref, wmix_ref, bias_ref,
                 q1_ref, q2_ref, k32_ref, v32_ref, kb_ref, vb_ref, gmo_ref, gn_ref, *, mix_rows):
    rows = x_ref.shape[0]
    xn = _rms(x_ref[...], g1_ref[...]).astype(BF16)
    z = jnp.dot(xn, win_ref[...], preferred_element_type=F32)
    cos_t, sin_a, sin_b = cos_ref[...], sa_ref[...], sb_ref[...]
    first_half = lax.broadcasted_iota(jnp.int32, (rows, HEAD_LANES), 1) < DA_HEAD_DIM

    def rope(t):
        return t * cos_t + pltpu.roll(t, HEAD_LANES - 8, 1) * sin_a + pltpu.roll(t, 8, 1) * sin_b

    for b in range(DA_HEADS):
        sl = slice(b * HEAD_LANES, (b + 1) * HEAD_LANES)
        q = rope(z[:, sl]) * (1.0 / math.sqrt(DA_HEAD_DIM))
        q1_ref[:, sl] = jnp.where(first_half, q, 0.0).astype(BF16)
        q2_ref[:, sl] = jnp.where(first_half, 0.0, q).astype(BF16)
        k = rope(z[:, DA_WIDTH + b * HEAD_LANES:DA_WIDTH + (b + 1) * HEAD_LANES])
        v = z[:, 2 * DA_WIDTH + b * HEAD_LANES:2 * DA_WIDTH + (b + 1) * HEAD_LANES]
        k32_ref[pl.ds(b, rows, stride=DA_HEADS), :] = k
        v32_ref[pl.ds(b, rows, stride=DA_HEADS), :] = v
        kb_ref[:, sl] = k.astype(BF16)
        vb_ref[:, sl] = v.astype(BF16)

    u = _gelu(z[:, 3 * DA_WIDTH:3 * DA_WIDTH + GM_WIDTH])
    g = _gelu(z[:, 3 * DA_WIDTH + GM_WIDTH:])
    for b in range(GM_HEADS):
        sl = slice(b * HEAD_LANES, (b + 1) * HEAD_LANES)
        gn = _rms(g[:, sl], gmg_ref[:, sl])
        gn_ref[:, sl] = gn
        gnb = gn.astype(BF16)
        for r in range(rows // mix_rows):
            rs = slice(r * mix_rows, (r + 1) * mix_rows)
            s = jnp.dot(wmix_ref[b], gnb[rs], preferred_element_type=F32) + bias_ref[:, sl]
            gmo_ref[rs, sl] = (u[rs, sl] * s).astype(BF16)


def _proj(x, g1, w_in, cos_t, sin_a, sin_b, gm_g, wmix, bias):
    rows = x.shape[0]
    tile = min(ROW_TILE, rows)
    mix_rows = wmix.shape[-1]
    row_spec = lambda w: pl.BlockSpec((tile, w), lambda i: (i, 0))
    table_blocks = cos_t.shape[0] // tile
    table_spec = pl.BlockSpec((tile, HEAD_LANES), lambda i: (i % table_blocks, 0))
    head_rows_spec = pl.BlockSpec((tile * DA_HEADS, HEAD_LANES), lambda i: (i, 0))
    out_shape = [jax.ShapeDtypeStruct((rows, DA_WIDTH), BF16),
                 jax.ShapeDtypeStruct((rows, DA_WIDTH), BF16),
                 jax.ShapeDtypeStruct((rows * DA_HEADS, HEAD_LANES), F32),
                 jax.ShapeDtypeStruct((rows * DA_HEADS, HEAD_LANES), F32),
                 jax.ShapeDtypeStruct((rows, DA_WIDTH), BF16),
                 jax.ShapeDtypeStruct((rows, DA_WIDTH), BF16),
                 jax.ShapeDtypeStruct((rows, GM_WIDTH), BF16),
                 jax.ShapeDtypeStruct((rows, GM_WIDTH), F32)]
    return pl.pallas_call(
        functools.partial(_proj_kernel, mix_rows=mix_rows),
        grid=(rows // tile,),
        in_specs=[row_spec(D_MODEL), _const_spec((1, D_MODEL)), _const_spec((D_MODEL, IN_WIDTH)),
                  table_spec, table_spec, table_spec,
                  _const_spec((1, GM_WIDTH)), _const_spec(wmix.shape), _const_spec(bias.shape)],
        out_specs=[row_spec(DA_WIDTH)] * 2 + [head_rows_spec] * 2 + [row_spec(DA_WIDTH)] * 2
                  + [row_spec(GM_WIDTH)] * 2,
        out_shape=out_shape,
        compiler_params=_params("parallel"),
        name="proj",
    )(x, g1, w_in, cos_t, sin_a, sin_b, gm_g, wmix, bias)


def _attn_kernel(lam_ref, q1_ref, q2_ref, k_ref, v_ref, o_ref):
    tq = q1_ref.shape[0]
    i = pl.program_id(2)
    q = jnp.concatenate([q1_ref[...], q2_ref[...]], axis=0)

    def step(j, carry, masked):
        m, l, acc = carry
        start = pl.multiple_of(j * tq, tq)
        kb = k_ref[pl.ds(start, tq), :]
        vb = v_ref[pl.ds(start, tq), :]
        s = lax.dot_general(q, kb, (((1,), (1,)), ((), ())), preferred_element_type=F32)
        if masked:
            row = lax.broadcasted_iota(jnp.int32, (2 * tq, tq), 0)
            col = lax.broadcasted_iota(jnp.int32, (2 * tq, tq), 1)
            row = jnp.where(row >= tq, row - tq, row)
            s = jnp.where(col <= row, s, NEG_INF)
        m_new = jnp.maximum(m, jnp.max(s, axis=-1, keepdims=True))
        alpha = jnp.exp(m - m_new)
        p = jnp.exp(s - m_new)
        l = alpha * l + jnp.sum(p, axis=-1, keepdims=True)
        acc = alpha * acc + jnp.dot(p.astype(BF16), vb, preferred_element_type=F32)
        return m_new, l, acc

    init = (jnp.full((2 * tq, 1), NEG_INF, F32), jnp.zeros((2 * tq, 1), F32),
            jnp.zeros((2 * tq, HEAD_LANES), F32))
    carry = lax.fori_loop(0, i, lambda j, c: step(j, c, False), init)
    _, l, acc = step(i, carry, True)
    o = acc / l
    o_ref[...] = o[:tq] - lam_ref[0] * o[tq:]


def _prompt_attention(lam, q1, q2, kb, vb, batch, seq):
    tq = min(ATTN_Q_TILE, seq)
    nq = seq // tq
    q_spec = pl.BlockSpec((tq, HEAD_LANES), lambda b, h, i: (b * nq + i, h))
    kv_spec = pl.BlockSpec((seq, HEAD_LANES), lambda b, h, i: (b, h))
    return pl.pallas_call(
        _attn_kernel,
        grid=(batch, DA_HEADS, nq),
        in_specs=[pl.BlockSpec(memory_space=pltpu.SMEM), q_spec, q_spec, kv_spec, kv_spec],
        out_specs=q_spec,
        out_shape=jax.ShapeDtypeStruct((batch * seq, DA_WIDTH), F32),
        compiler_params=_params("parallel", "parallel", "arbitrary"),
        name="prompt_attn",
    )(lam, q1, q2, kb, vb)


def _decode_attn_kernel(pt_ref, lam_ref, q_ref, kn_ref, vn_ref, *rest, n_pages, n_new):
    del pt_ref
    k_pages, v_pages = rest[:n_pages], rest[n_pages:2 * n_pages]
    o_ref, k_all, v_all = rest[2 * n_pages:]
    page_rows = PAGE_SIZE * DA_HEADS
    for p in range(n_pages):
        rs = slice(p * page_rows, (p + 1) * page_rows)
        k_all[rs, :] = k_pages[p][0].astype(BF16)
        v_all[rs, :] = v_pages[p][0].astype(BF16)
    q = q_ref[0]
    nrow = q.shape[0]
    contract_last = (((1,), (1,)), ((), ()))
    s_past = lax.dot_general(q, k_all[...], contract_last, preferred_element_type=F32)
    s_new = lax.dot_general(q, kn_ref[0], contract_last, preferred_element_type=F32)

    def row_col(shape):
        row = lax.broadcasted_iota(jnp.int32, shape, 0)
        col = lax.broadcasted_iota(jnp.int32, shape, 1)
        return (row // n_new) % DA_HEADS, row % n_new, col % DA_HEADS, col // DA_HEADS

    q_head, _, k_head, _ = row_col(s_past.shape)
    s_past = jnp.where(q_head == k_head, s_past, NEG_INF)
    q_head, q_tok, k_head, k_tok = row_col(s_new.shape)
    s_new = jnp.where(q_head == k_head, jnp.where(k_tok <= q_tok, s_new, NEG_INF), NEG_INF)
    m = jnp.maximum(jnp.max(s_past, axis=-1, keepdims=True), jnp.max(s_new, axis=-1, keepdims=True))
    p_past = jnp.exp(s_past - m)
    p_new = jnp.exp(s_new - m)
    inv = 1.0 / (jnp.sum(p_past, axis=-1, keepdims=True) + jnp.sum(p_new, axis=-1, keepdims=True))
    half = nrow // 2
    lam = lam_ref[0]
    a_past = (p_past[:half] * inv[:half] - lam * (p_past[half:] * inv[half:])).astype(BF16)
    a_new = (p_new[:half] * inv[:half] - lam * (p_new[half:] * inv[half:])).astype(BF16)
    o_ref[0] = (jnp.dot(a_past, v_all[...], preferred_element_type=F32)
                + jnp.dot(a_new, vn_ref[0], preferred_element_type=F32))


def _decode_attention(page_table, lam, q_rows, k_new, v_new, cache_k, cache_v):
    nb, n_pages = page_table.shape
    n_new = k_new.shape[1] // DA_HEADS
    page_rows = PAGE_SIZE * DA_HEADS
    page_spec = lambda p: pl.BlockSpec((1, page_rows, HEAD_LANES), lambda b, pt: (pt[b * n_pages + p], 0, 0))
    per_b = lambda r: pl.BlockSpec((1, r, HEAD_LANES), lambda b, pt: (b, 0, 0))
    grid_spec = pltpu.PrefetchScalarGridSpec(
        num_scalar_prefetch=1,
        grid=(nb,),
        in_specs=[pl.BlockSpec(memory_space=pltpu.SMEM), per_b(q_rows.shape[1]),
                  per_b(n_new * DA_HEADS), per_b(n_new * DA_HEADS)]
                 + [page_spec(p) for p in range(n_pages)] * 2,
        out_specs=per_b(n_new * DA_HEADS),
        scratch_shapes=[pltpu.VMEM((n_pages * page_rows, HEAD_LANES), BF16)] * 2,
    )
    return pl.pallas_call(
        functools.partial(_decode_attn_kernel, n_pages=n_pages, n_new=n_new),
        grid_spec=grid_spec,
        out_shape=jax.ShapeDtypeStruct((nb, n_new * DA_HEADS, HEAD_LANES), F32),
        compiler_params=_params("arbitrary"),
        name="decode_attn",
    )(page_table.reshape(-1), lam, q_rows, k_new, v_new,
      *([cache_k] * n_pages), *([cache_v] * n_pages))


def _merge_kernel(ao_ref, gmo_ref, x_ref, sg_ref, wo_ref, g2_ref, wpq_ref, kbig_ref,
                  h_ref, tn_ref, s_ref, *, attn_scale):
    parts = []
    for b in range(DA_HEADS):
        sl = slice(b * HEAD_LANES, (b + 1) * HEAD_LANES)
        parts.append((_rms(ao_ref[:, sl], sg_ref[...]) * attn_scale).astype(BF16))
    cat = jnp.concatenate(parts + [gmo_ref[...]], axis=1)
    h = x_ref[...] + jnp.dot(cat, wo_ref[...], preferred_element_type=F32)
    h_ref[...] = h
    tn = _rms(h, g2_ref[...]).astype(BF16)
    tn_ref[...] = tn
    qp = jnp.dot(tn, wpq_ref[...], preferred_element_type=F32).astype(BF16)
    width = PEER_HEADS * PEER_HALF
    for c in range(2):
        s_ref[c] = lax.dot_general(kbig_ref[c], qp[:, c * width:(c + 1) * width],
                                   (((1,), (1,)), ((), ())), preferred_element_type=F32)


def _merge(ao, gmo, x, subln_g, w_o, g2, wpq, kbig, attn_scale):
    rows = x.shape[0]
    tile = min(ROW_TILE, rows)
    width = PEER_HEADS * PEER_HALF
    row_spec = lambda w: pl.BlockSpec((tile, w), lambda i: (i, 0))
    return pl.pallas_call(
        functools.partial(_merge_kernel, attn_scale=attn_scale),
        grid=(rows // tile,),
        in_specs=[row_spec(DA_WIDTH), row_spec(GM_WIDTH), row_spec(D_MODEL), _const_spec((1, HEAD_LANES)),
                  _const_spec((D_MODEL, D_MODEL)), _const_spec((1, D_MODEL)),
                  _const_spec((D_MODEL, 2 * width)), _const_spec((2, PEER_N_KEYS * PEER_HEADS, width))],
        out_specs=[row_spec(D_MODEL), row_spec(D_MODEL),
                   pl.BlockSpec((2, PEER_N_KEYS * PEER_HEADS, tile), lambda i: (0, 0, i))],
        out_shape=[jax.ShapeDtypeStruct((rows, D_MODEL), F32),
                   jax.ShapeDtypeStruct((rows, D_MODEL), BF16),
                   jax.ShapeDtypeStruct((2, PEER_N_KEYS * PEER_HEADS, rows), F32)],
        compiler_params=_params("parallel"),
        name="merge",
    )(ao, gmo, x, subln_g, w_o, g2, wpq, kbig)


def _young_candidates():
    return [(p, q) for p in range(PEER_TOPK) for q in range(PEER_TOPK) if (p + 1) * (q + 1) <= PEER_TOPK]


def _topk_kernel(s_ref, r2_ref, e2_ref, n_ref, c_ref, work_ref, top_ref, idx_ref, young_ref,
                 rank_ref, exp_ref):
    nk = PEER_N_KEYS
    tile = s_ref.shape[2:]
    one, zero = jnp.ones(tile, F32), jnp.zeros(tile, F32)

    def extract_distinct(c):
        def round_(r, bound):
            tree = [jnp.where(s_ref[c, j] < bound, s_ref[c, j], NEG_INF) for j in range(nk)]
            while len(tree) > 1:
                tree = [jnp.maximum(tree[t], tree[t + 1]) for t in range(0, len(tree), 2)]
            top_ref[c, r] = tree[0]
            return tree[0]

        last = lax.fori_loop(0, PEER_TOPK, round_, jnp.full(tile, float("inf"), F32))
        reached = zero
        for j in range(nk):
            reached = reached + jnp.where(s_ref[c, j] >= last, one, zero)
        return reached

    def extract(c):
        for j in range(nk):
            work_ref[j] = s_ref[c, j]

        def round_(r, _):
            vals = [work_ref[j] for j in range(nk)]
            idxs = [None] * nk
            width = 1
            while len(vals) > 1:
                nv, ni = [], []
                for t in range(0, len(vals), 2):
                    a, b = vals[t], vals[t + 1]
                    take_b = b > a
                    ia = idxs[t] if idxs[t] is not None else float(t * width)
                    ib = idxs[t + 1] if idxs[t + 1] is not None else float((t + 1) * width)
                    nv.append(jnp.maximum(a, b))
                    ni.append(jnp.where(take_b, ib, ia))
                vals, idxs = nv, ni
                width *= 2
            best, best_idx = vals[0], idxs[0]
            top_ref[c, r] = best
            idx_ref[c, r] = best_idx
            for j in range(nk):
                work_ref[j] = jnp.where(best_idx == float(j), NEG_INF, work_ref[j])
            return 0

        lax.fori_loop(0, PEER_TOPK, round_, 0)

    miscount = jnp.maximum(jnp.abs(extract_distinct(0) - float(PEER_TOPK)),
                           jnp.abs(extract_distinct(1) - float(PEER_TOPK)))
    tied = jnp.max(miscount) > 0.0

    @pl.when(tied)
    def _():
        extract(0)
        extract(1)

    a = [top_ref[0, p] for p in range(PEER_TOPK)]
    b = [top_ref[1, q] for q in range(PEER_TOPK)]
    cands = _young_candidates()
    cand = {pq: a[pq[0]] + b[pq[1]] for pq in cands}
    n_of_p = [zero] * PEER_TOPK
    ea = [jnp.exp(a[p] - a[0]) for p in range(PEER_TOPK)]
    eb = [jnp.exp(b[q] - b[0]) for q in range(PEER_TOPK)]
    z = zero
    for (p, q) in cands:
        fixed = 0
        beaten = zero
        for (p2, q2) in cands:
            if (p2, q2) == (p, q):
                continue
            if p2 <= p and q2 <= q:
                fixed += 1
            elif p2 >= p and q2 >= q:
                continue
            elif p2 * PEER_TOPK + q2 < p * PEER_TOPK + q:
                beaten = beaten + jnp.where(cand[(p2, q2)] >= cand[(p, q)], one, zero)
            else:
                beaten = beaten + jnp.where(cand[(p2, q2)] > cand[(p, q)], one, zero)
        sel = jnp.where(beaten + float(fixed) < float(PEER_TOPK), one, zero)
        n_of_p[p] = n_of_p[p] + sel
        z = z + sel * (ea[p] * eb[q])
    inv_z = 1.0 / z
    for p in range(PEER_TOPK):
        young_ref[p] = n_of_p[p]

    def scatter(hit_a, hit_b):
        for j in range(nk):
            n_j = zero
            r_j = jnp.full(tile, float(nk - 1), F32)
            for p in range(PEER_TOPK):
                n_j = jnp.where(hit_a(p, j), young_ref[p], n_j)
                r_j = jnp.where(hit_b(p, j), float(p), r_j)
            n_ref[j] = n_j
            rank_ref[j * PEER_HEADS:(j + 1) * PEER_HEADS, :] = r_j

    @pl.when(jnp.logical_not(tied))
    def _():
        scatter(lambda p, j: top_ref[0, p] == s_ref[0, j], lambda p, j: top_ref[1, p] == s_ref[1, j])

    @pl.when(tied)
    def _():
        scatter(lambda p, j: idx_ref[0, p] == float(j), lambda p, j: idx_ref[1, p] == float(j))

    for j in range(nk):
        c_ref[j] = jnp.exp(s_ref[0, j] - a[0]) * inv_z
        exp_ref[j * PEER_HEADS:(j + 1) * PEER_HEADS, :] = jnp.exp(s_ref[1, j] - b[0])
    for h in range(PEER_HEADS):
        r2_ref[h] = rank_ref[pl.ds(h, nk, stride=PEER_HEADS), :].astype(BF16)
        e2_ref[h] = exp_ref[pl.ds(h, nk, stride=PEER_HEADS), :].astype(BF16)


def _topk(scores):
    tokens = scores.shape[-1]
    lanes = TOPK_LANES
    tab2 = jax.ShapeDtypeStruct((PEER_HEADS, PEER_N_KEYS, tokens), BF16)
    tab2_spec = pl.BlockSpec((PEER_HEADS, PEER_N_KEYS, lanes), lambda i: (0, 0, i))
    tab1 = jax.ShapeDtypeStruct((tokens // lanes, PEER_N_KEYS, PEER_HEADS, lanes), F32)
    tab1_spec = pl.BlockSpec((None, PEER_N_KEYS, PEER_HEADS, lanes), lambda i: (i, 0, 0, 0))
    return pl.pallas_call(
        _topk_kernel,
        grid=(tokens // lanes,),
        in_specs=[pl.BlockSpec((2, PEER_N_KEYS, PEER_HEADS, lanes), lambda i: (0, 0, 0, i))],
        out_specs=[tab2_spec] * 2 + [tab1_spec] * 2,
        out_shape=[tab2] * 2 + [tab1] * 2,
        scratch_shapes=[pltpu.VMEM((PEER_N_KEYS, PEER_HEADS, lanes), F32),
                        pltpu.VMEM((2, PEER_TOPK, PEER_HEADS, lanes), F32),
                        pltpu.VMEM((2, PEER_TOPK, PEER_HEADS, lanes), F32),
                        pltpu.VMEM((PEER_TOPK, PEER_HEADS, lanes), F32),
                        pltpu.VMEM((PEER_N_KEYS * PEER_HEADS, lanes), F32),
                        pltpu.VMEM((PEER_N_KEYS * PEER_HEADS, lanes), F32)],
        compiler_params=_params("parallel"),
        name="peer_topk",
    )(scores)


def _peer_kernel(tn_ref, pu_ref, pvt_ref, r2_ref, e2_ref, n_ref, c_ref, acc_ref,
                 at_ref, gate_ref, hid_ref, tab_ref):
    e = pl.program_id(1)
    tt = tn_ref.shape[0]
    nk = PEER_N_KEYS
    keys_per_step = pu_ref.shape[0] // nk
    lane_tiles = [(t, slice(t * HEAD_LANES, (t + 1) * HEAD_LANES)) for t in range(tt // HEAD_LANES)]
    pack = BF16_TILE_ROWS

    def first_matmul_and_gate(slot):
        at_ref[...] = lax.dot_general(pu_ref[...], tn_ref[...], (((1,), (1,)), ((), ())),
                                      preferred_element_type=F32).astype(BF16)
        for ii in range(keys_per_step):
            for t, lanes in lane_tiles:
                gate = jnp.zeros((nk, HEAD_LANES), BF16)
                for h in range(PEER_HEADS):
                    row = ii * PEER_HEADS + h
                    n = jnp.broadcast_to(n_ref[t, row:row + 1, :], (pack, HEAD_LANES)).astype(BF16)
                    c = jnp.broadcast_to(c_ref[t, row:row + 1, :], (pack, HEAD_LANES)).astype(BF16)
                    n = jnp.tile(n, (nk // pack, 1))
                    c = jnp.tile(c, (nk // pack, 1))
                    gate = gate + jnp.where(tab_ref[h, 0, :nk, lanes] < n, tab_ref[h, 1, :nk, lanes] * c,
                                            jnp.zeros_like(gate))
                gate_ref[ii * nk:(ii + 1) * nk, lanes] = gate
        hid_ref[slot] = gate_ref[:, :tt] * _gelu(at_ref[...])

    def second_matmul(slot):
        acc_ref[...] += jnp.dot(pvt_ref[...], hid_ref[slot], preferred_element_type=F32)

    @pl.when(e == 0)
    def _():
        tab_ref[:, 0, :nk, :tt] = r2_ref[...]
        tab_ref[:, 1, :nk, :tt] = e2_ref[...]
        acc_ref[...] = jnp.zeros_like(acc_ref)
        hid_ref[1] = jnp.zeros(hid_ref.shape[1:], BF16)

    slot = e % 2
    first_matmul_and_gate(slot)
    second_matmul(1 - slot)


def _peer(tn, pu, pvt, r2, e2, ntab, ctab):
    tokens = tn.shape[0]
    tt = min(PEER_TOKEN_TILE, tokens)
    et = PEER_EXPERT_TILE
    n_blocks = PEER_N_EXPERTS // et
    rows_per_step = (et // PEER_N_KEYS) * PEER_HEADS
    whole_tab = pl.BlockSpec((PEER_HEADS, PEER_N_KEYS, tt), lambda t, e: (0, 0, t))
    step_tab = pl.BlockSpec((tt // HEAD_LANES, rows_per_step, HEAD_LANES),
                            lambda t, e: (t, jnp.minimum(e, n_blocks - 1), 0))
    return pl.pallas_call(
        _peer_kernel,
        grid=(tokens // tt, n_blocks + 1),
        in_specs=[pl.BlockSpec((tt, D_MODEL), lambda t, e: (t, 0)),
                  pl.BlockSpec((et, D_MODEL), lambda t, e: (jnp.minimum(e, n_blocks - 1), 0)),
                  pl.BlockSpec((D_MODEL, et), lambda t, e: (0, jnp.maximum(e - 1, 0))),
                  whole_tab, whole_tab, step_tab, step_tab],
        out_specs=pl.BlockSpec((D_MODEL, tt), lambda t, e: (0, t)),
        out_shape=jax.ShapeDtypeStruct((D_MODEL, tokens), F32),
        scratch_shapes=[pltpu.VMEM((et, tt), BF16), pltpu.VMEM((et, tt + HEAD_LANES), BF16),
                        pltpu.VMEM((2, et, tt), BF16),
                        pltpu.VMEM((PEER_HEADS, 2, PEER_N_KEYS + BF16_TILE_ROWS, tt + HEAD_LANES), BF16)],
        compiler_params=_params("parallel", "arbitrary"),
        name="peer_experts",
    )(tn, pu, pvt, r2, e2, ntab, ctab)


def _final_kernel(h_ref, mix_ref, gf_ref, y_ref):
    y_ref[...] = _rms(h_ref[...] + jnp.transpose(mix_ref[...]), gf_ref[...])


def _final(h, mix_t, final_g):
    rows = h.shape[0]
    tile = min(ROW_TILE, rows)
    row_spec = pl.BlockSpec((tile, D_MODEL), lambda i: (i, 0))
    return pl.pallas_call(
        _final_kernel,
        grid=(rows // tile,),
        in_specs=[row_spec, pl.BlockSpec((D_MODEL, tile), lambda i: (0, i)), _const_spec((1, D_MODEL))],
        out_specs=row_spec,
        out_shape=jax.ShapeDtypeStruct((rows, D_MODEL), F32),
        compiler_params=_params("parallel"),
        name="final_norm",
    )(h, mix_t, final_g)


def _rope_tables(pos):
    half = ROPE_DIM // 2
    inv = ROPE_THETA ** (-jnp.arange(half, dtype=F32) * 2.0 / ROPE_DIM)
    d = jnp.arange(HEAD_LANES) % DA_HEAD_DIM
    ang = pos[:, None] * inv[d % half][None, :]
    cos_t = jnp.where(d < ROPE_DIM, jnp.cos(ang), 1.0)
    sin_a = jnp.where(d < half, -jnp.sin(ang), 0.0)
    sin_b = jnp.where((d >= half) & (d < ROPE_DIM), jnp.sin(ang), 0.0)
    return cos_t, sin_a, sin_b


def _row_pipeline(x, rope_tabs, wmix, bias, attend, weights):
    (g1, w_in, gm_g, subln_g, w_o, g2, wpq, kbig, pu, pvt, final_g, attn_scale) = weights
    q1, q2, k32, v32, kb, vb, gmo, gn = _proj(x, g1, w_in, *rope_tabs, gm_g, wmix, bias)
    ao = attend(q1, q2, k32, v32, kb, vb)
    h, tn, scores = _merge(ao, gmo, x, subln_g, w_o, g2, wpq, kbig, attn_scale)
    rows = x.shape[0]
    r2, e2, ntab, ctab = _topk(scores.reshape(2, PEER_N_KEYS, PEER_HEADS, rows))
    flat = lambda t: t.reshape(rows // TOPK_LANES, PEER_N_KEYS * PEER_HEADS, TOPK_LANES)
    y = _final(h, _peer(tn, pu, pvt, r2, e2, flat(ntab), flat(ctab)), final_g)
    return y, k32, v32, gn


def kernel(x_prompt, x_sample, cache_k, cache_v, page_table, norm1_g, w_in, gm_norm_g, gm_ws, gm_b,
           da_lq1, da_lk1, da_lq2, da_lk2, da_subln_g, w_o, norm2_g, peer_wq, peer_keys, peer_u,
           peer_v, final_g):
    batch, seq, _ = x_prompt.shape
    nb, n_new, _ = x_sample.shape
    n_pages = page_table.shape[1]
    past = n_pages * PAGE_SIZE
    assert w_in.shape[0] == 1 and seq % CHUNK == 0 and seq % min(ROW_TILE, batch * seq) == 0

    lam_init = 0.8 - 0.6 * math.exp(-0.3 * 0)
    dots = lambda a, b: jnp.exp(jnp.sum(a.astype(F32) * b.astype(F32)))
    lam = (dots(da_lq1[0], da_lk1[0]) - dots(da_lq2[0], da_lk2[0]) + lam_init).reshape(1).astype(F32)

    eye = jnp.eye(PEER_HEADS, dtype=F32)
    kbig = jnp.einsum("hcjd,hg->cjhgd", peer_keys[0], eye).reshape(
        2, PEER_N_KEYS * PEER_HEADS, PEER_HEADS * PEER_HALF).astype(BF16)
    wpq = peer_wq[0].reshape(D_MODEL, PEER_HEADS, 2, PEER_HALF).transpose(0, 2, 1, 3).reshape(
        D_MODEL, 2 * PEER_HEADS * PEER_HALF).astype(BF16)
    weights = (norm1_g[0][None], w_in[0].astype(BF16), gm_norm_g[0].reshape(1, GM_WIDTH),
               da_subln_g[0][None], w_o[0].astype(BF16), norm2_g[0][None], wpq, kbig,
               peer_u[0].astype(BF16), peer_v[0].T.astype(BF16), final_g[None], 1.0 - lam_init)

    xp = x_prompt.reshape(batch * seq, D_MODEL)
    tabs_p = _rope_tables(jnp.arange(seq, dtype=F32))
    wmix_p = jnp.tril(gm_ws[0]).astype(BF16)
    bias_p = jnp.repeat(jnp.transpose(gm_b[0]), HEAD_LANES, axis=1)
    attend_p = lambda q1, q2, k32, v32, kb, vb: _prompt_attention(lam, q1, q2, kb, vb, batch, seq)
    y_p, k_p, v_p, _ = _row_pipeline(xp, tabs_p, wmix_p, bias_p, attend_p, weights)

    xs = x_sample.reshape(nb * n_new, D_MODEL)
    tabs_s = _rope_tables(jnp.tile(past + jnp.arange(n_new, dtype=F32), nb))
    rows_s = min(ROW_TILE, nb * n_new)
    w_new = jnp.tril(gm_ws[0][:, :n_new, :n_new])
    r = jnp.arange(rows_s)
    pick = (r[:, None] % n_new == jnp.arange(n_new)[None, :]).astype(F32)
    w_rows = jnp.einsum("ri,hij,cj->hrc", pick, w_new, pick, precision=lax.Precision.HIGHEST)
    wmix_s = jnp.where(r[:, None] // n_new == r[None, :] // n_new, w_rows, 0.0).astype(BF16)
    bias_s = jnp.tile(jnp.repeat(jnp.transpose(gm_b[0][:, :n_new]), HEAD_LANES, axis=1), (rows_s // n_new, 1))
    ck = cache_k[0].reshape(-1, PAGE_SIZE * DA_HEADS, HEAD_LANES)
    cv = cache_v[0].reshape(-1, PAGE_SIZE * DA_HEADS, HEAD_LANES)

    def attend_s(q1, q2, k32, v32, kb, vb):
        per_head = lambda t: jnp.transpose(t.reshape(nb, n_new, DA_HEADS, HEAD_LANES), (0, 2, 1, 3))
        q = jnp.stack([per_head(q1), per_head(q2)], axis=1).reshape(nb, 2 * DA_HEADS * n_new, HEAD_LANES)
        new_rows = lambda t: t.reshape(nb, n_new * DA_HEADS, HEAD_LANES).astype(BF16)
        ao = _decode_attention(page_table, lam, q, new_rows(k32), new_rows(v32), ck, cv)
        ao = jnp.transpose(ao.reshape(nb, DA_HEADS, n_new, HEAD_LANES), (0, 2, 1, 3))
        return ao.reshape(nb * n_new, DA_WIDTH)

    y_s, k_s, v_s, gn_s = _row_pipeline(xs, tabs_s, wmix_s, bias_s, attend_s, weights)

    head_shape = lambda b, s: (1, b, s, DA_HEADS, HEAD_LANES)
    return (y_p.reshape(batch, seq, D_MODEL), y_s.reshape(nb, n_new, D_MODEL),
            k_p.reshape(head_shape(batch, seq)), v_p.reshape(head_shape(batch, seq)),
            k_s.reshape(head_shape(nb, n_new)), v_s.reshape(head_shape(nb, n_new)),
            gn_s.reshape(1, nb, n_new, GM_HEADS, HEAD_LANES))
```

```python
import functools
import math

import jax
import jax.numpy as jnp
from jax import lax
from jax.experimental import pallas as pl
from jax.experimental.pallas import tpu as pltpu

F32 = jnp.float32
BF16 = jnp.bfloat16

D_MODEL = 1024
DA_WIDTH = 512
DA_HEADS = 4
HEAD_LANES = 128
BF16_TILE_ROWS = 16
DA_HEAD_DIM = 64
GM_WIDTH = 512
GM_HEADS = 4
CHUNK = 128
IN_WIDTH = 3 * DA_WIDTH + 2 * GM_WIDTH
ROPE_THETA = 500000.0
ROPE_DIM = 16
PAGE_SIZE = 128
PEER_HEADS = 8
PEER_N_KEYS = 128
PEER_N_EXPERTS = PEER_N_KEYS * PEER_N_KEYS
PEER_HALF = 128
PEER_TOPK = 16
NORM_EPS = 1e-6
NEG_INF = float("-inf")

VMEM_LIMIT_BYTES = 56 * 1024 * 1024

ROW_TILE = 512
ATTN_Q_TILE = 256
TOPK_LANES = 128
PEER_EXPERT_TILE = 1024
PEER_TOKEN_TILE = 1024


def _params(*semantics):
    return pltpu.CompilerParams(dimension_semantics=semantics, vmem_limit_bytes=VMEM_LIMIT_BYTES)


def _rms(x, g):
    return x * lax.rsqrt(jnp.mean(x * x, axis=-1, keepdims=True) + NORM_EPS) * g


def _gelu(x):
    return x * (0.5 * (1.0 + jnp.tanh(math.sqrt(2.0 / math.pi) * (x + 0.044715 * (x * x * x)))))


def _const_spec(shape):
    zeros = (0,) * len(shape)
    return pl.BlockSpec(shape, lambda *_: zeros)


def _proj_kernel(x_ref, g1_ref, win_ref, cos_ref, sa_ref, sb_ref, gmg_ref, wmix_ref, bias_ref,
                 q1_ref, q2_ref, k32_ref, v32_ref, kb_ref, vb_ref, gmo_ref, gn_ref, *, mix_rows):
    rows = x_ref.shape[0]
    xn = _rms(x_ref[...], g1_ref[...]).astype(BF16)
    z = jnp.dot(xn, win_ref[...], preferred_element_type=F32)
    cos_t, sin_a, sin_b = cos_ref[...], sa_ref[...], sb_ref[...]
    first_half = lax.broadcasted_iota(jnp.int32, (rows, HEAD_LANES), 1) < DA_HEAD_DIM

    def rope(t):
        return t * cos_t + pltpu.roll(t, HEAD_LANES - 8, 1) * sin_a + pltpu.roll(t, 8, 1) * sin_b

    for b in range(DA_HEADS):
        sl = slice(b * HEAD_LANES, (b + 1) * HEAD_LANES)
        q = rope(z[:, sl]) * (1.0 / math.sqrt(DA_HEAD_DIM))
        q1_ref[:, sl] = jnp.where(first_half, q, 0.0).astype(BF16)
        q2_ref[:, sl] = jnp.where(first_half, 0.0, q).astype(BF16)
        k = rope(z[:, DA_WIDTH + b * HEAD_LANES:DA_WIDTH + (b + 1) * HEAD_LANES])
        v = z[:, 2 * DA_WIDTH + b * HEAD_LANES:2 * DA_WIDTH + (b + 1) * HEAD_LANES]
        k32_ref[pl.ds(b, rows, stride=DA_HEADS), :] = k
        v32_ref[pl.ds(b, rows, stride=DA_HEADS), :] = v
        kb_ref[:, sl] = k.astype(BF16)
        vb_ref[:, sl] = v.astype(BF16)

    u = _gelu(z[:, 3 * DA_WIDTH:3 * DA_WIDTH + GM_WIDTH])
    g = _gelu(z[:, 3 * DA_WIDTH + GM_WIDTH:])
    for b in range(GM_HEADS):
        sl = slice(b * HEAD_LANES, (b + 1) * HEAD_LANES)
        gn = _rms(g[:, sl], gmg_ref[:, sl])
        gn_ref[:, sl] = gn
        gnb = gn.astype(BF16)
        for r in range(rows // mix_rows):
            rs = slice(r * mix_rows, (r + 1) * mix_rows)
            s = jnp.dot(wmix_ref[b], gnb[rs], preferred_element_type=F32) + bias_ref[:, sl]
            gmo_ref[rs, sl] = (u[rs, sl] * s).astype(BF16)


def _proj(x, g1, w_in, cos_t, sin_a, sin_b, gm_g, wmix, bias):
    rows = x.shape[0]
    tile = min(ROW_TILE, rows)
    mix_rows = wmix.shape[-1]
    row_spec = lambda w: pl.BlockSpec((tile, w), lambda i: (i, 0))
    table_blocks = cos_t.shape[0] // tile
    table_spec = pl.BlockSpec((tile, HEAD_LANES), lambda i: (i % table_blocks, 0))
    head_rows_spec = pl.BlockSpec((tile * DA_HEADS, HEAD_LANES), lambda i: (i, 0))
    out_shape = [jax.ShapeDtypeStruct((rows, DA_WIDTH), BF16),
                 jax.ShapeDtypeStruct((rows, DA_WIDTH), BF16),
                 jax.ShapeDtypeStruct((rows * DA_HEADS, HEAD_LANES), F32),
                 jax.ShapeDtypeStruct((rows * DA_HEADS, HEAD_LANES), F32),
                 jax.ShapeDtypeStruct((rows, DA_WIDTH), BF16),
                 jax.ShapeDtypeStruct((rows, DA_WIDTH), BF16),
                 jax.ShapeDtypeStruct((rows, GM_WIDTH), BF16),
                 jax.ShapeDtypeStruct((rows, GM_WIDTH), F32)]
    return pl.pallas_call(
        functools.partial(_proj_kernel, mix_rows=mix_rows),
        grid=(rows // tile,),
        in_specs=[row_spec(D_MODEL), _const_spec((1, D_MODEL)), _const_spec((D_MODEL, IN_WIDTH)),
                  table_spec, table_spec, table_spec,
                  _const_spec((1, GM_WIDTH)), _const_spec(wmix.shape), _const_spec(bias.shape)],
        out_specs=[row_spec(DA_WIDTH)] * 2 + [head_rows_spec] * 2 + [row_spec(DA_WIDTH)] * 2
                  + [row_spec(GM_WIDTH)] * 2,
        out_shape=out_shape,
        compiler_params=_params("parallel"),
        name="proj",
    )(x, g1, w_in, cos_t, sin_a, sin_b, gm_g, wmix, bias)


def _attn_kernel(lam_ref, q1_ref, q2_ref, k_ref, v_ref, o_ref):
    tq = q1_ref.shape[0]
    i = pl.program_id(2)
    q = jnp.concatenate([q1_ref[...], q2_ref[...]], axis=0)

    def step(j, carry, masked):
        m, l, acc = carry
        start = pl.multiple_of(j * tq, tq)
        kb = k_ref[pl.ds(start, tq), :]
        vb = v_ref[pl.ds(start, tq), :]
        s = lax.dot_general(q, kb, (((1,), (1,)), ((), ())), preferred_element_type=F32)
        if masked:
            row = lax.broadcasted_iota(jnp.int32, (2 * tq, tq), 0)
            col = lax.broadcasted_iota(jnp.int32, (2 * tq, tq), 1)
            row = jnp.where(row >= tq, row - tq, row)
            s = jnp.where(col <= row, s, NEG_INF)
        m_new = jnp.maximum(m, jnp.max(s, axis=-1, keepdims=True))
        alpha = jnp.exp(m - m_new)
        p = jnp.exp(s - m_new)
        l = alpha * l + jnp.sum(p, axis=-1, keepdims=True)
        acc = alpha * acc + jnp.dot(p.astype(BF16), vb, preferred_element_type=F32)
        return m_new, l, acc

    init = (jnp.full((2 * tq, 1), NEG_INF, F32), jnp.zeros((2 * tq, 1), F32),
            jnp.zeros((2 * tq, HEAD_LANES), F32))
    carry = lax.fori_loop(0, i, lambda j, c: step(j, c, False), init)
    _, l, acc = step(i, carry, True)
    o = acc / l
    o_ref[...] = o[:tq] - lam_ref[0] * o[tq:]


def _prompt_attention(lam, q1, q2, kb, vb, batch, seq):
    tq = min(ATTN_Q_TILE, seq)
    nq = seq // tq
    q_spec = pl.BlockSpec((tq, HEAD_LANES), lambda b, h, i: (b * nq + i, h))
    kv_spec = pl.BlockSpec((seq, HEAD_LANES), lambda b, h, i: (b, h))
    return pl.pallas_call(
        _attn_kernel,
        grid=(batch, DA_HEADS, nq),
        in_specs=[pl.BlockSpec(memory_space=pltpu.SMEM), q_spec, q_spec, kv_spec, kv_spec],
        out_specs=q_spec,
        out_shape=jax.ShapeDtypeStruct((batch * seq, DA_WIDTH), F32),
        compiler_params=_params("parallel", "parallel", "arbitrary"),
        name="prompt_attn",
    )(lam, q1, q2, kb, vb)


def _decode_attn_kernel(pt_ref, lam_ref, q_ref, kn_ref, vn_ref, *rest, n_pages, n_new):
    del pt_ref
    k_pages, v_pages = rest[:n_pages], rest[n_pages:2 * n_pages]
    o_ref, k_all, v_all = rest[2 * n_pages:]
    page_rows = PAGE_SIZE * DA_HEADS
    for p in range(n_pages):
        rs = slice(p * page_rows, (p + 1) * page_rows)
        k_all[rs, :] = k_pages[p][0].astype(BF16)
        v_all[rs, :] = v_pages[p][0].astype(BF16)
    q = q_ref[0]
    nrow = q.shape[0]
    contract_last = (((1,), (1,)), ((), ()))
    s_past = lax.dot_general(q, k_all[...], contract_last, preferred_element_type=F32)
    s_new = lax.dot_general(q, kn_ref[0], contract_last, preferred_element_type=F32)

    def row_col(shape):
        row = lax.broadcasted_iota(jnp.int32, shape, 0)
        col = lax.broadcasted_iota(jnp.int32, shape, 1)
        return (row // n_new) % DA_HEADS, row % n_new, col % DA_HEADS, col // DA_HEADS

    q_head, _, k_head, _ = row_col(s_past.shape)
    s_past = jnp.where(q_head == k_head, s_past, NEG_INF)
    q_head, q_tok, k_head, k_tok = row_col(s_new.shape)
    s_new = jnp.where(q_head == k_head, jnp.where(k_tok <= q_tok, s_new, NEG_INF), NEG_INF)
    m = jnp.maximum(jnp.max(s_past, axis=-1, keepdims=True), jnp.max(s_new, axis=-1, keepdims=True))
    p_past = jnp.exp(s_past - m)
    p_new = jnp.exp(s_new - m)
    inv = 1.0 / (jnp.sum(p_past, axis=-1, keepdims=True) + jnp.sum(p_new, axis=-1, keepdims=True))
    half = nrow // 2
    lam = lam_ref[0]
    a_past = (p_past[:half] * inv[:half] - lam * (p_past[half:] * inv[half:])).astype(BF16)
    a_new = (p_new[:half] * inv[:half] - lam * (p_new[half:] * inv[half:])).astype(BF16)
    o_ref[0] = (jnp.dot(a_past, v_all[...], preferred_element_type=F32)
                + jnp.dot(a_new, vn_ref[0], preferred_element_type=F32))


def _decode_attention(page_table, lam, q_rows, k_new, v_new, cache_k, cache_v):
    nb, n_pages = page_table.shape
    n_new = k_new.shape[1] // DA_HEADS
    page_rows = PAGE_SIZE * DA_HEADS
    page_spec = lambda p: pl.BlockSpec((1, page_rows, HEAD_LANES), lambda b, pt: (pt[b * n_pages + p], 0, 0))
    per_b = lambda r: pl.BlockSpec((1, r, HEAD_LANES), lambda b, pt: (b, 0, 0))
    grid_spec = pltpu.PrefetchScalarGridSpec(
        num_scalar_prefetch=1,
        grid=(nb,),
        in_specs=[pl.BlockSpec(memory_space=pltpu.SMEM), per_b(q_rows.shape[1]),
                  per_b(n_new * DA_HEADS), per_b(n_new * DA_HEADS)]
                 + [page_spec(p) for p in range(n_pages)] * 2,
        out_specs=per_b(n_new * DA_HEADS),
        scratch_shapes=[pltpu.VMEM((n_pages * page_rows, HEAD_LANES), BF16)] * 2,
    )
    return pl.pallas_call(
        functools.partial(_decode_attn_kernel, n_pages=n_pages, n_new=n_new),
        grid_spec=grid_spec,
        out_shape=jax.ShapeDtypeStruct((nb, n_new * DA_HEADS, HEAD_LANES), F32),
        compiler_params=_params("arbitrary"),
        name="decode_attn",
    )(page_table.reshape(-1), lam, q_rows, k_new, v_new,
      *([cache_k] * n_pages), *([cache_v] * n_pages))


def _merge_kernel(ao_ref, gmo_ref, x_ref, sg_ref, wo_ref, g2_ref, wpq_ref, kbig_ref,
                  h_ref, tn_ref, s_ref, *, attn_scale):
    parts = []
    for b in range(DA_HEADS):
        sl = slice(b * HEAD_LANES, (b + 1) * HEAD_LANES)
        parts.append((_rms(ao_ref[:, sl], sg_ref[...]) * attn_scale).astype(BF16))
    cat = jnp.concatenate(parts + [gmo_ref[...]], axis=1)
    h = x_ref[...] + jnp.dot(cat, wo_ref[...], preferred_element_type=F32)
    h_ref[...] = h
    tn = _rms(h, g2_ref[...]).astype(BF16)
    tn_ref[...] = tn
    qp = jnp.dot(tn, wpq_ref[...], preferred_element_type=F32).astype(BF16)
    width = PEER_HEADS * PEER_HALF
    for c in range(2):
        s_ref[c] = lax.dot_general(kbig_ref[c], qp[:, c * width:(c + 1) * width],
                                   (((1,), (1,)), ((), ())), preferred_element_type=F32)


def _merge(ao, gmo, x, subln_g, w_o, g2, wpq, kbig, attn_scale):
    rows = x.shape[0]
    tile = min(ROW_TILE, rows)
    width = PEER_HEADS * PEER_HALF
    row_spec = lambda w: pl.BlockSpec((tile, w), lambda i: (i, 0))
    return pl.pallas_call(
        functools.partial(_merge_kernel, attn_scale=attn_scale),
        grid=(rows // tile,),
        in_specs=[row_spec(DA_WIDTH), row_spec(GM_WIDTH), row_spec(D_MODEL), _const_spec((1, HEAD_LANES)),
                  _const_spec((D_MODEL, D_MODEL)), _const_spec((1, D_MODEL)),
                  _const_spec((D_MODEL, 2 * width)), _const_spec((2, PEER_N_KEYS * PEER_HEADS, width))],
        out_specs=[row_spec(D_MODEL), row_spec(D_MODEL),
                   pl.BlockSpec((2, PEER_N_KEYS * PEER_HEADS, tile), lambda i: (0, 0, i))],
        out_shape=[jax.ShapeDtypeStruct((rows, D_MODEL), F32),
                   jax.ShapeDtypeStruct((rows, D_MODEL), BF16),
                   jax.ShapeDtypeStruct((2, PEER_N_KEYS * PEER_HEADS, rows), F32)],
        compiler_params=_params("parallel"),
        name="merge",
    )(ao, gmo, x, subln_g, w_o, g2, wpq, kbig)


def _young_candidates():
    return [(p, q) for p in range(PEER_TOPK) for q in range(PEER_TOPK) if (p + 1) * (q + 1) <= PEER_TOPK]


def _topk_kernel(s_ref, r2_ref, e2_ref, n_ref, c_ref, work_ref, top_ref, idx_ref, young_ref,
                 rank_ref, exp_ref):
    nk = PEER_N_KEYS
    tile = s_ref.shape[2:]
    one, zero = jnp.ones(tile, F32), jnp.zeros(tile, F32)

    def extract_distinct(c):
        def round_(r, bound):
            tree = [jnp.where(s_ref[c, j] < bound, s_ref[c, j], NEG_INF) for j in range(nk)]
            while len(tree) > 1:
                tree = [jnp.maximum(tree[t], tree[t + 1]) for t in range(0, len(tree), 2)]
            top_ref[c, r] = tree[0]
            return tree[0]

        last = lax.fori_loop(0, PEER_TOPK, round_, jnp.full(tile, float("inf"), F32))
        reached = zero
        for j in range(nk):
            reached = reached + jnp.where(s_ref[c, j] >= last, one, zero)
        return reached

    def extract(c):
        for j in range(nk):
            work_ref[j] = s_ref[c, j]

        def round_(r, _):
            vals = [work_ref[j] for j in range(nk)]
            idxs = [None] * nk
            width = 1
            while len(vals) > 1:
                nv, ni = [], []
                for t in range(0, len(vals), 2):
                    a, b = vals[t], vals[t + 1]
                    take_b = b > a
                    ia = idxs[t] if idxs[t] is not None else float(t * width)
                    ib = idxs[t + 1] if idxs[t + 1] is not None else float((t + 1) * width)
                    nv.append(jnp.maximum(a, b))
                    ni.append(jnp.where(take_b, ib, ia))
                vals, idxs = nv, ni
                width *= 2
            best, best_idx = vals[0], idxs[0]
            top_ref[c, r] = best
            idx_ref[c, r] = best_idx
            for j in range(nk):
                work_ref[j] = jnp.where(best_idx == float(j), NEG_INF, work_ref[j])
            return 0

        lax.fori_loop(0, PEER_TOPK, round_, 0)

    miscount = jnp.maximum(jnp.abs(extract_distinct(0) - float(PEER_TOPK)),
                           jnp.abs(extract_distinct(1) - float(PEER_TOPK)))
    tied = jnp.max(miscount) > 0.0

    @pl.when(tied)
    def _():
        extract(0)
        extract(1)

    a = [top_ref[0, p] for p in range(PEER_TOPK)]
    b = [top_ref[1, q] for q in range(PEER_TOPK)]
    cands = _young_candidates()
    cand = {pq: a[pq[0]] + b[pq[1]] for pq in cands}
    n_of_p = [zero] * PEER_TOPK
    ea = [jnp.exp(a[p] - a[0]) for p in range(PEER_TOPK)]
    eb = [jnp.exp(b[q] - b[0]) for q in range(PEER_TOPK)]
    z = zero
    for (p, q) in cands:
        fixed = 0
        beaten = zero
        for (p2, q2) in cands:
            if (p2, q2) == (p, q):
                continue
            if p2 <= p and q2 <= q:
                fixed += 1
            elif p2 >= p and q2 >= q:
                continue
            elif p2 * PEER_TOPK + q2 < p * PEER_TOPK + q:
                beaten = beaten + jnp.where(cand[(p2, q2)] >= cand[(p, q)], one, zero)
            else:
                beaten = beaten + jnp.where(cand[(p2, q2)] > cand[(p, q)], one, zero)
        sel = jnp.where(beaten + float(fixed) < float(PEER_TOPK), one, zero)
        n_of_p[p] = n_of_p[p] + sel
        z = z + sel * (ea[p] * eb[q])
    inv_z = 1.0 / z
    for p in range(PEER_TOPK):
        young_ref[p] = n_of_p[p]

    def scatter(hit_a, hit_b):
        for j in range(nk):
            n_j = zero
            r_j = jnp.full(tile, float(nk - 1), F32)
            for p in range(PEER_TOPK):
                n_j = jnp.where(hit_a(p, j), young_ref[p], n_j)
                r_j = jnp.where(hit_b(p, j), float(p), r_j)
            n_ref[j] = n_j
            rank_ref[j * PEER_HEADS:(j + 1) * PEER_HEADS, :] = r_j

    @pl.when(jnp.logical_not(tied))
    def _():
        scatter(lambda p, j: top_ref[0, p] == s_ref[0, j], lambda p, j: top_ref[1, p] == s_ref[1, j])

    @pl.when(tied)
    def _():
        scatter(lambda p, j: idx_ref[0, p] == float(j), lambda p, j: idx_ref[1, p] == float(j))

    for j in range(nk):
        c_ref[j] = jnp.exp(s_ref[0, j] - a[0]) * inv_z
        exp_ref[j * PEER_HEADS:(j + 1) * PEER_HEADS, :] = jnp.exp(s_ref[1, j] - b[0])
    for h in range(PEER_HEADS):
        r2_ref[h] = rank_ref[pl.ds(h, nk, stride=PEER_HEADS), :].astype(BF16)
        e2_ref[h] = exp_ref[pl.ds(h, nk, stride=PEER_HEADS), :].astype(BF16)


def _topk(scores):
    tokens = scores.shape[-1]
    lanes = TOPK_LANES
    tab2 = jax.ShapeDtypeStruct((PEER_HEADS, PEER_N_KEYS, tokens), BF16)
    tab2_spec = pl.BlockSpec((PEER_HEADS, PEER_N_KEYS, lanes), lambda i: (0, 0, i))
    tab1 = jax.ShapeDtypeStruct((tokens // lanes, PEER_N_KEYS, PEER_HEADS, lanes), F32)
    tab1_spec = pl.BlockSpec((None, PEER_N_KEYS, PEER_HEADS, lanes), lambda i: (i, 0, 0, 0))
    return pl.pallas_call(
        _topk_kernel,
        grid=(tokens // lanes,),
        in_specs=[pl.BlockSpec((2, PEER_N_KEYS, PEER_HEADS, lanes), lambda i: (0, 0, 0, i))],
        out_specs=[tab2_spec] * 2 + [tab1_spec] * 2,
        out_shape=[tab2] * 2 + [tab1] * 2,
        scratch_shapes=[pltpu.VMEM((PEER_N_KEYS, PEER_HEADS, lanes), F32),
                        pltpu.VMEM((2, PEER_TOPK, PEER_HEADS, lanes), F32),
                        pltpu.VMEM((2, PEER_TOPK, PEER_HEADS, lanes), F32),
                        pltpu.VMEM((PEER_TOPK, PEER_HEADS, lanes), F32),
                        pltpu.VMEM((PEER_N_KEYS * PEER_HEADS, lanes), F32),
                        pltpu.VMEM((PEER_N_KEYS * PEER_HEADS, lanes), F32)],
        compiler_params=_params("parallel"),
        name="peer_topk",
    )(scores)


def _peer_kernel(tn_ref, pu_ref, pvt_ref, r2_ref, e2_ref, n_ref, c_ref, acc_ref,
                 at_ref, gate_ref, hid_ref, tab_ref, tnt_ref):
    e = pl.program_id(1)
    tt = tn_ref.shape[0]
    nk = PEER_N_KEYS
    keys_per_step = pu_ref.shape[0] // nk
    lane_tiles = [(t, slice(t * HEAD_LANES, (t + 1) * HEAD_LANES)) for t in range(tt // HEAD_LANES)]
    pack = BF16_TILE_ROWS

    def first_matmul_and_gate(slot):
        at_ref[...] = jnp.dot(pu_ref[...], tnt_ref[...], preferred_element_type=F32).astype(BF16)
        for ii in range(keys_per_step):
            for t, lanes in lane_tiles:
                gate = jnp.zeros((nk, HEAD_LANES), BF16)
                for h in range(PEER_HEADS):
                    row = ii * PEER_HEADS + h
                    n = jnp.broadcast_to(n_ref[t, row:row + 1, :], (pack, HEAD_LANES)).astype(BF16)
                    c = jnp.broadcast_to(c_ref[t, row:row + 1, :], (pack, HEAD_LANES)).astype(BF16)
                    n = jnp.tile(n, (nk // pack, 1))
                    c = jnp.tile(c, (nk // pack, 1))
                    gate = gate + jnp.where(tab_ref[h, 0, :nk, lanes] < n, tab_ref[h, 1, :nk, lanes] * c,
                                            jnp.zeros_like(gate))
                gate_ref[ii * nk:(ii + 1) * nk, lanes] = gate
        hid_ref[slot] = gate_ref[:, :tt] * _gelu(at_ref[...])

    def second_matmul(slot):
        acc_ref[...] += jnp.dot(pvt_ref[...], hid_ref[slot], preferred_element_type=F32)

    @pl.when(e == 0)
    def _():
        tab_ref[:, 0, :nk, :tt] = r2_ref[...]
        tab_ref[:, 1, :nk, :tt] = e2_ref[...]
        tnt_ref[...] = jnp.transpose(tn_ref[...].astype(F32)).astype(BF16)
        acc_ref[...] = jnp.zeros_like(acc_ref)
        hid_ref[1] = jnp.zeros(hid_ref.shape[1:], BF16)

    slot = e % 2
    first_matmul_and_gate(slot)
    second_matmul(1 - slot)


def _peer(tn, pu, pvt, r2, e2, ntab, ctab):
    tokens = tn.shape[0]
    tt = min(PEER_TOKEN_TILE, tokens)
    et = PEER_EXPERT_TILE
    n_blocks = PEER_N_EXPERTS // et
    rows_per_step = (et // PEER_N_KEYS) * PEER_HEADS
    whole_tab = pl.BlockSpec((PEER_HEADS, PEER_N_KEYS, tt), lambda t, e: (0, 0, t))
    step_tab = pl.BlockSpec((tt // HEAD_LANES, rows_per_step, HEAD_LANES),
                            lambda t, e: (t, jnp.minimum(e, n_blocks - 1), 0))
    return pl.pallas_call(
        _peer_kernel,
        grid=(tokens // tt, n_blocks + 1),
        in_specs=[pl.BlockSpec((tt, D_MODEL), lambda t, e: (t, 0)),
                  pl.BlockSpec((et, D_MODEL), lambda t, e: (jnp.minimum(e, n_blocks - 1), 0)),
                  pl.BlockSpec((D_MODEL, et), lambda t, e: (0, jnp.maximum(e - 1, 0))),
                  whole_tab, whole_tab, step_tab, step_tab],
        out_specs=pl.BlockSpec((D_MODEL, tt), lambda t, e: (0, t)),
        out_shape=jax.ShapeDtypeStruct((D_MODEL, tokens), F32),
        scratch_shapes=[pltpu.VMEM((et, tt), BF16), pltpu.VMEM((et, tt + HEAD_LANES), BF16),
                        pltpu.VMEM((2, et, tt), BF16),
                        pltpu.VMEM((PEER_HEADS, 2, PEER_N_KEYS + BF16_TILE_ROWS, tt + HEAD_LANES), BF16),
                        pltpu.VMEM((D_MODEL, tt), BF16)],
        compiler_params=_params("parallel", "arbitrary"),
        name="peer_experts",
    )(tn, pu, pvt, r2, e2, ntab, ctab)


def _final_kernel(h_ref, mix_ref, gf_ref, y_ref):
    y_ref[...] = _rms(h_ref[...] + jnp.transpose(mix_ref[...]), gf_ref[...])


def _final(h, mix_t, final_g):
    rows = h.shape[0]
    tile = min(ROW_TILE, rows)
    row_spec = pl.BlockSpec((tile, D_MODEL), lambda i: (i, 0))
    return pl.pallas_call(
        _final_kernel,
        grid=(rows // tile,),
        in_specs=[row_spec, pl.BlockSpec((D_MODEL, tile), lambda i: (0, i)), _const_spec((1, D_MODEL))],
        out_specs=row_spec,
        out_shape=jax.ShapeDtypeStruct((rows, D_MODEL), F32),
        compiler_params=_params("parallel"),
        name="final_norm",
    )(h, mix_t, final_g)


def _rope_tables(pos):
    half = ROPE_DIM // 2
    inv = ROPE_THETA ** (-jnp.arange(half, dtype=F32) * 2.0 / ROPE_DIM)
    d = jnp.arange(HEAD_LANES) % DA_HEAD_DIM
    ang = pos[:, None] * inv[d % half][None, :]
    cos_t = jnp.where(d < ROPE_DIM, jnp.cos(ang), 1.0)
    sin_a = jnp.where(d < half, -jnp.sin(ang), 0.0)
    sin_b = jnp.where((d >= half) & (d < ROPE_DIM), jnp.sin(ang), 0.0)
    return cos_t, sin_a, sin_b


def _row_pipeline(x, rope_tabs, wmix, bias, attend, weights):
    (g1, w_in, gm_g, subln_g, w_o, g2, wpq, kbig, pu, pvt, final_g, attn_scale) = weights
    q1, q2, k32, v32, kb, vb, gmo, gn = _proj(x, g1, w_in, *rope_tabs, gm_g, wmix, bias)
    ao = attend(q1, q2, k32, v32, kb, vb)
    h, tn, scores = _merge(ao, gmo, x, subln_g, w_o, g2, wpq, kbig, attn_scale)
    rows = x.shape[0]
    r2, e2, ntab, ctab = _topk(scores.reshape(2, PEER_N_KEYS, PEER_HEADS, rows))
    flat = lambda t: t.reshape(rows // TOPK_LANES, PEER_N_KEYS * PEER_HEADS, TOPK_LANES)
    y = _final(h, _peer(tn, pu, pvt, r2, e2, flat(ntab), flat(ctab)), final_g)
    return y, k32, v32, gn


def kernel(x_prompt, x_sample, cache_k, cache_v, page_table, norm1_g, w_in, gm_norm_g, gm_ws, gm_b,
           da_lq1, da_lk1, da_lq2, da_lk2, da_subln_g, w_o, norm2_g, peer_wq, peer_keys, peer_u,
           peer_v, final_g):
    batch, seq, _ = x_prompt.shape
    nb, n_new, _ = x_sample.shape
    n_pages = page_table.shape[1]
    past = n_pages * PAGE_SIZE
    assert w_in.shape[0] == 1 and seq % CHUNK == 0 and seq % min(ROW_TILE, batch * seq) == 0

    lam_init = 0.8 - 0.6 * math.exp(-0.3 * 0)
    dots = lambda a, b: jnp.exp(jnp.sum(a.astype(F32) * b.astype(F32)))
    lam = (dots(da_lq1[0], da_lk1[0]) - dots(da_lq2[0], da_lk2[0]) + lam_init).reshape(1).astype(F32)

    eye = jnp.eye(PEER_HEADS, dtype=F32)
    kbig = jnp.einsum("hcjd,hg->cjhgd", peer_keys[0], eye).reshape(
        2, PEER_N_KEYS * PEER_HEADS, PEER_HEADS * PEER_HALF).astype(BF16)
    wpq = peer_wq[0].reshape(D_MODEL, PEER_HEADS, 2, PEER_HALF).transpose(0, 2, 1, 3).reshape(
        D_MODEL, 2 * PEER_HEADS * PEER_HALF).astype(BF16)
    weights = (norm1_g[0][None], w_in[0].astype(BF16), gm_norm_g[0].reshape(1, GM_WIDTH),
               da_subln_g[0][None], w_o[0].astype(BF16), norm2_g[0][None], wpq, kbig,
               peer_u[0].astype(BF16), peer_v[0].T.astype(BF16), final_g[None], 1.0 - lam_init)

    xp = x_prompt.reshape(batch * seq, D_MODEL)
    tabs_p = _rope_tables(jnp.arange(seq, dtype=F32))
    wmix_p = jnp.tril(gm_ws[0]).astype(BF16)
    bias_p = jnp.repeat(jnp.transpose(gm_b[0]), HEAD_LANES, axis=1)
    attend_p = lambda q1, q2, k32, v32, kb, vb: _prompt_attention(lam, q1, q2, kb, vb, batch, seq)
    y_p, k_p, v_p, _ = _row_pipeline(xp, tabs_p, wmix_p, bias_p, attend_p, weights)

    xs = x_sample.reshape(nb * n_new, D_MODEL)
    tabs_s = _rope_tables(jnp.tile(past + jnp.arange(n_new, dtype=F32), nb))
    rows_s = min(ROW_TILE, nb * n_new)
    w_new = jnp.tril(gm_ws[0][:, :n_new, :n_new])
    r = jnp.arange(rows_s)
    pick = (r[:, None] % n_new == jnp.arange(n_new)[None, :]).astype(F32)
    w_rows = jnp.einsum("ri,hij,cj->hrc", pick, w_new, pick, precision=lax.Precision.HIGHEST)
    wmix_s = jnp.where(r[:, None] // n_new == r[None, :] // n_new, w_rows, 0.0).astype(BF16)
    bias_s = jnp.tile(jnp.repeat(jnp.transpose(gm_b[0][:, :n_new]), HEAD_LANES, axis=1), (rows_s // n_new, 1))
    ck = cache_k[0].reshape(-1, PAGE_SIZE * DA_HEADS, HEAD_LANES)
    cv = cache_v[0].reshape(-1, PAGE_SIZE * DA_HEADS, HEAD_LANES)

    def attend_s(q1, q2, k32, v32, kb, vb):
        per_head = lambda t: jnp.transpose(t.reshape(nb, n_new, DA_HEADS, HEAD_LANES), (0, 2, 1, 3))
        q = jnp.stack([per_head(q1), per_head(q2)], axis=1).reshape(nb, 2 * DA_HEADS * n_new, HEAD_LANES)
        new_rows = lambda t: t.reshape(nb, n_new * DA_HEADS, HEAD_LANES).astype(BF16)
        ao = _decode_attention(page_table, lam, q, new_rows(k32), new_rows(v32), ck, cv)
        ao = jnp.transpose(ao.reshape(nb, DA_HEADS, n_new, HEAD_LANES), (0, 2, 1, 3))
        return ao.reshape(nb * n_new, DA_WIDTH)

    y_s, k_s, v_s, gn_s = _row_pipeline(xs, tabs_s, wmix_s, bias_s, attend_s, weights)

    head_shape = lambda b, s: (1, b, s, DA_HEADS, HEAD_LANES)
    return (y_p.reshape(batch, seq, D_MODEL), y_s.reshape(nb, n_new, D_MODEL),
            k_p.reshape(head_shape(batch, seq)), v_p.reshape(head_shape(batch, seq)),
            k_s.reshape(head_shape(nb, n_new)), v_s.reshape(head_shape(nb, n_new)),
            gn_s.reshape(1, nb, n_new, GM_HEADS, HEAD_LANES))
```

```python
import functools
import math

import jax
import jax.numpy as jnp
from jax import lax
from jax.experimental import pallas as pl
from jax.experimental.pallas import tpu as pltpu

F32 = jnp.float32
BF16 = jnp.bfloat16

D_MODEL = 1024
DA_WIDTH = 512
DA_HEADS = 4
HEAD_LANES = 128
BF16_TILE_ROWS = 16
DA_HEAD_DIM = 64
GM_WIDTH = 512
GM_HEADS = 4
CHUNK = 128
IN_WIDTH = 3 * DA_WIDTH + 2 * GM_WIDTH
ROPE_THETA = 500000.0
ROPE_DIM = 16
PAGE_SIZE = 128
PEER_HEADS = 8
PEER_N_KEYS = 128
PEER_N_EXPERTS = PEER_N_KEYS * PEER_N_KEYS
PEER_HALF = 128
PEER_TOPK = 16
NORM_EPS = 1e-6
NEG_INF = float("-inf")

VMEM_LIMIT_BYTES = 56 * 1024 * 1024

ROW_TILE = 512
ATTN_Q_TILE = 256
ATTN_HEADS_PER_STEP = 2
TOPK_LANES = 128
PEER_EXPERT_TILE = 1024
PEER_TOKEN_TILE = 1024


def _params(*semantics):
    return pltpu.CompilerParams(dimension_semantics=semantics, vmem_limit_bytes=VMEM_LIMIT_BYTES)


def _rms(x, g):
    return x * lax.rsqrt(jnp.mean(x * x, axis=-1, keepdims=True) + NORM_EPS) * g


def _gelu(x):
    return x * (0.5 * (1.0 + jnp.tanh(math.sqrt(2.0 / math.pi) * (x + 0.044715 * (x * x * x)))))


def _const_spec(shape):
    zeros = (0,) * len(shape)
    return pl.BlockSpec(shape, lambda *_: zeros)


def _proj_kernel(x_ref, g1_ref, win_ref, cos_ref, sa_ref, sb_ref, gmg_ref, wmix_ref, bias_ref,
                 q1_ref, q2_ref, k32_ref, v32_ref, kb_ref, vb_ref, gmo_ref, gn_ref, *, mix_rows):
    rows = x_ref.shape[0]
    xn = _rms(x_ref[...], g1_ref[...]).astype(BF16)
    z = jnp.dot(xn, win_ref[...], preferred_element_type=F32)
    cos_t, sin_a, sin_b = cos_ref[...], sa_ref[...], sb_ref[...]
    first_half = lax.broadcasted_iota(jnp.int32, (rows, HEAD_LANES), 1) < DA_HEAD_DIM

    def rope(t):
        return t * cos_t + pltpu.roll(t, HEAD_LANES - 8, 1) * sin_a + pltpu.roll(t, 8, 1) * sin_b

    for b in range(DA_HEADS):
        sl = slice(b * HEAD_LANES, (b + 1) * HEAD_LANES)
        q = rope(z[:, sl]) * (1.0 / math.sqrt(DA_HEAD_DIM))
        q1_ref[:, sl] = jnp.where(first_half, q, 0.0).astype(BF16)
        q2_ref[:, sl] = jnp.where(first_half, 0.0, q).astype(BF16)
        k = rope(z[:, DA_WIDTH + b * HEAD_LANES:DA_WIDTH + (b + 1) * HEAD_LANES])
        v = z[:, 2 * DA_WIDTH + b * HEAD_LANES:2 * DA_WIDTH + (b + 1) * HEAD_LANES]
        k32_ref[pl.ds(b, rows, stride=DA_HEADS), :] = k
        v32_ref[pl.ds(b, rows, stride=DA_HEADS), :] = v
        kb_ref[:, sl] = k.astype(BF16)
        vb_ref[:, sl] = v.astype(BF16)

    u = _gelu(z[:, 3 * DA_WIDTH:3 * DA_WIDTH + GM_WIDTH])
    g = _gelu(z[:, 3 * DA_WIDTH + GM_WIDTH:])
    for b in range(GM_HEADS):
        sl = slice(b * HEAD_LANES, (b + 1) * HEAD_LANES)
        gn = _rms(g[:, sl], gmg_ref[:, sl])
        gn_ref[:, sl] = gn
        gnb = gn.astype(BF16)
        for r in range(rows // mix_rows):
            rs = slice(r * mix_rows, (r + 1) * mix_rows)
            s = jnp.dot(wmix_ref[b], gnb[rs], preferred_element_type=F32) + bias_ref[:, sl]
            gmo_ref[rs, sl] = (u[rs, sl] * s).astype(BF16)


def _proj(x, g1, w_in, cos_t, sin_a, sin_b, gm_g, wmix, bias):
    rows = x.shape[0]
    tile = min(ROW_TILE, rows)
    mix_rows = wmix.shape[-1]
    row_spec = lambda w: pl.BlockSpec((tile, w), lambda i: (i, 0))
    table_blocks = cos_t.shape[0] // tile
    table_spec = pl.BlockSpec((tile, HEAD_LANES), lambda i: (i % table_blocks, 0))
    head_rows_spec = pl.BlockSpec((tile * DA_HEADS, HEAD_LANES), lambda i: (i, 0))
    out_shape = [jax.ShapeDtypeStruct((rows, DA_WIDTH), BF16),
                 jax.ShapeDtypeStruct((rows, DA_WIDTH), BF16),
                 jax.ShapeDtypeStruct((rows * DA_HEADS, HEAD_LANES), F32),
                 jax.ShapeDtypeStruct((rows * DA_HEADS, HEAD_LANES), F32),
                 jax.ShapeDtypeStruct((rows, DA_WIDTH), BF16),
                 jax.ShapeDtypeStruct((rows, DA_WIDTH), BF16),
                 jax.ShapeDtypeStruct((rows, GM_WIDTH), BF16),
                 jax.ShapeDtypeStruct((rows, GM_WIDTH), F32)]
    return pl.pallas_call(
        functools.partial(_proj_kernel, mix_rows=mix_rows),
        grid=(rows // tile,),
        in_specs=[row_spec(D_MODEL), _const_spec((1, D_MODEL)), _const_spec((D_MODEL, IN_WIDTH)),
                  table_spec, table_spec, table_spec,
                  _const_spec((1, GM_WIDTH)), _const_spec(wmix.shape), _const_spec(bias.shape)],
        out_specs=[row_spec(DA_WIDTH)] * 2 + [head_rows_spec] * 2 + [row_spec(DA_WIDTH)] * 2
                  + [row_spec(GM_WIDTH)] * 2,
        out_shape=out_shape,
        compiler_params=_params("parallel"),
        name="proj",
    )(x, g1, w_in, cos_t, sin_a, sin_b, gm_g, wmix, bias)


def _attn_kernel(lam_ref, q1_ref, q2_ref, k_ref, v_ref, o_ref):
    tq = q1_ref.shape[0]
    heads = q1_ref.shape[1] // HEAD_LANES
    i = pl.program_id(2)
    head_lanes = [slice(h * HEAD_LANES, (h + 1) * HEAD_LANES) for h in range(heads)]
    qs = [jnp.concatenate([q1_ref[:, sl], q2_ref[:, sl]], axis=0) for sl in head_lanes]

    def step(j, carry, masked):
        start = pl.multiple_of(j * tq, tq)
        new = []
        for q, sl, (m, l, acc) in zip(qs, head_lanes, carry):
            kb = k_ref[pl.ds(start, tq), sl]
            vb = v_ref[pl.ds(start, tq), sl]
            s = lax.dot_general(q, kb, (((1,), (1,)), ((), ())), preferred_element_type=F32)
            if masked:
                row = lax.broadcasted_iota(jnp.int32, (2 * tq, tq), 0)
                col = lax.broadcasted_iota(jnp.int32, (2 * tq, tq), 1)
                row = jnp.where(row >= tq, row - tq, row)
                s = jnp.where(col <= row, s, NEG_INF)
            m_new = jnp.maximum(m, jnp.max(s, axis=-1, keepdims=True))
            alpha = jnp.exp(m - m_new)
            p = jnp.exp(s - m_new)
            l = alpha * l + jnp.sum(p, axis=-1, keepdims=True)
            acc = alpha * acc + jnp.dot(p.astype(BF16), vb, preferred_element_type=F32)
            new.append((m_new, l, acc))
        return tuple(new)

    init = tuple((jnp.full((2 * tq, 1), NEG_INF, F32), jnp.zeros((2 * tq, 1), F32),
                  jnp.zeros((2 * tq, HEAD_LANES), F32)) for _ in range(heads))
    carry = lax.fori_loop(0, i, lambda j, c: step(j, c, False), init)
    for sl, (_, l, acc) in zip(head_lanes, step(i, carry, True)):
        o = acc / l
        o_ref[:, sl] = o[:tq] - lam_ref[0] * o[tq:]


def _prompt_attention(lam, q1, q2, kb, vb, batch, seq):
    tq = min(ATTN_Q_TILE, seq)
    nq = seq // tq
    width = ATTN_HEADS_PER_STEP * HEAD_LANES
    q_spec = pl.BlockSpec((tq, width), lambda b, h, i: (b * nq + i, h))
    kv_spec = pl.BlockSpec((seq, width), lambda b, h, i: (b, h))
    return pl.pallas_call(
        _attn_kernel,
        grid=(batch, DA_HEADS // ATTN_HEADS_PER_STEP, nq),
        in_specs=[pl.BlockSpec(memory_space=pltpu.SMEM), q_spec, q_spec, kv_spec, kv_spec],
        out_specs=q_spec,
        out_shape=jax.ShapeDtypeStruct((batch * seq, DA_WIDTH), F32),
        compiler_params=_params("parallel", "parallel", "arbitrary"),
        name="prompt_attn",
    )(lam, q1, q2, kb, vb)


def _decode_attn_kernel(pt_ref, lam_ref, q_ref, kn_ref, vn_ref, *rest, n_pages, n_new):
    del pt_ref
    k_pages, v_pages = rest[:n_pages], rest[n_pages:2 * n_pages]
    o_ref, k_all, v_all = rest[2 * n_pages:]
    page_rows = PAGE_SIZE * DA_HEADS
    for p in range(n_pages):
        rs = slice(p * page_rows, (p + 1) * page_rows)
        k_all[rs, :] = k_pages[p][0].astype(BF16)
        v_all[rs, :] = v_pages[p][0].astype(BF16)
    q = q_ref[0]
    nrow = q.shape[0]
    contract_last = (((1,), (1,)), ((), ()))
    s_past = lax.dot_general(q, k_all[...], contract_last, preferred_element_type=F32)
    s_new = lax.dot_general(q, kn_ref[0], contract_last, preferred_element_type=F32)

    def row_col(shape):
        row = lax.broadcasted_iota(jnp.int32, shape, 0)
        col = lax.broadcasted_iota(jnp.int32, shape, 1)
        return (row // n_new) % DA_HEADS, row % n_new, col % DA_HEADS, col // DA_HEADS

    q_head, _, k_head, _ = row_col(s_past.shape)
    s_past = jnp.where(q_head == k_head, s_past, NEG_INF)
    q_head, q_tok, k_head, k_tok = row_col(s_new.shape)
    s_new = jnp.where(q_head == k_head, jnp.where(k_tok <= q_tok, s_new, NEG_INF), NEG_INF)
    m = jnp.maximum(jnp.max(s_past, axis=-1, keepdims=True), jnp.max(s_new, axis=-1, keepdims=True))
    p_past = jnp.exp(s_past - m)
    p_new = jnp.exp(s_new - m)
    inv = 1.0 / (jnp.sum(p_past, axis=-1, keepdims=True) + jnp.sum(p_new, axis=-1, keepdims=True))
    half = nrow // 2
    lam = lam_ref[0]
    a_past = (p_past[:half] * inv[:half] - lam * (p_past[half:] * inv[half:])).astype(BF16)
    a_new = (p_new[:half] * inv[:half] - lam * (p_new[half:] * inv[half:])).astype(BF16)
    o_ref[0] = (jnp.dot(a_past, v_all[...], preferred_element_type=F32)
                + jnp.dot(a_new, vn_ref[0], preferred_element_type=F32))


def _decode_attention(page_table, lam, q_rows, k_new, v_new, cache_k, cache_v):
    nb, n_pages = page_table.shape
    n_new = k_new.shape[1] // DA_HEADS
    page_rows = PAGE_SIZE * DA_HEADS
    page_spec = lambda p: pl.BlockSpec((1, page_rows, HEAD_LANES), lambda b, pt: (pt[b * n_pages + p], 0, 0))
    per_b = lambda r: pl.BlockSpec((1, r, HEAD_LANES), lambda b, pt: (b, 0, 0))
    grid_spec = pltpu.PrefetchScalarGridSpec(
        num_scalar_prefetch=1,
        grid=(nb,),
        in_specs=[pl.BlockSpec(memory_space=pltpu.SMEM), per_b(q_rows.shape[1]),
                  per_b(n_new * DA_HEADS), per_b(n_new * DA_HEADS)]
                 + [page_spec(p) for p in range(n_pages)] * 2,
        out_specs=per_b(n_new * DA_HEADS),
        scratch_shapes=[pltpu.VMEM((n_pages * page_rows, HEAD_LANES), BF16)] * 2,
    )
    return pl.pallas_call(
        functools.partial(_decode_attn_kernel, n_pages=n_pages, n_new=n_new),
        grid_spec=grid_spec,
        out_shape=jax.ShapeDtypeStruct((nb, n_new * DA_HEADS, HEAD_LANES), F32),
        compiler_params=_params("arbitrary"),
        name="decode_attn",
    )(page_table.reshape(-1), lam, q_rows, k_new, v_new,
      *([cache_k] * n_pages), *([cache_v] * n_pages))


def _merge_kernel(ao_ref, gmo_ref, x_ref, sg_ref, wo_ref, g2_ref, wpq_ref, kbig_ref,
                  h_ref, tn_ref, s_ref, *, attn_scale):
    parts = []
    for b in range(DA_HEADS):
        sl = slice(b * HEAD_LANES, (b + 1) * HEAD_LANES)
        parts.append((_rms(ao_ref[:, sl], sg_ref[...]) * attn_scale).astype(BF16))
    cat = jnp.concatenate(parts + [gmo_ref[...]], axis=1)
    h = x_ref[...] + jnp.dot(cat, wo_ref[...], preferred_element_type=F32)
    h_ref[...] = h
    tn = _rms(h, g2_ref[...]).astype(BF16)
    tn_ref[...] = tn
    qp = jnp.dot(tn, wpq_ref[...], preferred_element_type=F32).astype(BF16)
    width = PEER_HEADS * PEER_HALF
    for c in range(2):
        s_ref[c] = lax.dot_general(kbig_ref[c], qp[:, c * width:(c + 1) * width],
                                   (((1,), (1,)), ((), ())), preferred_element_type=F32)


def _merge(ao, gmo, x, subln_g, w_o, g2, wpq, kbig, attn_scale):
    rows = x.shape[0]
    tile = min(ROW_TILE, rows)
    width = PEER_HEADS * PEER_HALF
    row_spec = lambda w: pl.BlockSpec((tile, w), lambda i: (i, 0))
    return pl.pallas_call(
        functools.partial(_merge_kernel, attn_scale=attn_scale),
        grid=(rows // tile,),
        in_specs=[row_spec(DA_WIDTH), row_spec(GM_WIDTH), row_spec(D_MODEL), _const_spec((1, HEAD_LANES)),
                  _const_spec((D_MODEL, D_MODEL)), _const_spec((1, D_MODEL)),
                  _const_spec((D_MODEL, 2 * width)), _const_spec((2, PEER_N_KEYS * PEER_HEADS, width))],
        out_specs=[row_spec(D_MODEL), row_spec(D_MODEL),
                   pl.BlockSpec((2, PEER_N_KEYS * PEER_HEADS, tile), lambda i: (0, 0, i))],
        out_shape=[jax.ShapeDtypeStruct((rows, D_MODEL), F32),
                   jax.ShapeDtypeStruct((rows, D_MODEL), BF16),
                   jax.ShapeDtypeStruct((2, PEER_N_KEYS * PEER_HEADS, rows), F32)],
        compiler_params=_params("parallel"),
        name="merge",
    )(ao, gmo, x, subln_g, w_o, g2, wpq, kbig)


def _young_candidates():
    return [(p, q) for p in range(PEER_TOPK) for q in range(PEER_TOPK) if (p + 1) * (q + 1) <= PEER_TOPK]


def _topk_kernel(s_ref, r2_ref, e2_ref, n_ref, c_ref, work_ref, top_ref, idx_ref, young_ref,
                 rank_ref, exp_ref):
    nk = PEER_N_KEYS
    tile = s_ref.shape[2:]
    one, zero = jnp.ones(tile, F32), jnp.zeros(tile, F32)

    def extract_distinct(c):
        def round_(r, bound):
            tree = [jnp.where(s_ref[c, j] < bound, s_ref[c, j], NEG_INF) for j in range(nk)]
            while len(tree) > 1:
                tree = [jnp.maximum(tree[t], tree[t + 1]) for t in range(0, len(tree), 2)]
            top_ref[c, r] = tree[0]
            return tree[0]

        last = lax.fori_loop(0, PEER_TOPK, round_, jnp.full(tile, float("inf"), F32))
        reached = zero
        for j in range(nk):
            reached = reached + jnp.where(s_ref[c, j] >= last, one, zero)
        return reached

    def extract(c):
        for j in range(nk):
            work_ref[j] = s_ref[c, j]

        def round_(r, _):
            vals = [work_ref[j] for j in range(nk)]
            idxs = [None] * nk
            width = 1
            while len(vals) > 1:
                nv, ni = [], []
                for t in range(0, len(vals), 2):
                    a, b = vals[t], vals[t + 1]
                    take_b = b > a
                    ia = idxs[t] if idxs[t] is not None else float(t * width)
                    ib = idxs[t + 1] if idxs[t + 1] is not None else float((t + 1) * width)
                    nv.append(jnp.maximum(a, b))
                    ni.append(jnp.where(take_b, ib, ia))
                vals, idxs = nv, ni
                width *= 2
            best, best_idx = vals[0], idxs[0]
            top_ref[c, r] = best
            idx_ref[c, r] = best_idx
            for j in range(nk):
                work_ref[j] = jnp.where(best_idx == float(j), NEG_INF, work_ref[j])
            return 0

        lax.fori_loop(0, PEER_TOPK, round_, 0)

    miscount = jnp.maximum(jnp.abs(extract_distinct(0) - float(PEER_TOPK)),
                           jnp.abs(extract_distinct(1) - float(PEER_TOPK)))
    tied = jnp.max(miscount) > 0.0

    @pl.when(tied)
    def _():
        extract(0)
        extract(1)

    a = [top_ref[0, p] for p in range(PEER_TOPK)]
    b = [top_ref[1, q] for q in range(PEER_TOPK)]
    cands = _young_candidates()
    cand = {pq: a[pq[0]] + b[pq[1]] for pq in cands}
    n_of_p = [zero] * PEER_TOPK
    ea = [jnp.exp(a[p] - a[0]) for p in range(PEER_TOPK)]
    eb = [jnp.exp(b[q] - b[0]) for q in range(PEER_TOPK)]
    z = zero
    for (p, q) in cands:
        fixed = 0
        beaten = zero
        for (p2, q2) in cands:
            if (p2, q2) == (p, q):
                continue
            if p2 <= p and q2 <= q:
                fixed += 1
            elif p2 >= p and q2 >= q:
                continue
            elif p2 * PEER_TOPK + q2 < p * PEER_TOPK + q:
                beaten = beaten + jnp.where(cand[(p2, q2)] >= cand[(p, q)], one, zero)
            else:
                beaten = beaten + jnp.where(cand[(p2, q2)] > cand[(p, q)], one, zero)
        sel = jnp.where(beaten + float(fixed) < float(PEER_TOPK), one, zero)
        n_of_p[p] = n_of_p[p] + sel
        z = z + sel * (ea[p] * eb[q])
    inv_z = 1.0 / z
    for p in range(PEER_TOPK):
        young_ref[p] = n_of_p[p]

    def scatter(hit_a, hit_b):
        for j in range(nk):
            n_j = zero
            r_j = jnp.full(tile, float(nk - 1), F32)
            for p in range(PEER_TOPK):
                n_j = jnp.where(hit_a(p, j), young_ref[p], n_j)
                r_j = jnp.where(hit_b(p, j), float(p), r_j)
            n_ref[j] = n_j
            rank_ref[j * PEER_HEADS:(j + 1) * PEER_HEADS, :] = r_j

    @pl.when(jnp.logical_not(tied))
    def _():
        scatter(lambda p, j: top_ref[0, p] == s_ref[0, j], lambda p, j: top_ref[1, p] == s_ref[1, j])

    @pl.when(tied)
    def _():
        scatter(lambda p, j: idx_ref[0, p] == float(j), lambda p, j: idx_ref[1, p] == float(j))

    for j in range(nk):
        c_ref[j] = jnp.exp(s_ref[0, j] - a[0]) * inv_z
        exp_ref[j * PEER_HEADS:(j + 1) * PEER_HEADS, :] = jnp.exp(s_ref[1, j] - b[0])
    for h in range(PEER_HEADS):
        r2_ref[h] = rank_ref[pl.ds(h, nk, stride=PEER_HEADS), :].astype(BF16)
        e2_ref[h] = exp_ref[pl.ds(h, nk, stride=PEER_HEADS), :].astype(BF16)


def _topk(scores):
    tokens = scores.shape[-1]
    lanes = TOPK_LANES
    tab2 = jax.ShapeDtypeStruct((PEER_HEADS, PEER_N_KEYS, tokens), BF16)
    tab2_spec = pl.BlockSpec((PEER_HEADS, PEER_N_KEYS, lanes), lambda i: (0, 0, i))
    tab1 = jax.ShapeDtypeStruct((tokens // lanes, PEER_N_KEYS, PEER_HEADS, lanes), F32)
    tab1_spec = pl.BlockSpec((None, PEER_N_KEYS, PEER_HEADS, lanes), lambda i: (i, 0, 0, 0))
    return pl.pallas_call(
        _topk_kernel,
        grid=(tokens // lanes,),
        in_specs=[pl.BlockSpec((2, PEER_N_KEYS, PEER_HEADS, lanes), lambda i: (0, 0, 0, i))],
        out_specs=[tab2_spec] * 2 + [tab1_spec] * 2,
        out_shape=[tab2] * 2 + [tab1] * 2,
        scratch_shapes=[pltpu.VMEM((PEER_N_KEYS, PEER_HEADS, lanes), F32),
                        pltpu.VMEM((2, PEER_TOPK, PEER_HEADS, lanes), F32),
                        pltpu.VMEM((2, PEER_TOPK, PEER_HEADS, lanes), F32),
                        pltpu.VMEM((PEER_TOPK, PEER_HEADS, lanes), F32),
                        pltpu.VMEM((PEER_N_KEYS * PEER_HEADS, lanes), F32),
                        pltpu.VMEM((PEER_N_KEYS * PEER_HEADS, lanes), F32)],
        compiler_params=_params("parallel"),
        name="peer_topk",
    )(scores)


def _peer_kernel(tn_ref, pu_ref, pvt_ref, r2_ref, e2_ref, n_ref, c_ref, acc_ref,
                 at_ref, gate_ref, hid_ref, tab_ref):
    e = pl.program_id(1)
    tt = tn_ref.shape[0]
    nk = PEER_N_KEYS
    keys_per_step = pu_ref.shape[0] // nk
    lane_tiles = [(t, slice(t * HEAD_LANES, (t + 1) * HEAD_LANES)) for t in range(tt // HEAD_LANES)]
    pack = BF16_TILE_ROWS

    @pl.when(e == 0)
    def _():
        tab_ref[:, 0, :nk, :tt] = r2_ref[...]
        tab_ref[:, 1, :nk, :tt] = e2_ref[...]
        acc_ref[...] = jnp.zeros_like(acc_ref)

    at_ref[...] = lax.dot_general(pu_ref[...], tn_ref[...], (((1,), (1,)), ((), ())),
                                  preferred_element_type=F32).astype(BF16)
    for ii in range(keys_per_step):
        for t, lanes in lane_tiles:
            gate = jnp.zeros((nk, HEAD_LANES), BF16)
            for h in range(PEER_HEADS):
                row = ii * PEER_HEADS + h
                n = jnp.broadcast_to(n_ref[t, row:row + 1, :], (pack, HEAD_LANES)).astype(BF16)
                c = jnp.broadcast_to(c_ref[t, row:row + 1, :], (pack, HEAD_LANES)).astype(BF16)
                n = jnp.tile(n, (nk // pack, 1))
                c = jnp.tile(c, (nk // pack, 1))
                gate = gate + jnp.where(tab_ref[h, 0, :nk, lanes] < n, tab_ref[h, 1, :nk, lanes] * c,
                                        jnp.zeros_like(gate))
            gate_ref[ii * nk:(ii + 1) * nk, lanes] = gate
    hid_ref[...] = gate_ref[:, :tt] * _gelu(at_ref[...])

    @pl.when(e < pl.num_programs(1))
    def _():
        acc_ref[...] += jnp.dot(pvt_ref[...], hid_ref[...], preferred_element_type=F32)


def _peer(tn, pu, pvt, r2, e2, ntab, ctab):
    tokens = tn.shape[0]
    tt = min(PEER_TOKEN_TILE, tokens)
    et = PEER_EXPERT_TILE
    rows_per_step = (et // PEER_N_KEYS) * PEER_HEADS
    whole_tab = pl.BlockSpec((PEER_HEADS, PEER_N_KEYS, tt), lambda t, e: (0, 0, t))
    step_tab = pl.BlockSpec((tt // HEAD_LANES, rows_per_step, HEAD_LANES), lambda t, e: (t, e, 0))
    return pl.pallas_call(
        _peer_kernel,
        grid=(tokens // tt, PEER_N_EXPERTS // et),
        in_specs=[pl.BlockSpec((tt, D_MODEL), lambda t, e: (t, 0)),
                  pl.BlockSpec((et, D_MODEL), lambda t, e: (e, 0)),
                  pl.BlockSpec((D_MODEL, et), lambda t, e: (0, e)),
                  whole_tab, whole_tab, step_tab, step_tab],
        out_specs=pl.BlockSpec((D_MODEL, tt), lambda t, e: (0, t)),
        out_shape=jax.ShapeDtypeStruct((D_MODEL, tokens), F32),
        scratch_shapes=[pltpu.VMEM((et, tt), BF16), pltpu.VMEM((et, tt + HEAD_LANES), BF16),
                        pltpu.VMEM((et, tt), BF16),
                        pltpu.VMEM((PEER_HEADS, 2, PEER_N_KEYS + BF16_TILE_ROWS, tt + HEAD_LANES), BF16)],
        compiler_params=_params("parallel", "arbitrary"),
        name="peer_experts",
    )(tn, pu, pvt, r2, e2, ntab, ctab)


def _final_kernel(h_ref, mix_ref, gf_ref, y_ref):
    y_ref[...] = _rms(h_ref[...] + jnp.transpose(mix_ref[...]), gf_ref[...])


def _final(h, mix_t, final_g):
    rows = h.shape[0]
    tile = min(ROW_TILE, rows)
    row_spec = pl.BlockSpec((tile, D_MODEL), lambda i: (i, 0))
    return pl.pallas_call(
        _final_kernel,
        grid=(rows // tile,),
        in_specs=[row_spec, pl.BlockSpec((D_MODEL, tile), lambda i: (0, i)), _const_spec((1, D_MODEL))],
        out_specs=row_spec,
        out_shape=jax.ShapeDtypeStruct((rows, D_MODEL), F32),
        compiler_params=_params("parallel"),
        name="final_norm",
    )(h, mix_t, final_g)


def _rope_tables(pos):
    half = ROPE_DIM // 2
    inv = ROPE_THETA ** (-jnp.arange(half, dtype=F32) * 2.0 / ROPE_DIM)
    d = jnp.arange(HEAD_LANES) % DA_HEAD_DIM
    ang = pos[:, None] * inv[d % half][None, :]
    cos_t = jnp.where(d < ROPE_DIM, jnp.cos(ang), 1.0)
    sin_a = jnp.where(d < half, -jnp.sin(ang), 0.0)
    sin_b = jnp.where((d >= half) & (d < ROPE_DIM), jnp.sin(ang), 0.0)
    return cos_t, sin_a, sin_b


def _row_pipeline(x, rope_tabs, wmix, bias, attend, weights):
    (g1, w_in, gm_g, subln_g, w_o, g2, wpq, kbig, pu, pvt, final_g, attn_scale) = weights
    q1, q2, k32, v32, kb, vb, gmo, gn = _proj(x, g1, w_in, *rope_tabs, gm_g, wmix, bias)
    ao = attend(q1, q2, k32, v32, kb, vb)
    h, tn, scores = _merge(ao, gmo, x, subln_g, w_o, g2, wpq, kbig, attn_scale)
    rows = x.shape[0]
    r2, e2, ntab, ctab = _topk(scores.reshape(2, PEER_N_KEYS, PEER_HEADS, rows))
    flat = lambda t: t.reshape(rows // TOPK_LANES, PEER_N_KEYS * PEER_HEADS, TOPK_LANES)
    y = _final(h, _peer(tn, pu, pvt, r2, e2, flat(ntab), flat(ctab)), final_g)
    return y, k32, v32, gn


def kernel(x_prompt, x_sample, cache_k, cache_v, page_table, norm1_g, w_in, gm_norm_g, gm_ws, gm_b,
           da_lq1, da_lk1, da_lq2, da_lk2, da_subln_g, w_o, norm2_g, peer_wq, peer_keys, peer_u,
           peer_v, final_g):
    batch, seq, _ = x_prompt.shape
    nb, n_new, _ = x_sample.shape
    n_pages = page_table.shape[1]
    past = n_pages * PAGE_SIZE
    assert w_in.shape[0] == 1 and seq % CHUNK == 0 and seq % min(ROW_TILE, batch * seq) == 0

    lam_init = 0.8 - 0.6 * math.exp(-0.3 * 0)
    dots = lambda a, b: jnp.exp(jnp.sum(a.astype(F32) * b.astype(F32)))
    lam = (dots(da_lq1[0], da_lk1[0]) - dots(da_lq2[0], da_lk2[0]) + lam_init).reshape(1).astype(F32)

    eye = jnp.eye(PEER_HEADS, dtype=F32)
    kbig = jnp.einsum("hcjd,hg->cjhgd", peer_keys[0], eye).reshape(
        2, PEER_N_KEYS * PEER_HEADS, PEER_HEADS * PEER_HALF).astype(BF16)
    wpq = peer_wq[0].reshape(D_MODEL, PEER_HEADS, 2, PEER_HALF).transpose(0, 2, 1, 3).reshape(
        D_MODEL, 2 * PEER_HEADS * PEER_HALF).astype(BF16)
    weights = (norm1_g[0][None], w_in[0].astype(BF16), gm_norm_g[0].reshape(1, GM_WIDTH),
               da_subln_g[0][None], w_o[0].astype(BF16), norm2_g[0][None], wpq, kbig,
               peer_u[0].astype(BF16), peer_v[0].T.astype(BF16), final_g[None], 1.0 - lam_init)

    xp = x_prompt.reshape(batch * seq, D_MODEL)
    tabs_p = _rope_tables(jnp.arange(seq, dtype=F32))
    wmix_p = jnp.tril(gm_ws[0]).astype(BF16)
    bias_p = jnp.repeat(jnp.transpose(gm_b[0]), HEAD_LANES, axis=1)
    attend_p = lambda q1, q2, k32, v32, kb, vb: _prompt_attention(lam, q1, q2, kb, vb, batch, seq)
    y_p, k_p, v_p, _ = _row_pipeline(xp, tabs_p, wmix_p, bias_p, attend_p, weights)

    xs = x_sample.reshape(nb * n_new, D_MODEL)
    tabs_s = _rope_tables(jnp.tile(past + jnp.arange(n_new, dtype=F32), nb))
    rows_s = min(ROW_TILE, nb * n_new)
    w_new = jnp.tril(gm_ws[0][:, :n_new, :n_new])
    r = jnp.arange(rows_s)
    pick = (r[:, None] % n_new == jnp.arange(n_new)[None, :]).astype(F32)
    w_rows = jnp.einsum("ri,hij,cj->hrc", pick, w_new, pick, precision=lax.Precision.HIGHEST)
    wmix_s = jnp.where(r[:, None] // n_new == r[None, :] // n_new, w_rows, 0.0).astype(BF16)
    bias_s = jnp.tile(jnp.repeat(jnp.transpose(gm_b[0][:, :n_new]), HEAD_LANES, axis=1), (rows_s // n_new, 1))
    ck = cache_k[0].reshape(-1, PAGE_SIZE * DA_HEADS, HEAD_LANES)
    cv = cache_v[0].reshape(-1, PAGE_SIZE * DA_HEADS, HEAD_LANES)

    def attend_s(q1, q2, k32, v32, kb, vb):
        per_head = lambda t: jnp.transpose(t.reshape(nb, n_new, DA_HEADS, HEAD_LANES), (0, 2, 1, 3))
        q = jnp.stack([per_head(q1), per_head(q2)], axis=1).reshape(nb, 2 * DA_HEADS * n_new, HEAD_LANES)
        new_rows = lambda t: t.reshape(nb, n_new * DA_HEADS, HEAD_LANES).astype(BF16)
        ao = _decode_attention(page_table, lam, q, new_rows(k32), new_rows(v32), ck, cv)
        ao = jnp.transpose(ao.reshape(nb, DA_HEADS, n_new, HEAD_LANES), (0, 2, 1, 3))
        return ao.reshape(nb * n_new, DA_WIDTH)

    y_s, k_s, v_s, gn_s = _row_pipeline(xs, tabs_s, wmix_s, bias_s, attend_s, weights)

    head_shape = lambda b, s: (1, b, s, DA_HEADS, HEAD_LANES)
    return (y_p.reshape(batch, seq, D_MODEL), y_s.reshape(nb, n_new, D_MODEL),
            k_p.reshape(head_shape(batch, seq)), v_p.reshape(head_shape(batch, seq)),
            k_s.reshape(head_shape(nb, n_new)), v_s.reshape(head_shape(nb, n_new)),
            gn_s.reshape(1, nb, n_new, GM_HEADS, HEAD_LANES))
```

```python
import functools
import math

import jax
import jax.numpy as jnp
from jax import lax
from jax.experimental import pallas as pl
from jax.experimental.pallas import tpu as pltpu

F32 = jnp.float32
BF16 = jnp.bfloat16

D_MODEL = 1024
DA_WIDTH = 512
DA_HEADS = 4
HEAD_LANES = 128
BF16_TILE_ROWS = 16
DA_HEAD_DIM = 64
GM_WIDTH = 512
GM_HEADS = 4
CHUNK = 128
IN_WIDTH = 3 * DA_WIDTH + 2 * GM_WIDTH
ROPE_THETA = 500000.0
ROPE_DIM = 16
PAGE_SIZE = 128
PEER_HEADS = 8
PEER_N_KEYS = 128
PEER_N_EXPERTS = PEER_N_KEYS * PEER_N_KEYS
PEER_HALF = 128
PEER_TOPK = 16
NORM_EPS = 1e-6
NEG_INF = float("-inf")

VMEM_LIMIT_BYTES = 56 * 1024 * 1024

ROW_TILE = 512
ATTN_Q_TILE = 256
ATTN_HEADS_PER_STEP = 2
TOPK_LANES = 128
PEER_EXPERT_TILE = 1024
PEER_TOKEN_TILE = 1024


def _params(*semantics):
    return pltpu.CompilerParams(dimension_semantics=semantics, vmem_limit_bytes=VMEM_LIMIT_BYTES)


def _rms(x, g):
    return x * lax.rsqrt(jnp.mean(x * x, axis=-1, keepdims=True) + NORM_EPS) * g


def _gelu(x):
    return x * (0.5 * (1.0 + jnp.tanh(math.sqrt(2.0 / math.pi) * (x + 0.044715 * (x * x * x)))))


def _const_spec(shape):
    zeros = (0,) * len(shape)
    return pl.BlockSpec(shape, lambda *_: zeros)


def _proj_kernel(x_ref, g1_ref, win_ref, cos_ref, sa_ref, sb_ref, gmg_ref, wmix_ref, bias_ref,
                 q1_ref, q2_ref, k32_ref, v32_ref, kb_ref, vb_ref, gmo_ref, gn_ref, *, mix_rows):
    rows = x_ref.shape[0]
    xn = _rms(x_ref[...], g1_ref[...]).astype(BF16)
    z = jnp.dot(xn, win_ref[...], preferred_element_type=F32)
    cos_t, sin_a, sin_b = cos_ref[...], sa_ref[...], sb_ref[...]
    first_half = lax.broadcasted_iota(jnp.int32, (rows, HEAD_LANES), 1) < DA_HEAD_DIM

    def rope(t):
        return t * cos_t + pltpu.roll(t, HEAD_LANES - 8, 1) * sin_a + pltpu.roll(t, 8, 1) * sin_b

    for b in range(DA_HEADS):
        sl = slice(b * HEAD_LANES, (b + 1) * HEAD_LANES)
        q = rope(z[:, sl]) * (1.0 / math.sqrt(DA_HEAD_DIM))
        q1_ref[:, sl] = jnp.where(first_half, q, 0.0).astype(BF16)
        q2_ref[:, sl] = jnp.where(first_half, 0.0, q).astype(BF16)
        k = rope(z[:, DA_WIDTH + b * HEAD_LANES:DA_WIDTH + (b + 1) * HEAD_LANES])
        v = z[:, 2 * DA_WIDTH + b * HEAD_LANES:2 * DA_WIDTH + (b + 1) * HEAD_LANES]
        k32_ref[pl.ds(b, rows, stride=DA_HEADS), :] = k
        v32_ref[pl.ds(b, rows, stride=DA_HEADS), :] = v
        kb_ref[:, sl] = k.astype(BF16)
        vb_ref[:, sl] = v.astype(BF16)

    u = _gelu(z[:, 3 * DA_WIDTH:3 * DA_WIDTH + GM_WIDTH])
    g = _gelu(z[:, 3 * DA_WIDTH + GM_WIDTH:])
    for b in range(GM_HEADS):
        sl = slice(b * HEAD_LANES, (b + 1) * HEAD_LANES)
        gn = _rms(g[:, sl], gmg_ref[:, sl])
        gn_ref[:, sl] = gn
        gnb = gn.astype(BF16)
        for r in range(rows // mix_rows):
            rs = slice(r * mix_rows, (r + 1) * mix_rows)
            s = jnp.dot(wmix_ref[b], gnb[rs], preferred_element_type=F32) + bias_ref[:, sl]
            gmo_ref[rs, sl] = (u[rs, sl] * s).astype(BF16)


def _proj(x, g1, w_in, cos_t, sin_a, sin_b, gm_g, wmix, bias):
    rows = x.shape[0]
    tile = min(ROW_TILE, rows)
    mix_rows = wmix.shape[-1]
    row_spec = lambda w: pl.BlockSpec((tile, w), lambda i: (i, 0))
    table_blocks = cos_t.shape[0] // tile
    table_spec = pl.BlockSpec((tile, HEAD_LANES), lambda i: (i % table_blocks, 0))
    head_rows_spec = pl.BlockSpec((tile * DA_HEADS, HEAD_LANES), lambda i: (i, 0))
    out_shape = [jax.ShapeDtypeStruct((rows, DA_WIDTH), BF16),
                 jax.ShapeDtypeStruct((rows, DA_WIDTH), BF16),
                 jax.ShapeDtypeStruct((rows * DA_HEADS, HEAD_LANES), F32),
                 jax.ShapeDtypeStruct((rows * DA_HEADS, HEAD_LANES), F32),
                 jax.ShapeDtypeStruct((rows, DA_WIDTH), BF16),
                 jax.ShapeDtypeStruct((rows, DA_WIDTH), BF16),
                 jax.ShapeDtypeStruct((rows, GM_WIDTH), BF16),
                 jax.ShapeDtypeStruct((rows, GM_WIDTH), F32)]
    return pl.pallas_call(
        functools.partial(_proj_kernel, mix_rows=mix_rows),
        grid=(rows // tile,),
        in_specs=[row_spec(D_MODEL), _const_spec((1, D_MODEL)), _const_spec((D_MODEL, IN_WIDTH)),
                  table_spec, table_spec, table_spec,
                  _const_spec((1, GM_WIDTH)), _const_spec(wmix.shape), _const_spec(bias.shape)],
        out_specs=[row_spec(DA_WIDTH)] * 2 + [head_rows_spec] * 2 + [row_spec(DA_WIDTH)] * 2
                  + [row_spec(GM_WIDTH)] * 2,
        out_shape=out_shape,
        compiler_params=_params("parallel"),
        name="proj",
    )(x, g1, w_in, cos_t, sin_a, sin_b, gm_g, wmix, bias)


def _attn_kernel(lam_ref, q1_ref, q2_ref, k_ref, v_ref, o_ref):
    tq = q1_ref.shape[0]
    heads = q1_ref.shape[1] // HEAD_LANES
    i = pl.program_id(2)
    head_lanes = [slice(h * HEAD_LANES, (h + 1) * HEAD_LANES) for h in range(heads)]
    qs = [jnp.concatenate([q1_ref[:, sl], q2_ref[:, sl]], axis=0) for sl in head_lanes]

    def step(j, carry, masked):
        start = pl.multiple_of(j * tq, tq)
        new = []
        for q, sl, (m, l, acc) in zip(qs, head_lanes, carry):
            kb = k_ref[pl.ds(start, tq), sl]
            vb = v_ref[pl.ds(start, tq), sl]
            s = lax.dot_general(q, kb, (((1,), (1,)), ((), ())), preferred_element_type=F32)
            if masked:
                row = lax.broadcasted_iota(jnp.int32, (2 * tq, tq), 0)
                col = lax.broadcasted_iota(jnp.int32, (2 * tq, tq), 1)
                row = jnp.where(row >= tq, row - tq, row)
                s = jnp.where(col <= row, s, NEG_INF)
            m_new = jnp.maximum(m, jnp.max(s, axis=-1, keepdims=True))
            alpha = jnp.exp(m - m_new)
            p = jnp.exp(s - m_new)
            l = alpha * l + jnp.sum(p, axis=-1, keepdims=True)
            acc = alpha * acc + jnp.dot(p.astype(BF16), vb, preferred_element_type=F32)
            new.append((m_new, l, acc))
        return tuple(new)

    init = tuple((jnp.full((2 * tq, 1), NEG_INF, F32), jnp.zeros((2 * tq, 1), F32),
                  jnp.zeros((2 * tq, HEAD_LANES), F32)) for _ in range(heads))
    carry = lax.fori_loop(0, i, lambda j, c: step(j, c, False), init)
    for sl, (_, l, acc) in zip(head_lanes, step(i, carry, True)):
        o = acc / l
        o_ref[:, sl] = o[:tq] - lam_ref[0] * o[tq:]


def _prompt_attention(lam, q1, q2, kb, vb, batch, seq):
    tq = min(ATTN_Q_TILE, seq)
    nq = seq // tq
    width = ATTN_HEADS_PER_STEP * HEAD_LANES
    q_spec = pl.BlockSpec((tq, width), lambda b, h, i: (b * nq + i, h))
    kv_spec = pl.BlockSpec((seq, width), lambda b, h, i: (b, h))
    return pl.pallas_call(
        _attn_kernel,
        grid=(batch, DA_HEADS // ATTN_HEADS_PER_STEP, nq),
        in_specs=[pl.BlockSpec(memory_space=pltpu.SMEM), q_spec, q_spec, kv_spec, kv_spec],
        out_specs=q_spec,
        out_shape=jax.ShapeDtypeStruct((batch * seq, DA_WIDTH), F32),
        compiler_params=_params("parallel", "parallel", "arbitrary"),
        name="prompt_attn",
    )(lam, q1, q2, kb, vb)


def _decode_attn_kernel(pt_ref, lam_ref, q_ref, kn_ref, vn_ref, *rest, n_pages, n_new):
    del pt_ref
    k_pages, v_pages = rest[:n_pages], rest[n_pages:2 * n_pages]
    o_ref, k_all, v_all = rest[2 * n_pages:]
    page_rows = PAGE_SIZE * DA_HEADS
    for p in range(n_pages):
        rs = slice(p * page_rows, (p + 1) * page_rows)
        k_all[rs, :] = k_pages[p][0].astype(BF16)
        v_all[rs, :] = v_pages[p][0].astype(BF16)
    q = q_ref[0]
    nrow = q.shape[0]
    contract_last = (((1,), (1,)), ((), ()))
    s_past = lax.dot_general(q, k_all[...], contract_last, preferred_element_type=F32)
    s_new = lax.dot_general(q, kn_ref[0], contract_last, preferred_element_type=F32)

    def row_col(shape):
        row = lax.broadcasted_iota(jnp.int32, shape, 0)
        col = lax.broadcasted_iota(jnp.int32, shape, 1)
        return (row // n_new) % DA_HEADS, row % n_new, col % DA_HEADS, col // DA_HEADS

    q_head, _, k_head, _ = row_col(s_past.shape)
    s_past = jnp.where(q_head == k_head, s_past, NEG_INF)
    q_head, q_tok, k_head, k_tok = row_col(s_new.shape)
    s_new = jnp.where(q_head == k_head, jnp.where(k_tok <= q_tok, s_new, NEG_INF), NEG_INF)
    m = jnp.maximum(jnp.max(s_past, axis=-1, keepdims=True), jnp.max(s_new, axis=-1, keepdims=True))
    p_past = jnp.exp(s_past - m)
    p_new = jnp.exp(s_new - m)
    inv = 1.0 / (jnp.sum(p_past, axis=-1, keepdims=True) + jnp.sum(p_new, axis=-1, keepdims=True))
    half = nrow // 2
    lam = lam_ref[0]
    a_past = (p_past[:half] * inv[:half] - lam * (p_past[half:] * inv[half:])).astype(BF16)
    a_new = (p_new[:half] * inv[:half] - lam * (p_new[half:] * inv[half:])).astype(BF16)
    o_ref[0] = (jnp.dot(a_past, v_all[...], preferred_element_type=F32)
                + jnp.dot(a_new, vn_ref[0], preferred_element_type=F32))


def _decode_attention(page_table, lam, q_rows, k_new, v_new, cache_k, cache_v):
    nb, n_pages = page_table.shape
    n_new = k_new.shape[1] // DA_HEADS
    page_rows = PAGE_SIZE * DA_HEADS
    page_spec = lambda p: pl.BlockSpec((1, page_rows, HEAD_LANES), lambda b, pt: (pt[b * n_pages + p], 0, 0))
    per_b = lambda r: pl.BlockSpec((1, r, HEAD_LANES), lambda b, pt: (b, 0, 0))
    grid_spec = pltpu.PrefetchScalarGridSpec(
        num_scalar_prefetch=1,
        grid=(nb,),
        in_specs=[pl.BlockSpec(memory_space=pltpu.SMEM), per_b(q_rows.shape[1]),
                  per_b(n_new * DA_HEADS), per_b(n_new * DA_HEADS)]
                 + [page_spec(p) for p in range(n_pages)] * 2,
        out_specs=per_b(n_new * DA_HEADS),
        scratch_shapes=[pltpu.VMEM((n_pages * page_rows, HEAD_LANES), BF16)] * 2,
    )
    return pl.pallas_call(
        functools.partial(_decode_attn_kernel, n_pages=n_pages, n_new=n_new),
        grid_spec=grid_spec,
        out_shape=jax.ShapeDtypeStruct((nb, n_new * DA_HEADS, HEAD_LANES), F32),
        compiler_params=_params("arbitrary"),
        name="decode_attn",
    )(page_table.reshape(-1), lam, q_rows, k_new, v_new,
      *([cache_k] * n_pages), *([cache_v] * n_pages))


def _merge_kernel(ao_ref, gmo_ref, x_ref, sg_ref, wo_ref, g2_ref, wpq_ref, kbig_ref,
                  h_ref, tn_ref, s_ref, *, attn_scale):
    parts = []
    for b in range(DA_HEADS):
        sl = slice(b * HEAD_LANES, (b + 1) * HEAD_LANES)
        parts.append((_rms(ao_ref[:, sl], sg_ref[...]) * attn_scale).astype(BF16))
    cat = jnp.concatenate(parts + [gmo_ref[...]], axis=1)
    h = x_ref[...] + jnp.dot(cat, wo_ref[...], preferred_element_type=F32)
    h_ref[...] = h
    tn = _rms(h, g2_ref[...]).astype(BF16)
    tn_ref[...] = tn
    qp = jnp.dot(tn, wpq_ref[...], preferred_element_type=F32).astype(BF16)
    width = PEER_HEADS * PEER_HALF
    for c in range(2):
        s_ref[c] = lax.dot_general(kbig_ref[c], qp[:, c * width:(c + 1) * width],
                                   (((1,), (1,)), ((), ())), preferred_element_type=F32)


def _merge(ao, gmo, x, subln_g, w_o, g2, wpq, kbig, attn_scale):
    rows = x.shape[0]
    tile = min(ROW_TILE, rows)
    width = PEER_HEADS * PEER_HALF
    row_spec = lambda w: pl.BlockSpec((tile, w), lambda i: (i, 0))
    return pl.pallas_call(
        functools.partial(_merge_kernel, attn_scale=attn_scale),
        grid=(rows // tile,),
        in_specs=[row_spec(DA_WIDTH), row_spec(GM_WIDTH), row_spec(D_MODEL), _const_spec((1, HEAD_LANES)),
                  _const_spec((D_MODEL, D_MODEL)), _const_spec((1, D_MODEL)),
                  _const_spec((D_MODEL, 2 * width)), _const_spec((2, PEER_N_KEYS * PEER_HEADS, width))],
        out_specs=[row_spec(D_MODEL), row_spec(D_MODEL),
                   pl.BlockSpec((2, PEER_N_KEYS * PEER_HEADS, tile), lambda i: (0, 0, i))],
        out_shape=[jax.ShapeDtypeStruct((rows, D_MODEL), F32),
                   jax.ShapeDtypeStruct((rows, D_MODEL), BF16),
                   jax.ShapeDtypeStruct((2, PEER_N_KEYS * PEER_HEADS, rows), F32)],
        compiler_params=_params("parallel"),
        name="merge",
    )(ao, gmo, x, subln_g, w_o, g2, wpq, kbig)


def _young_candidates():
    return [(p, q) for p in range(PEER_TOPK) for q in range(PEER_TOPK) if (p + 1) * (q + 1) <= PEER_TOPK]


def _topk_kernel(s_ref, r2_ref, e2_ref, n_ref, c_ref, work_ref, top_ref, idx_ref, young_ref,
                 rank_ref, exp_ref):
    nk = PEER_N_KEYS
    tile = s_ref.shape[2:]
    one, zero = jnp.ones(tile, F32), jnp.zeros(tile, F32)

    def extract_distinct(c):
        def round_(r, bound):
            tree = [jnp.where(s_ref[c, j] < bound, s_ref[c, j], NEG_INF) for j in range(nk)]
            while len(tree) > 1:
                tree = [jnp.maximum(tree[t], tree[t + 1]) for t in range(0, len(tree), 2)]
            top_ref[c, r] = tree[0]
            return tree[0]

        last = lax.fori_loop(0, PEER_TOPK, round_, jnp.full(tile, float("inf"), F32))
        reached = zero
        for j in range(nk):
            reached = reached + jnp.where(s_ref[c, j] >= last, one, zero)
        return reached

    def extract(c):
        for j in range(nk):
            work_ref[j] = s_ref[c, j]

        def round_(r, _):
            vals = [work_ref[j] for j in range(nk)]
            idxs = [None] * nk
            width = 1
            while len(vals) > 1:
                nv, ni = [], []
                for t in range(0, len(vals), 2):
                    a, b = vals[t], vals[t + 1]
                    take_b = b > a
                    ia = idxs[t] if idxs[t] is not None else float(t * width)
                    ib = idxs[t + 1] if idxs[t + 1] is not None else float((t + 1) * width)
                    nv.append(jnp.maximum(a, b))
                    ni.append(jnp.where(take_b, ib, ia))
                vals, idxs = nv, ni
                width *= 2
            best, best_idx = vals[0], idxs[0]
            top_ref[c, r] = best
            idx_ref[c, r] = best_idx
            for j in range(nk):
                work_ref[j] = jnp.where(best_idx == float(j), NEG_INF, work_ref[j])
            return 0

        lax.fori_loop(0, PEER_TOPK, round_, 0)

    miscount = jnp.maximum(jnp.abs(extract_distinct(0) - float(PEER_TOPK)),
                           jnp.abs(extract_distinct(1) - float(PEER_TOPK)))
    tied = jnp.max(miscount) > 0.0

    @pl.when(tied)
    def _():
        extract(0)
        extract(1)

    a = [top_ref[0, p] for p in range(PEER_TOPK)]
    b = [top_ref[1, q] for q in range(PEER_TOPK)]
    cands = _young_candidates()
    cand = {pq: a[pq[0]] + b[pq[1]] for pq in cands}
    n_of_p = [zero] * PEER_TOPK
    ea = [jnp.exp(a[p] - a[0]) for p in range(PEER_TOPK)]
    eb = [jnp.exp(b[q] - b[0]) for q in range(PEER_TOPK)]
    z = zero
    for (p, q) in cands:
        fixed = 0
        beaten = zero
        for (p2, q2) in cands:
            if (p2, q2) == (p, q):
                continue
            if p2 <= p and q2 <= q:
                fixed += 1
            elif p2 >= p and q2 >= q:
                continue
            elif p2 * PEER_TOPK + q2 < p * PEER_TOPK + q:
                beaten = beaten + jnp.where(cand[(p2, q2)] >= cand[(p, q)], one, zero)
            else:
                beaten = beaten + jnp.where(cand[(p2, q2)] > cand[(p, q)], one, zero)
        sel = jnp.where(beaten + float(fixed) < float(PEER_TOPK), one, zero)
        n_of_p[p] = n_of_p[p] + sel
        z = z + sel * (ea[p] * eb[q])
    inv_z = 1.0 / z
    for p in range(PEER_TOPK):
        young_ref[p] = n_of_p[p]

    def scatter(hit_a, hit_b):
        for j in range(nk):
            n_j = zero
            r_j = jnp.full(tile, float(nk - 1), F32)
            for p in range(PEER_TOPK):
                n_j = jnp.where(hit_a(p, j), young_ref[p], n_j)
                r_j = jnp.where(hit_b(p, j), float(p), r_j)
            n_ref[j] = n_j
            rank_ref[j * PEER_HEADS:(j + 1) * PEER_HEADS, :] = r_j

    @pl.when(jnp.logical_not(tied))
    def _():
        scatter(lambda p, j: top_ref[0, p] == s_ref[0, j], lambda p, j: top_ref[1, p] == s_ref[1, j])

    @pl.when(tied)
    def _():
        scatter(lambda p, j: idx_ref[0, p] == float(j), lambda p, j: idx_ref[1, p] == float(j))

    for j in range(nk):
        c_ref[j] = jnp.exp(s_ref[0, j] - a[0]) * inv_z
        exp_ref[j * PEER_HEADS:(j + 1) * PEER_HEADS, :] = jnp.exp(s_ref[1, j] - b[0])
    for h in range(PEER_HEADS):
        r2_ref[h] = rank_ref[pl.ds(h, nk, stride=PEER_HEADS), :].astype(BF16)
        e2_ref[h] = exp_ref[pl.ds(h, nk, stride=PEER_HEADS), :].astype(BF16)


def _topk(scores):
    tokens = scores.shape[-1]
    lanes = TOPK_LANES
    tab2 = jax.ShapeDtypeStruct((PEER_HEADS, PEER_N_KEYS, tokens), BF16)
    tab2_spec = pl.BlockSpec((PEER_HEADS, PEER_N_KEYS, lanes), lambda i: (0, 0, i))
    tab1 = jax.ShapeDtypeStruct((tokens // lanes, PEER_N_KEYS, PEER_HEADS, lanes), F32)
    tab1_spec = pl.BlockSpec((None, PEER_N_KEYS, PEER_HEADS, lanes), lambda i: (i, 0, 0, 0))
    return pl.pallas_call(
        _topk_kernel,
        grid=(tokens // lanes,),
        in_specs=[pl.BlockSpec((2, PEER_N_KEYS, PEER_HEADS, lanes), lambda i: (0, 0, 0, i))],
        out_specs=[tab2_spec] * 2 + [tab1_spec] * 2,
        out_shape=[tab2] * 2 + [tab1] * 2,
        scratch_shapes=[pltpu.VMEM((PEER_N_KEYS, PEER_HEADS, lanes), F32),
                        pltpu.VMEM((2, PEER_TOPK, PEER_HEADS, lanes), F32),
                        pltpu.VMEM((2, PEER_TOPK, PEER_HEADS, lanes), F32),
                        pltpu.VMEM((PEER_TOPK, PEER_HEADS, lanes), F32),
                        pltpu.VMEM((PEER_N_KEYS * PEER_HEADS, lanes), F32),
                        pltpu.VMEM((PEER_N_KEYS * PEER_HEADS, lanes), F32)],
        compiler_params=_params("parallel"),
        name="peer_topk",
    )(scores)


def _peer_kernel(tn_ref, pu_ref, pvt_ref, r2_ref, e2_ref, n_ref, c_ref, acc_ref,
                 at_ref, gate_ref, hid_ref, tab_ref):
    e = pl.program_id(1)
    tt = tn_ref.shape[0]
    nk = PEER_N_KEYS
    keys_per_step = pu_ref.shape[0] // nk
    lane_tiles = [(t, slice(t * HEAD_LANES, (t + 1) * HEAD_LANES)) for t in range(tt // HEAD_LANES)]
    pack = BF16_TILE_ROWS

    @pl.when(e == 0)
    def _():
        tab_ref[:, 0, :nk, :tt] = r2_ref[...]
        tab_ref[:, 1, :nk, :tt] = e2_ref[...]
        acc_ref[...] = jnp.zeros_like(acc_ref)

    at_ref[...] = lax.dot_general(pu_ref[...], tn_ref[...], (((1,), (1,)), ((), ())),
                                  preferred_element_type=F32).astype(BF16)
    for ii in range(keys_per_step):
        for t, lanes in lane_tiles:
            gate = jnp.zeros((nk, HEAD_LANES), BF16)
            for h in range(PEER_HEADS):
                row = ii * PEER_HEADS + h
                n = jnp.broadcast_to(n_ref[t, row:row + 1, :], (pack, HEAD_LANES)).astype(BF16)
                c = jnp.broadcast_to(c_ref[t, row:row + 1, :], (pack, HEAD_LANES)).astype(BF16)
                n = jnp.tile(n, (nk // pack, 1))
                c = jnp.tile(c, (nk // pack, 1))
                gate = gate + jnp.where(tab_ref[h, 0, :nk, lanes] < n, tab_ref[h, 1, :nk, lanes] * c,
                                        jnp.zeros_like(gate))
            gate_ref[ii * nk:(ii + 1) * nk, lanes] = gate
    hid_ref[...] = gate_ref[:, :tt] * _gelu(at_ref[...])

    @pl.when(e < pl.num_programs(1))
    def _():
        acc_ref[...] += jnp.dot(pvt_ref[...], hid_ref[...], preferred_element_type=F32)


def _peer(tn, pu, pvt, r2, e2, ntab, ctab):
    tokens = tn.shape[0]
    tt = min(PEER_TOKEN_TILE, tokens)
    et = PEER_EXPERT_TILE
    rows_per_step = (et // PEER_N_KEYS) * PEER_HEADS
    whole_tab = pl.BlockSpec((PEER_HEADS, PEER_N_KEYS, tt), lambda t, e: (0, 0, t))
    step_tab = pl.BlockSpec((tt // HEAD_LANES, rows_per_step, HEAD_LANES), lambda t, e: (t, e, 0))
    return pl.pallas_call(
        _peer_kernel,
        grid=(tokens // tt, PEER_N_EXPERTS // et),
        in_specs=[pl.BlockSpec((tt, D_MODEL), lambda t, e: (t, 0)),
                  pl.BlockSpec((et, D_MODEL), lambda t, e: (e, 0)),
                  pl.BlockSpec((D_MODEL, et), lambda t, e: (0, e)),
                  whole_tab, whole_tab, step_tab, step_tab],
        out_specs=pl.BlockSpec((D_MODEL, tt), lambda t, e: (0, t)),
        out_shape=jax.ShapeDtypeStruct((D_MODEL, tokens), F32),
        scratch_shapes=[pltpu.VMEM((et, tt), BF16), pltpu.VMEM((et, tt + HEAD_LANES), BF16),
                        pltpu.VMEM((et, tt), BF16),
                        pltpu.VMEM((PEER_HEADS, 2, PEER_N_KEYS + 2 * BF16_TILE_ROWS, tt + HEAD_LANES), BF16)],
        compiler_params=_params("parallel", "arbitrary"),
        name="peer_experts",
    )(tn, pu, pvt, r2, e2, ntab, ctab)


def _final_kernel(h_ref, mix_ref, gf_ref, y_ref):
    y_ref[...] = _rms(h_ref[...] + jnp.transpose(mix_ref[...]), gf_ref[...])


def _final(h, mix_t, final_g):
    rows = h.shape[0]
    tile = min(ROW_TILE, rows)
    row_spec = pl.BlockSpec((tile, D_MODEL), lambda i: (i, 0))
    return pl.pallas_call(
        _final_kernel,
        grid=(rows // tile,),
        in_specs=[row_spec, pl.BlockSpec((D_MODEL, tile), lambda i: (0, i)), _const_spec((1, D_MODEL))],
        out_specs=row_spec,
        out_shape=jax.ShapeDtypeStruct((rows, D_MODEL), F32),
        compiler_params=_params("parallel"),
        name="final_norm",
    )(h, mix_t, final_g)


def _rope_tables(pos):
    half = ROPE_DIM // 2
    inv = ROPE_THETA ** (-jnp.arange(half, dtype=F32) * 2.0 / ROPE_DIM)
    d = jnp.arange(HEAD_LANES) % DA_HEAD_DIM
    ang = pos[:, None] * inv[d % half][None, :]
    cos_t = jnp.where(d < ROPE_DIM, jnp.cos(ang), 1.0)
    sin_a = jnp.where(d < half, -jnp.sin(ang), 0.0)
    sin_b = jnp.where((d >= half) & (d < ROPE_DIM), jnp.sin(ang), 0.0)
    return cos_t, sin_a, sin_b


def _row_pipeline(x, rope_tabs, wmix, bias, attend, weights):
    (g1, w_in, gm_g, subln_g, w_o, g2, wpq, kbig, pu, pvt, final_g, attn_scale) = weights
    q1, q2, k32, v32, kb, vb, gmo, gn = _proj(x, g1, w_in, *rope_tabs, gm_g, wmix, bias)
    ao = attend(q1, q2, k32, v32, kb, vb)
    h, tn, scores = _merge(ao, gmo, x, subln_g, w_o, g2, wpq, kbig, attn_scale)
    rows = x.shape[0]
    r2, e2, ntab, ctab = _topk(scores.reshape(2, PEER_N_KEYS, PEER_HEADS, rows))
    flat = lambda t: t.reshape(rows // TOPK_LANES, PEER_N_KEYS * PEER_HEADS, TOPK_LANES)
    y = _final(h, _peer(tn, pu, pvt, r2, e2, flat(ntab), flat(ctab)), final_g)
    return y, k32, v32, gn


def kernel(x_prompt, x_sample, cache_k, cache_v, page_table, norm1_g, w_in, gm_norm_g, gm_ws, gm_b,
           da_lq1, da_lk1, da_lq2, da_lk2, da_subln_g, w_o, norm2_g, peer_wq, peer_keys, peer_u,
           peer_v, final_g):
    batch, seq, _ = x_prompt.shape
    nb, n_new, _ = x_sample.shape
    n_pages = page_table.shape[1]
    past = n_pages * PAGE_SIZE
    assert w_in.shape[0] == 1 and seq % CHUNK == 0 and seq % min(ROW_TILE, batch * seq) == 0

    lam_init = 0.8 - 0.6 * math.exp(-0.3 * 0)
    dots = lambda a, b: jnp.exp(jnp.sum(a.astype(F32) * b.astype(F32)))
    lam = (dots(da_lq1[0], da_lk1[0]) - dots(da_lq2[0], da_lk2[0]) + lam_init).reshape(1).astype(F32)

    eye = jnp.eye(PEER_HEADS, dtype=F32)
    kbig = jnp.einsum("hcjd,hg->cjhgd", peer_keys[0], eye).reshape(
        2, PEER_N_KEYS * PEER_HEADS, PEER_HEADS * PEER_HALF).astype(BF16)
    wpq = peer_wq[0].reshape(D_MODEL, PEER_HEADS, 2, PEER_HALF).transpose(0, 2, 1, 3).reshape(
        D_MODEL, 2 * PEER_HEADS * PEER_HALF).astype(BF16)
    weights = (norm1_g[0][None], w_in[0].astype(BF16), gm_norm_g[0].reshape(1, GM_WIDTH),
               da_subln_g[0][None], w_o[0].astype(BF16), norm2_g[0][None], wpq, kbig,
               peer_u[0].astype(BF16), peer_v[0].T.astype(BF16), final_g[None], 1.0 - lam_init)

    xp = x_prompt.reshape(batch * seq, D_MODEL)
    tabs_p = _rope_tables(jnp.arange(seq, dtype=F32))
    wmix_p = jnp.tril(gm_ws[0]).astype(BF16)
    bias_p = jnp.repeat(jnp.transpose(gm_b[0]), HEAD_LANES, axis=1)
    attend_p = lambda q1, q2, k32, v32, kb, vb: _prompt_attention(lam, q1, q2, kb, vb, batch, seq)
    y_p, k_p, v_p, _ = _row_pipeline(xp, tabs_p, wmix_p, bias_p, attend_p, weights)

    xs = x_sample.reshape(nb * n_new, D_MODEL)
    tabs_s = _rope_tables(jnp.tile(past + jnp.arange(n_new, dtype=F32), nb))
    rows_s = min(ROW_TILE, nb * n_new)
    w_new = jnp.tril(gm_ws[0][:, :n_new, :n_new])
    r = jnp.arange(rows_s)
    pick = (r[:, None] % n_new == jnp.arange(n_new)[None, :]).astype(F32)
    w_rows = jnp.einsum("ri,hij,cj->hrc", pick, w_new, pick, precision=lax.Precision.HIGHEST)
    wmix_s = jnp.where(r[:, None] // n_new == r[None, :] // n_new, w_rows, 0.0).astype(BF16)
    bias_s = jnp.tile(jnp.repeat(jnp.transpose(gm_b[0][:, :n_new]), HEAD_LANES, axis=1), (rows_s // n_new, 1))
    ck = cache_k[0].reshape(-1, PAGE_SIZE * DA_HEADS, HEAD_LANES)
    cv = cache_v[0].reshape(-1, PAGE_SIZE * DA_HEADS, HEAD_LANES)

    def attend_s(q1, q2, k32, v32, kb, vb):
        per_head = lambda t: jnp.transpose(t.reshape(nb, n_new, DA_HEADS, HEAD_LANES), (0, 2, 1, 3))
        q = jnp.stack([per_head(q1), per_head(q2)], axis=1).reshape(nb, 2 * DA_HEADS * n_new, HEAD_LANES)
        new_rows = lambda t: t.reshape(nb, n_new * DA_HEADS, HEAD_LANES).astype(BF16)
        ao = _decode_attention(page_table, lam, q, new_rows(k32), new_rows(v32), ck, cv)
        ao = jnp.transpose(ao.reshape(nb, DA_HEADS, n_new, HEAD_LANES), (0, 2, 1, 3))
        return ao.reshape(nb * n_new, DA_WIDTH)

    y_s, k_s, v_s, gn_s = _row_pipeline(xs, tabs_s, wmix_s, bias_s, attend_s, weights)

    head_shape = lambda b, s: (1, b, s, DA_HEADS, HEAD_LANES)
    return (y_p.reshape(batch, seq, D_MODEL), y_s.reshape(nb, n_new, D_MODEL),
            k_p.reshape(head_shape(batch, seq)), v_p.reshape(head_shape(batch, seq)),
            k_s.reshape(head_shape(nb, n_new)), v_s.reshape(head_shape(nb, n_new)),
            gn_s.reshape(1, nb, n_new, GM_HEADS, HEAD_LANES))
```

```python
import functools
import math

import jax
import jax.numpy as jnp
from jax import lax
from jax.experimental import pallas as pl
from jax.experimental.pallas import tpu as pltpu

F32 = jnp.float32
BF16 = jnp.bfloat16

D_MODEL = 1024
DA_WIDTH = 512
DA_HEADS = 4
HEAD_LANES = 128
BF16_TILE_ROWS = 16
DA_HEAD_DIM = 64
GM_WIDTH = 512
GM_HEADS = 4
CHUNK = 128
IN_WIDTH = 3 * DA_WIDTH + 2 * GM_WIDTH
ROPE_THETA = 500000.0
ROPE_DIM = 16
PAGE_SIZE = 128
PEER_HEADS = 8
PEER_N_KEYS = 128
PEER_N_EXPERTS = PEER_N_KEYS * PEER_N_KEYS
PEER_HALF = 128
PEER_TOPK = 16
NORM_EPS = 1e-6
NEG_INF = float("-inf")

VMEM_LIMIT_BYTES = 56 * 1024 * 1024

ROW_TILE = 512
ATTN_Q_TILE = 256
ATTN_HEADS_PER_STEP = 2
TOPK_LANES = 128
PEER_EXPERT_TILE = 1024
PEER_TOKEN_TILE = 1024


def _params(*semantics):
    return pltpu.CompilerParams(dimension_semantics=semantics, vmem_limit_bytes=VMEM_LIMIT_BYTES)


def _rms(x, g):
    return x * lax.rsqrt(jnp.mean(x * x, axis=-1, keepdims=True) + NORM_EPS) * g


def _gelu(x):
    return x * (0.5 * (1.0 + jnp.tanh(math.sqrt(2.0 / math.pi) * (x + 0.044715 * (x * x * x)))))


def _const_spec(shape):
    zeros = (0,) * len(shape)
    return pl.BlockSpec(shape, lambda *_: zeros)


def _proj_kernel(x_ref, g1_ref, win_ref, cos_ref, sa_ref, sb_ref, gmg_ref, wmix_ref, bias_ref,
                 q1_ref, q2_ref, k32_ref, v32_ref, kb_ref, vb_ref, gmo_ref, gn_ref, *, mix_rows):
    rows = x_ref.shape[0]
    xn = _rms(x_ref[...], g1_ref[...]).astype(BF16)
    z = jnp.dot(xn, win_ref[...], preferred_element_type=F32)
    cos_t, sin_a, sin_b = cos_ref[...], sa_ref[...], sb_ref[...]
    first_half = lax.broadcasted_iota(jnp.int32, (rows, HEAD_LANES), 1) < DA_HEAD_DIM

    def rope(t):
        return t * cos_t + pltpu.roll(t, HEAD_LANES - 8, 1) * sin_a + pltpu.roll(t, 8, 1) * sin_b

    for b in range(DA_HEADS):
        sl = slice(b * HEAD_LANES, (b + 1) * HEAD_LANES)
        q = rope(z[:, sl]) * (1.0 / math.sqrt(DA_HEAD_DIM))
        q1_ref[:, sl] = jnp.where(first_half, q, 0.0).astype(BF16)
        q2_ref[:, sl] = jnp.where(first_half, 0.0, q).astype(BF16)
        k = rope(z[:, DA_WIDTH + b * HEAD_LANES:DA_WIDTH + (b + 1) * HEAD_LANES])
        v = z[:, 2 * DA_WIDTH + b * HEAD_LANES:2 * DA_WIDTH + (b + 1) * HEAD_LANES]
        k32_ref[pl.ds(b, rows, stride=DA_HEADS), :] = k
        v32_ref[pl.ds(b, rows, stride=DA_HEADS), :] = v
        kb_ref[:, sl] = k.astype(BF16)
        vb_ref[:, sl] = v.astype(BF16)

    u = _gelu(z[:, 3 * DA_WIDTH:3 * DA_WIDTH + GM_WIDTH])
    g = _gelu(z[:, 3 * DA_WIDTH + GM_WIDTH:])
    for b in range(GM_HEADS):
        sl = slice(b * HEAD_LANES, (b + 1) * HEAD_LANES)
        gn = _rms(g[:, sl], gmg_ref[:, sl])
        gn_ref[:, sl] = gn
        gnb = gn.astype(BF16)
        for r in range(rows // mix_rows):
            rs = slice(r * mix_rows, (r + 1) * mix_rows)
            s = jnp.dot(wmix_ref[b], gnb[rs], preferred_element_type=F32) + bias_ref[:, sl]
            gmo_ref[rs, sl] = (u[rs, sl] * s).astype(BF16)


def _proj(x, g1, w_in, cos_t, sin_a, sin_b, gm_g, wmix, bias):
    rows = x.shape[0]
    tile = min(ROW_TILE, rows)
    mix_rows = wmix.shape[-1]
    row_spec = lambda w: pl.BlockSpec((tile, w), lambda i: (i, 0))
    table_blocks = cos_t.shape[0] // tile
    table_spec = pl.BlockSpec((tile, HEAD_LANES), lambda i: (i % table_blocks, 0))
    head_rows_spec = pl.BlockSpec((tile * DA_HEADS, HEAD_LANES), lambda i: (i, 0))
    out_shape = [jax.ShapeDtypeStruct((rows, DA_WIDTH), BF16),
                 jax.ShapeDtypeStruct((rows, DA_WIDTH), BF16),
                 jax.ShapeDtypeStruct((rows * DA_HEADS, HEAD_LANES), F32),
                 jax.ShapeDtypeStruct((rows * DA_HEADS, HEAD_LANES), F32),
                 jax.ShapeDtypeStruct((rows, DA_WIDTH), BF16),
                 jax.ShapeDtypeStruct((rows, DA_WIDTH), BF16),
                 jax.ShapeDtypeStruct((rows, GM_WIDTH), BF16),
                 jax.ShapeDtypeStruct((rows, GM_WIDTH), F32)]
    return pl.pallas_call(
        functools.partial(_proj_kernel, mix_rows=mix_rows),
        grid=(rows // tile,),
        in_specs=[row_spec(D_MODEL), _const_spec((1, D_MODEL)), _const_spec((D_MODEL, IN_WIDTH)),
                  table_spec, table_spec, table_spec,
                  _const_spec((1, GM_WIDTH)), _const_spec(wmix.shape), _const_spec(bias.shape)],
        out_specs=[row_spec(DA_WIDTH)] * 2 + [head_rows_spec] * 2 + [row_spec(DA_WIDTH)] * 2
                  + [row_spec(GM_WIDTH)] * 2,
        out_shape=out_shape,
        compiler_params=_params("parallel"),
        name="proj",
    )(x, g1, w_in, cos_t, sin_a, sin_b, gm_g, wmix, bias)


def _attn_kernel(lam_ref, q1_ref, q2_ref, k_ref, v_ref, o_ref):
    tq = q1_ref.shape[0]
    heads = q1_ref.shape[1] // HEAD_LANES
    i = pl.program_id(2)
    head_lanes = [slice(h * HEAD_LANES, (h + 1) * HEAD_LANES) for h in range(heads)]
    qs = [jnp.concatenate([q1_ref[:, sl], q2_ref[:, sl]], axis=0) for sl in head_lanes]

    def step(j, carry, masked):
        start = pl.multiple_of(j * tq, tq)
        new = []
        for q, sl, (m, l, acc) in zip(qs, head_lanes, carry):
            kb = k_ref[pl.ds(start, tq), sl]
            vb = v_ref[pl.ds(start, tq), sl]
            s = lax.dot_general(q, kb, (((1,), (1,)), ((), ())), preferred_element_type=F32)
            if masked:
                row = lax.broadcasted_iota(jnp.int32, (2 * tq, tq), 0)
                col = lax.broadcasted_iota(jnp.int32, (2 * tq, tq), 1)
                row = jnp.where(row >= tq, row - tq, row)
                s = jnp.where(col <= row, s, NEG_INF)
            m_new = jnp.maximum(m, jnp.max(s, axis=-1, keepdims=True))
            alpha = jnp.exp(m - m_new)
            p = jnp.exp(s - m_new)
            l = alpha * l + jnp.sum(p, axis=-1, keepdims=True)
            acc = alpha * acc + jnp.dot(p.astype(BF16), vb, preferred_element_type=F32)
            new.append((m_new, l, acc))
        return tuple(new)

    init = tuple((jnp.full((2 * tq, 1), NEG_INF, F32), jnp.zeros((2 * tq, 1), F32),
                  jnp.zeros((2 * tq, HEAD_LANES), F32)) for _ in range(heads))
    carry = lax.fori_loop(0, i, lambda j, c: step(j, c, False), init)
    for sl, (_, l, acc) in zip(head_lanes, step(i, carry, True)):
        o = acc / l
        o_ref[:, sl] = o[:tq] - lam_ref[0] * o[tq:]


def _prompt_attention(lam, q1, q2, kb, vb, batch, seq):
    tq = min(ATTN_Q_TILE, seq)
    nq = seq // tq
    width = ATTN_HEADS_PER_STEP * HEAD_LANES
    q_spec = pl.BlockSpec((tq, width), lambda b, h, i: (b * nq + i, h))
    kv_spec = pl.BlockSpec((seq, width), lambda b, h, i: (b, h))
    return pl.pallas_call(
        _attn_kernel,
        grid=(batch, DA_HEADS // ATTN_HEADS_PER_STEP, nq),
        in_specs=[pl.BlockSpec(memory_space=pltpu.SMEM), q_spec, q_spec, kv_spec, kv_spec],
        out_specs=q_spec,
        out_shape=jax.ShapeDtypeStruct((batch * seq, DA_WIDTH), F32),
        compiler_params=_params("parallel", "parallel", "arbitrary"),
        name="prompt_attn",
    )(lam, q1, q2, kb, vb)


def _decode_attn_kernel(pt_ref, lam_ref, q_ref, kn_ref, vn_ref, *rest, n_pages, n_new):
    del pt_ref
    k_pages, v_pages = rest[:n_pages], rest[n_pages:2 * n_pages]
    o_ref, k_all, v_all = rest[2 * n_pages:]
    page_rows = PAGE_SIZE * DA_HEADS
    for p in range(n_pages):
        rs = slice(p * page_rows, (p + 1) * page_rows)
        k_all[rs, :] = k_pages[p][0].astype(BF16)
        v_all[rs, :] = v_pages[p][0].astype(BF16)
    q = q_ref[0]
    nrow = q.shape[0]
    contract_last = (((1,), (1,)), ((), ()))
    s_past = lax.dot_general(q, k_all[...], contract_last, preferred_element_type=F32)
    s_new = lax.dot_general(q, kn_ref[0], contract_last, preferred_element_type=F32)

    def row_col(shape):
        row = lax.broadcasted_iota(jnp.int32, shape, 0)
        col = lax.broadcasted_iota(jnp.int32, shape, 1)
        return (row // n_new) % DA_HEADS, row % n_new, col % DA_HEADS, col // DA_HEADS

    q_head, _, k_head, _ = row_col(s_past.shape)
    s_past = jnp.where(q_head == k_head, s_past, NEG_INF)
    q_head, q_tok, k_head, k_tok = row_col(s_new.shape)
    s_new = jnp.where(q_head == k_head, jnp.where(k_tok <= q_tok, s_new, NEG_INF), NEG_INF)
    m = jnp.maximum(jnp.max(s_past, axis=-1, keepdims=True), jnp.max(s_new, axis=-1, keepdims=True))
    p_past = jnp.exp(s_past - m)
    p_new = jnp.exp(s_new - m)
    inv = 1.0 / (jnp.sum(p_past, axis=-1, keepdims=True) + jnp.sum(p_new, axis=-1, keepdims=True))
    half = nrow // 2
    lam = lam_ref[0]
    a_past = (p_past[:half] * inv[:half] - lam * (p_past[half:] * inv[half:])).astype(BF16)
    a_new = (p_new[:half] * inv[:half] - lam * (p_new[half:] * inv[half:])).astype(BF16)
    o_ref[0] = (jnp.dot(a_past, v_all[...], preferred_element_type=F32)
                + jnp.dot(a_new, vn_ref[0], preferred_element_type=F32))


def _decode_attention(page_table, lam, q_rows, k_new, v_new, cache_k, cache_v):
    nb, n_pages = page_table.shape
    n_new = k_new.shape[1] // DA_HEADS
    page_rows = PAGE_SIZE * DA_HEADS
    page_spec = lambda p: pl.BlockSpec((1, page_rows, HEAD_LANES), lambda b, pt: (pt[b * n_pages + p], 0, 0))
    per_b = lambda r: pl.BlockSpec((1, r, HEAD_LANES), lambda b, pt: (b, 0, 0))
    grid_spec = pltpu.PrefetchScalarGridSpec(
        num_scalar_prefetch=1,
        grid=(nb,),
        in_specs=[pl.BlockSpec(memory_space=pltpu.SMEM), per_b(q_rows.shape[1]),
                  per_b(n_new * DA_HEADS), per_b(n_new * DA_HEADS)]
                 + [page_spec(p) for p in range(n_pages)] * 2,
        out_specs=per_b(n_new * DA_HEADS),
        scratch_shapes=[pltpu.VMEM((n_pages * page_rows, HEAD_LANES), BF16)] * 2,
    )
    return pl.pallas_call(
        functools.partial(_decode_attn_kernel, n_pages=n_pages, n_new=n_new),
        grid_spec=grid_spec,
        out_shape=jax.ShapeDtypeStruct((nb, n_new * DA_HEADS, HEAD_LANES), F32),
        compiler_params=_params("arbitrary"),
        name="decode_attn",
    )(page_table.reshape(-1), lam, q_rows, k_new, v_new,
      *([cache_k] * n_pages), *([cache_v] * n_pages))


def _merge_kernel(ao_ref, gmo_ref, x_ref, sg_ref, wo_ref, g2_ref, wpq_ref, kbig_ref,
                  h_ref, tn_ref, s_ref, *, attn_scale):
    parts = []
    for b in range(DA_HEADS):
        sl = slice(b * HEAD_LANES, (b + 1) * HEAD_LANES)
        parts.append((_rms(ao_ref[:, sl], sg_ref[...]) * attn_scale).astype(BF16))
    cat = jnp.concatenate(parts + [gmo_ref[...]], axis=1)
    h = x_ref[...] + jnp.dot(cat, wo_ref[...], preferred_element_type=F32)
    h_ref[...] = h
    tn = _rms(h, g2_ref[...]).astype(BF16)
    tn_ref[...] = tn
    qp = jnp.dot(tn, wpq_ref[...], preferred_element_type=F32).astype(BF16)
    width = PEER_HEADS * PEER_HALF
    for c in range(2):
        s_ref[c] = lax.dot_general(kbig_ref[c], qp[:, c * width:(c + 1) * width],
                                   (((1,), (1,)), ((), ())), preferred_element_type=F32)


def _merge(ao, gmo, x, subln_g, w_o, g2, wpq, kbig, attn_scale):
    rows = x.shape[0]
    tile = min(ROW_TILE, rows)
    width = PEER_HEADS * PEER_HALF
    row_spec = lambda w: pl.BlockSpec((tile, w), lambda i: (i, 0))
    return pl.pallas_call(
        functools.partial(_merge_kernel, attn_scale=attn_scale),
        grid=(rows // tile,),
        in_specs=[row_spec(DA_WIDTH), row_spec(GM_WIDTH), row_spec(D_MODEL), _const_spec((1, HEAD_LANES)),
                  _const_spec((D_MODEL, D_MODEL)), _const_spec((1, D_MODEL)),
                  _const_spec((D_MODEL, 2 * width)), _const_spec((2, PEER_N_KEYS * PEER_HEADS, width))],
        out_specs=[row_spec(D_MODEL), row_spec(D_MODEL),
                   pl.BlockSpec((2, PEER_N_KEYS * PEER_HEADS, tile), lambda i: (0, 0, i))],
        out_shape=[jax.ShapeDtypeStruct((rows, D_MODEL), F32),
                   jax.ShapeDtypeStruct((rows, D_MODEL), BF16),
                   jax.ShapeDtypeStruct((2, PEER_N_KEYS * PEER_HEADS, rows), F32)],
        compiler_params=_params("parallel"),
        name="merge",
    )(ao, gmo, x, subln_g, w_o, g2, wpq, kbig)


def _young_candidates():
    return [(p, q) for p in range(PEER_TOPK) for q in range(PEER_TOPK) if (p + 1) * (q + 1) <= PEER_TOPK]


def _topk_kernel(s_ref, r2_ref, e2_ref, n_ref, c_ref, work_ref, top_ref, idx_ref, young_ref,
                 rank_ref, exp_ref):
    nk = PEER_N_KEYS
    keys_per_block = n_ref.shape[1]
    tile = s_ref.shape[2:]
    one, zero = jnp.ones(tile, F32), jnp.zeros(tile, F32)

    def extract_distinct(c):
        def round_(r, bound):
            tree = [jnp.where(s_ref[c, j] < bound, s_ref[c, j], NEG_INF) for j in range(nk)]
            while len(tree) > 1:
                tree = [jnp.maximum(tree[t], tree[t + 1]) for t in range(0, len(tree), 2)]
            top_ref[c, r] = tree[0]
            return tree[0]

        last = lax.fori_loop(0, PEER_TOPK, round_, jnp.full(tile, float("inf"), F32))
        reached = zero
        for j in range(nk):
            reached = reached + jnp.where(s_ref[c, j] >= last, one, zero)
        return reached

    def extract(c):
        for j in range(nk):
            work_ref[j] = s_ref[c, j]

        def round_(r, _):
            vals = [work_ref[j] for j in range(nk)]
            idxs = [None] * nk
            width = 1
            while len(vals) > 1:
                nv, ni = [], []
                for t in range(0, len(vals), 2):
                    a, b = vals[t], vals[t + 1]
                    take_b = b > a
                    ia = idxs[t] if idxs[t] is not None else float(t * width)
                    ib = idxs[t + 1] if idxs[t + 1] is not None else float((t + 1) * width)
                    nv.append(jnp.maximum(a, b))
                    ni.append(jnp.where(take_b, ib, ia))
                vals, idxs = nv, ni
                width *= 2
            best, best_idx = vals[0], idxs[0]
            top_ref[c, r] = best
            idx_ref[c, r] = best_idx
            for j in range(nk):
                work_ref[j] = jnp.where(best_idx == float(j), NEG_INF, work_ref[j])
            return 0

        lax.fori_loop(0, PEER_TOPK, round_, 0)

    miscount = jnp.maximum(jnp.abs(extract_distinct(0) - float(PEER_TOPK)),
                           jnp.abs(extract_distinct(1) - float(PEER_TOPK)))
    tied = jnp.max(miscount) > 0.0

    @pl.when(tied)
    def _():
        extract(0)
        extract(1)

    a = [top_ref[0, p] for p in range(PEER_TOPK)]
    b = [top_ref[1, q] for q in range(PEER_TOPK)]
    cands = _young_candidates()
    cand = {pq: a[pq[0]] + b[pq[1]] for pq in cands}
    n_of_p = [zero] * PEER_TOPK
    ea = [jnp.exp(a[p] - a[0]) for p in range(PEER_TOPK)]
    eb = [jnp.exp(b[q] - b[0]) for q in range(PEER_TOPK)]
    z = zero
    for (p, q) in cands:
        fixed = 0
        beaten = zero
        for (p2, q2) in cands:
            if (p2, q2) == (p, q):
                continue
            if p2 <= p and q2 <= q:
                fixed += 1
            elif p2 >= p and q2 >= q:
                continue
            elif p2 * PEER_TOPK + q2 < p * PEER_TOPK + q:
                beaten = beaten + jnp.where(cand[(p2, q2)] >= cand[(p, q)], one, zero)
            else:
                beaten = beaten + jnp.where(cand[(p2, q2)] > cand[(p, q)], one, zero)
        sel = jnp.where(beaten + float(fixed) < float(PEER_TOPK), one, zero)
        n_of_p[p] = n_of_p[p] + sel
        z = z + sel * (ea[p] * eb[q])
    inv_z = 1.0 / z
    for p in range(PEER_TOPK):
        young_ref[p] = n_of_p[p]

    def scatter(hit_a, hit_b):
        for j in range(nk):
            n_j = zero
            r_j = jnp.full(tile, float(nk - 1), F32)
            for p in range(PEER_TOPK):
                n_j = jnp.where(hit_a(p, j), young_ref[p], n_j)
                r_j = jnp.where(hit_b(p, j), float(p), r_j)
            n_ref[j // keys_per_block, j % keys_per_block] = n_j
            rank_ref[j * PEER_HEADS:(j + 1) * PEER_HEADS, :] = r_j

    @pl.when(jnp.logical_not(tied))
    def _():
        scatter(lambda p, j: top_ref[0, p] == s_ref[0, j], lambda p, j: top_ref[1, p] == s_ref[1, j])

    @pl.when(tied)
    def _():
        scatter(lambda p, j: idx_ref[0, p] == float(j), lambda p, j: idx_ref[1, p] == float(j))

    for j in range(nk):
        c_ref[j // keys_per_block, j % keys_per_block] = jnp.exp(s_ref[0, j] - a[0]) * inv_z
        exp_ref[j * PEER_HEADS:(j + 1) * PEER_HEADS, :] = jnp.exp(s_ref[1, j] - b[0])
    for h in range(PEER_HEADS):
        r2_ref[h] = rank_ref[pl.ds(h, nk, stride=PEER_HEADS), :].astype(BF16)
        e2_ref[h] = exp_ref[pl.ds(h, nk, stride=PEER_HEADS), :].astype(BF16)


def _topk(scores):
    tokens = scores.shape[-1]
    lanes = TOPK_LANES
    lane_tiles = min(PEER_TOKEN_TILE, tokens) // lanes
    keys_per_block = PEER_EXPERT_TILE // PEER_N_KEYS
    n_blocks = PEER_N_KEYS // keys_per_block
    tab2 = jax.ShapeDtypeStruct((PEER_HEADS, PEER_N_KEYS, tokens), BF16)
    tab2_spec = pl.BlockSpec((PEER_HEADS, PEER_N_KEYS, lanes), lambda i: (0, 0, i))
    tab1 = jax.ShapeDtypeStruct((tokens // (lane_tiles * lanes), n_blocks, lane_tiles, keys_per_block,
                                 PEER_HEADS, lanes), F32)
    tab1_spec = pl.BlockSpec((None, n_blocks, None, keys_per_block, PEER_HEADS, lanes),
                             lambda i: (i // lane_tiles, 0, i % lane_tiles, 0, 0, 0))
    return pl.pallas_call(
        _topk_kernel,
        grid=(tokens // lanes,),
        in_specs=[pl.BlockSpec((2, PEER_N_KEYS, PEER_HEADS, lanes), lambda i: (0, 0, 0, i))],
        out_specs=[tab2_spec] * 2 + [tab1_spec] * 2,
        out_shape=[tab2] * 2 + [tab1] * 2,
        scratch_shapes=[pltpu.VMEM((PEER_N_KEYS, PEER_HEADS, lanes), F32),
                        pltpu.VMEM((2, PEER_TOPK, PEER_HEADS, lanes), F32),
                        pltpu.VMEM((2, PEER_TOPK, PEER_HEADS, lanes), F32),
                        pltpu.VMEM((PEER_TOPK, PEER_HEADS, lanes), F32),
                        pltpu.VMEM((PEER_N_KEYS * PEER_HEADS, lanes), F32),
                        pltpu.VMEM((PEER_N_KEYS * PEER_HEADS, lanes), F32)],
        compiler_params=_params("parallel"),
        name="peer_topk",
    )(scores)


def _peer_kernel(tn_ref, pu_ref, pvt_ref, r2_ref, e2_ref, n_ref, c_ref, acc_ref,
                 at_ref, gate_ref, hid_ref, tab_ref):
    e = pl.program_id(1)
    tt = tn_ref.shape[0]
    nk = PEER_N_KEYS
    keys_per_step = pu_ref.shape[0] // nk
    lane_tiles = [(t, slice(t * HEAD_LANES, (t + 1) * HEAD_LANES)) for t in range(tt // HEAD_LANES)]
    pack = BF16_TILE_ROWS

    @pl.when(e == 0)
    def _():
        tab_ref[:, 0, :nk, :tt] = r2_ref[...]
        tab_ref[:, 1, :nk, :tt] = e2_ref[...]
        acc_ref[...] = jnp.zeros_like(acc_ref)

    at_ref[...] = lax.dot_general(pu_ref[...], tn_ref[...], (((1,), (1,)), ((), ())),
                                  preferred_element_type=F32).astype(BF16)
    for ii in range(keys_per_step):
        for t, lanes in lane_tiles:
            gate = jnp.zeros((nk, HEAD_LANES), BF16)
            for h in range(PEER_HEADS):
                row = ii * PEER_HEADS + h
                n = jnp.broadcast_to(n_ref[t, row:row + 1, :], (pack, HEAD_LANES)).astype(BF16)
                c = jnp.broadcast_to(c_ref[t, row:row + 1, :], (pack, HEAD_LANES)).astype(BF16)
                n = jnp.tile(n, (nk // pack, 1))
                c = jnp.tile(c, (nk // pack, 1))
                gate = gate + jnp.where(tab_ref[h, 0, :nk, lanes] < n, tab_ref[h, 1, :nk, lanes] * c,
                                        jnp.zeros_like(gate))
            gate_ref[ii * nk:(ii + 1) * nk, lanes] = gate
    hid_ref[...] = gate_ref[:, :tt] * _gelu(at_ref[...])

    @pl.when(e < pl.num_programs(1))
    def _():
        acc_ref[...] += jnp.dot(pvt_ref[...], hid_ref[...], preferred_element_type=F32)


def _peer(tn, pu, pvt, r2, e2, ntab, ctab):
    tokens = tn.shape[0]
    tt = min(PEER_TOKEN_TILE, tokens)
    et = PEER_EXPERT_TILE
    rows_per_step = (et // PEER_N_KEYS) * PEER_HEADS
    whole_tab = pl.BlockSpec((PEER_HEADS, PEER_N_KEYS, tt), lambda t, e: (0, 0, t))
    step_tab = pl.BlockSpec((None, None, tt // HEAD_LANES, rows_per_step, HEAD_LANES),
                            lambda t, e: (t, e, 0, 0, 0))
    return pl.pallas_call(
        _peer_kernel,
        grid=(tokens // tt, PEER_N_EXPERTS // et),
        in_specs=[pl.BlockSpec((tt, D_MODEL), lambda t, e: (t, 0)),
                  pl.BlockSpec((et, D_MODEL), lambda t, e: (e, 0)),
                  pl.BlockSpec((D_MODEL, et), lambda t, e: (0, e)),
                  whole_tab, whole_tab, step_tab, step_tab],
        out_specs=pl.BlockSpec((D_MODEL, tt), lambda t, e: (0, t)),
        out_shape=jax.ShapeDtypeStruct((D_MODEL, tokens), F32),
        scratch_shapes=[pltpu.VMEM((et, tt), BF16), pltpu.VMEM((et, tt + HEAD_LANES), BF16),
                        pltpu.VMEM((et, tt), BF16),
                        pltpu.VMEM((PEER_HEADS, 2, PEER_N_KEYS + 2 * BF16_TILE_ROWS, tt + HEAD_LANES), BF16)],
        compiler_params=_params("parallel", "arbitrary"),
        name="peer_experts",
    )(tn, pu, pvt, r2, e2, ntab, ctab)


def _final_kernel(h_ref, mix_ref, gf_ref, y_ref):
    y_ref[...] = _rms(h_ref[...] + jnp.transpose(mix_ref[...]), gf_ref[...])


def _final(h, mix_t, final_g):
    rows = h.shape[0]
    tile = min(ROW_TILE, rows)
    row_spec = pl.BlockSpec((tile, D_MODEL), lambda i: (i, 0))
    return pl.pallas_call(
        _final_kernel,
        grid=(rows // tile,),
        in_specs=[row_spec, pl.BlockSpec((D_MODEL, tile), lambda i: (0, i)), _const_spec((1, D_MODEL))],
        out_specs=row_spec,
        out_shape=jax.ShapeDtypeStruct((rows, D_MODEL), F32),
        compiler_params=_params("parallel"),
        name="final_norm",
    )(h, mix_t, final_g)


def _rope_tables(pos):
    half = ROPE_DIM // 2
    inv = ROPE_THETA ** (-jnp.arange(half, dtype=F32) * 2.0 / ROPE_DIM)
    d = jnp.arange(HEAD_LANES) % DA_HEAD_DIM
    ang = pos[:, None] * inv[d % half][None, :]
    cos_t = jnp.where(d < ROPE_DIM, jnp.cos(ang), 1.0)
    sin_a = jnp.where(d < half, -jnp.sin(ang), 0.0)
    sin_b = jnp.where((d >= half) & (d < ROPE_DIM), jnp.sin(ang), 0.0)
    return cos_t, sin_a, sin_b


def _row_pipeline(x, rope_tabs, wmix, bias, attend, weights):
    (g1, w_in, gm_g, subln_g, w_o, g2, wpq, kbig, pu, pvt, final_g, attn_scale) = weights
    q1, q2, k32, v32, kb, vb, gmo, gn = _proj(x, g1, w_in, *rope_tabs, gm_g, wmix, bias)
    ao = attend(q1, q2, k32, v32, kb, vb)
    h, tn, scores = _merge(ao, gmo, x, subln_g, w_o, g2, wpq, kbig, attn_scale)
    rows = x.shape[0]
    r2, e2, ntab, ctab = _topk(scores.reshape(2, PEER_N_KEYS, PEER_HEADS, rows))
    flat = lambda t: t.reshape(t.shape[:3] + (t.shape[3] * PEER_HEADS, TOPK_LANES))
    y = _final(h, _peer(tn, pu, pvt, r2, e2, flat(ntab), flat(ctab)), final_g)
    return y, k32, v32, gn


def kernel(x_prompt, x_sample, cache_k, cache_v, page_table, norm1_g, w_in, gm_norm_g, gm_ws, gm_b,
           da_lq1, da_lk1, da_lq2, da_lk2, da_subln_g, w_o, norm2_g, peer_wq, peer_keys, peer_u,
           peer_v, final_g):
    batch, seq, _ = x_prompt.shape
    nb, n_new, _ = x_sample.shape
    n_pages = page_table.shape[1]
    past = n_pages * PAGE_SIZE
    assert w_in.shape[0] == 1 and seq % CHUNK == 0 and seq % min(ROW_TILE, batch * seq) == 0

    lam_init = 0.8 - 0.6 * math.exp(-0.3 * 0)
    dots = lambda a, b: jnp.exp(jnp.sum(a.astype(F32) * b.astype(F32)))
    lam = (dots(da_lq1[0], da_lk1[0]) - dots(da_lq2[0], da_lk2[0]) + lam_init).reshape(1).astype(F32)

    eye = jnp.eye(PEER_HEADS, dtype=F32)
    kbig = jnp.einsum("hcjd,hg->cjhgd", peer_keys[0], eye).reshape(
        2, PEER_N_KEYS * PEER_HEADS, PEER_HEADS * PEER_HALF).astype(BF16)
    wpq = peer_wq[0].reshape(D_MODEL, PEER_HEADS, 2, PEER_HALF).transpose(0, 2, 1, 3).reshape(
        D_MODEL, 2 * PEER_HEADS * PEER_HALF).astype(BF16)
    weights = (norm1_g[0][None], w_in[0].astype(BF16), gm_norm_g[0].reshape(1, GM_WIDTH),
               da_subln_g[0][None], w_o[0].astype(BF16), norm2_g[0][None], wpq, kbig,
               peer_u[0].astype(BF16), peer_v[0].T.astype(BF16), final_g[None], 1.0 - lam_init)

    xp = x_prompt.reshape(batch * seq, D_MODEL)
    tabs_p = _rope_tables(jnp.arange(seq, dtype=F32))
    wmix_p = jnp.tril(gm_ws[0]).astype(BF16)
    bias_p = jnp.repeat(jnp.transpose(gm_b[0]), HEAD_LANES, axis=1)
    attend_p = lambda q1, q2, k32, v32, kb, vb: _prompt_attention(lam, q1, q2, kb, vb, batch, seq)
    y_p, k_p, v_p, _ = _row_pipeline(xp, tabs_p, wmix_p, bias_p, attend_p, weights)

    xs = x_sample.reshape(nb * n_new, D_MODEL)
    tabs_s = _rope_tables(jnp.tile(past + jnp.arange(n_new, dtype=F32), nb))
    rows_s = min(ROW_TILE, nb * n_new)
    w_new = jnp.tril(gm_ws[0][:, :n_new, :n_new])
    r = jnp.arange(rows_s)
    pick = (r[:, None] % n_new == jnp.arange(n_new)[None, :]).astype(F32)
    w_rows = jnp.einsum("ri,hij,cj->hrc", pick, w_new, pick, precision=lax.Precision.HIGHEST)
    wmix_s = jnp.where(r[:, None] // n_new == r[None, :] // n_new, w_rows, 0.0).astype(BF16)
    bias_s = jnp.tile(jnp.repeat(jnp.transpose(gm_b[0][:, :n_new]), HEAD_LANES, axis=1), (rows_s // n_new, 1))
    ck = cache_k[0].reshape(-1, PAGE_SIZE * DA_HEADS, HEAD_LANES)
    cv = cache_v[0].reshape(-1, PAGE_SIZE * DA_HEADS, HEAD_LANES)

    def attend_s(q1, q2, k32, v32, kb, vb):
        per_head = lambda t: jnp.transpose(t.reshape(nb, n_new, DA_HEADS, HEAD_LANES), (0, 2, 1, 3))
        q = jnp.stack([per_head(q1), per_head(q2)], axis=1).reshape(nb, 2 * DA_HEADS * n_new, HEAD_LANES)
        new_rows = lambda t: t.reshape(nb, n_new * DA_HEADS, HEAD_LANES).astype(BF16)
        ao = _decode_attention(page_table, lam, q, new_rows(k32), new_rows(v32), ck, cv)
        ao = jnp.transpose(ao.reshape(nb, DA_HEADS, n_new, HEAD_LANES), (0, 2, 1, 3))
        return ao.reshape(nb * n_new, DA_WIDTH)

    y_s, k_s, v_s, gn_s = _row_pipeline(xs, tabs_s, wmix_s, bias_s, attend_s, weights)

    head_shape = lambda b, s: (1, b, s, DA_HEADS, HEAD_LANES)
    return (y_p.reshape(batch, seq, D_MODEL), y_s.reshape(nb, n_new, D_MODEL),
            k_p.reshape(head_shape(batch, seq)), v_p.reshape(head_shape(batch, seq)),
            k_s.reshape(head_shape(nb, n_new)), v_s.reshape(head_shape(nb, n_new)),
            gn_s.reshape(1, nb, n_new, GM_HEADS, HEAD_LANES))
```

```python
import functools
import math

import jax
import jax.numpy as jnp
from jax import lax
from jax.experimental import pallas as pl
from jax.experimental.pallas import tpu as pltpu

F32 = jnp.float32
BF16 = jnp.bfloat16

D_MODEL = 1024
DA_WIDTH = 512
DA_HEADS = 4
HEAD_LANES = 128
BF16_TILE_ROWS = 16
DA_HEAD_DIM = 64
GM_WIDTH = 512
GM_HEADS = 4
CHUNK = 128
IN_WIDTH = 3 * DA_WIDTH + 2 * GM_WIDTH
ROPE_THETA = 500000.0
ROPE_DIM = 16
PAGE_SIZE = 128
PEER_HEADS = 8
PEER_N_KEYS = 128
PEER_N_EXPERTS = PEER_N_KEYS * PEER_N_KEYS
PEER_HALF = 128
PEER_TOPK = 16
NORM_EPS = 1e-6
NEG_INF = float("-inf")

VMEM_LIMIT_BYTES = 56 * 1024 * 1024

ROW_TILE = 512
ATTN_Q_TILE = 256
ATTN_HEADS_PER_STEP = 2
TOPK_LANES = 128
PEER_EXPERT_TILE = 1024
PEER_TOKEN_TILE = 1024


def _params(*semantics):
    return pltpu.CompilerParams(dimension_semantics=semantics, vmem_limit_bytes=VMEM_LIMIT_BYTES)


def _rms(x, g):
    return x * lax.rsqrt(jnp.mean(x * x, axis=-1, keepdims=True) + NORM_EPS) * g


def _gelu(x):
    return x * (0.5 * (1.0 + jnp.tanh(math.sqrt(2.0 / math.pi) * (x + 0.044715 * (x * x * x)))))


def _const_spec(shape):
    zeros = (0,) * len(shape)
    return pl.BlockSpec(shape, lambda *_: zeros)


def _proj_kernel(x_ref, g1_ref, win_ref, cos_ref, sa_ref, sb_ref, gmg_ref, wmix_ref, bias_ref,
                 q1_ref, q2_ref, k32_ref, v32_ref, kb_ref, vb_ref, gmo_ref, gn_ref, *, mix_rows):
    rows = x_ref.shape[0]
    xn = _rms(x_ref[...], g1_ref[...]).astype(BF16)
    z = jnp.dot(xn, win_ref[...], preferred_element_type=F32)
    cos_t, sin_a, sin_b = cos_ref[...], sa_ref[...], sb_ref[...]
    first_half = lax.broadcasted_iota(jnp.int32, (rows, HEAD_LANES), 1) < DA_HEAD_DIM

    def rope(t):
        return t * cos_t + pltpu.roll(t, HEAD_LANES - 8, 1) * sin_a + pltpu.roll(t, 8, 1) * sin_b

    for b in range(DA_HEADS):
        sl = slice(b * HEAD_LANES, (b + 1) * HEAD_LANES)
        q = rope(z[:, sl]) * (1.0 / math.sqrt(DA_HEAD_DIM))
        q1_ref[:, sl] = jnp.where(first_half, q, 0.0).astype(BF16)
        q2_ref[:, sl] = jnp.where(first_half, 0.0, q).astype(BF16)
        k = rope(z[:, DA_WIDTH + b * HEAD_LANES:DA_WIDTH + (b + 1) * HEAD_LANES])
        v = z[:, 2 * DA_WIDTH + b * HEAD_LANES:2 * DA_WIDTH + (b + 1) * HEAD_LANES]
        k32_ref[pl.ds(b, rows, stride=DA_HEADS), :] = k
        v32_ref[pl.ds(b, rows, stride=DA_HEADS), :] = v
        kb_ref[:, sl] = k.astype(BF16)
        vb_ref[:, sl] = v.astype(BF16)

    u = _gelu(z[:, 3 * DA_WIDTH:3 * DA_WIDTH + GM_WIDTH])
    g = _gelu(z[:, 3 * DA_WIDTH + GM_WIDTH:])
    for b in range(GM_HEADS):
        sl = slice(b * HEAD_LANES, (b + 1) * HEAD_LANES)
        gn = _rms(g[:, sl], gmg_ref[:, sl])
        gn_ref[:, sl] = gn
        gnb = gn.astype(BF16)
        for r in range(rows // mix_rows):
            rs = slice(r * mix_rows, (r + 1) * mix_rows)
            s = jnp.dot(wmix_ref[b], gnb[rs], preferred_element_type=F32) + bias_ref[:, sl]
            gmo_ref[rs, sl] = (u[rs, sl] * s).astype(BF16)


def _proj(x, g1, w_in, cos_t, sin_a, sin_b, gm_g, wmix, bias):
    rows = x.shape[0]
    tile = min(ROW_TILE, rows)
    mix_rows = wmix.shape[-1]
    row_spec = lambda w: pl.BlockSpec((tile, w), lambda i: (i, 0))
    table_blocks = cos_t.shape[0] // tile
    table_spec = pl.BlockSpec((tile, HEAD_LANES), lambda i: (i % table_blocks, 0))
    head_rows_spec = pl.BlockSpec((tile * DA_HEADS, HEAD_LANES), lambda i: (i, 0))
    out_shape = [jax.ShapeDtypeStruct((rows, DA_WIDTH), BF16),
                 jax.ShapeDtypeStruct((rows, DA_WIDTH), BF16),
                 jax.ShapeDtypeStruct((rows * DA_HEADS, HEAD_LANES), F32),
                 jax.ShapeDtypeStruct((rows * DA_HEADS, HEAD_LANES), F32),
                 jax.ShapeDtypeStruct((rows, DA_WIDTH), BF16),
                 jax.ShapeDtypeStruct((rows, DA_WIDTH), BF16),
                 jax.ShapeDtypeStruct((rows, GM_WIDTH), BF16),
                 jax.ShapeDtypeStruct((rows, GM_WIDTH), F32)]
    return pl.pallas_call(
        functools.partial(_proj_kernel, mix_rows=mix_rows),
        grid=(rows // tile,),
        in_specs=[row_spec(D_MODEL), _const_spec((1, D_MODEL)), _const_spec((D_MODEL, IN_WIDTH)),
                  table_spec, table_spec, table_spec,
                  _const_spec((1, GM_WIDTH)), _const_spec(wmix.shape), _const_spec(bias.shape)],
        out_specs=[row_spec(DA_WIDTH)] * 2 + [head_rows_spec] * 2 + [row_spec(DA_WIDTH)] * 2
                  + [row_spec(GM_WIDTH)] * 2,
        out_shape=out_shape,
        compiler_params=_params("parallel"),
        name="proj",
    )(x, g1, w_in, cos_t, sin_a, sin_b, gm_g, wmix, bias)


def _attn_kernel(lam_ref, q1_ref, q2_ref, k_ref, v_ref, o_ref):
    tq = q1_ref.shape[0]
    heads = q1_ref.shape[1] // HEAD_LANES
    i = pl.program_id(2)
    head_lanes = [slice(h * HEAD_LANES, (h + 1) * HEAD_LANES) for h in range(heads)]
    qs = [jnp.concatenate([q1_ref[:, sl], q2_ref[:, sl]], axis=0) for sl in head_lanes]

    def step(j, carry, masked):
        start = pl.multiple_of(j * tq, tq)
        new = []
        for q, sl, (m, l, acc) in zip(qs, head_lanes, carry):
            kb = k_ref[pl.ds(start, tq), sl]
            vb = v_ref[pl.ds(start, tq), sl]
            s = lax.dot_general(q, kb, (((1,), (1,)), ((), ())), preferred_element_type=F32)
            if masked:
                row = lax.broadcasted_iota(jnp.int32, (2 * tq, tq), 0)
                col = lax.broadcasted_iota(jnp.int32, (2 * tq, tq), 1)
                row = jnp.where(row >= tq, row - tq, row)
                s = jnp.where(col <= row, s, NEG_INF)
            m_new = jnp.maximum(m, jnp.max(s, axis=-1, keepdims=True))
            alpha = jnp.exp(m - m_new)
            p = jnp.exp(s - m_new)
            l = alpha * l + jnp.sum(p, axis=-1, keepdims=True)
            acc = alpha * acc + jnp.dot(p.astype(BF16), vb, preferred_element_type=F32)
            new.append((m_new, l, acc))
        return tuple(new)

    init = tuple((jnp.full((2 * tq, 1), NEG_INF, F32), jnp.zeros((2 * tq, 1), F32),
                  jnp.zeros((2 * tq, HEAD_LANES), F32)) for _ in range(heads))
    carry = lax.fori_loop(0, i, lambda j, c: step(j, c, False), init)
    for sl, (_, l, acc) in zip(head_lanes, step(i, carry, True)):
        o = acc / l
        o_ref[:, sl] = o[:tq] - lam_ref[0] * o[tq:]


def _prompt_attention(lam, q1, q2, kb, vb, batch, seq):
    tq = min(ATTN_Q_TILE, seq)
    nq = seq // tq
    width = ATTN_HEADS_PER_STEP * HEAD_LANES
    q_spec = pl.BlockSpec((tq, width), lambda b, h, i: (b * nq + i, h))
    kv_spec = pl.BlockSpec((seq, width), lambda b, h, i: (b, h))
    return pl.pallas_call(
        _attn_kernel,
        grid=(batch, DA_HEADS // ATTN_HEADS_PER_STEP, nq),
        in_specs=[pl.BlockSpec(memory_space=pltpu.SMEM), q_spec, q_spec, kv_spec, kv_spec],
        out_specs=q_spec,
        out_shape=jax.ShapeDtypeStruct((batch * seq, DA_WIDTH), F32),
        compiler_params=_params("parallel", "parallel", "arbitrary"),
        name="prompt_attn",
    )(lam, q1, q2, kb, vb)


def _decode_attn_kernel(pt_ref, lam_ref, q_ref, kn_ref, vn_ref, *rest, n_pages, n_new):
    del pt_ref
    k_pages, v_pages = rest[:n_pages], rest[n_pages:2 * n_pages]
    o_ref, k_all, v_all = rest[2 * n_pages:]
    page_rows = PAGE_SIZE * DA_HEADS
    for p in range(n_pages):
        rs = slice(p * page_rows, (p + 1) * page_rows)
        k_all[rs, :] = k_pages[p][0].astype(BF16)
        v_all[rs, :] = v_pages[p][0].astype(BF16)
    q = q_ref[0]
    nrow = q.shape[0]
    contract_last = (((1,), (1,)), ((), ()))
    s_past = lax.dot_general(q, k_all[...], contract_last, preferred_element_type=F32)
    s_new = lax.dot_general(q, kn_ref[0], contract_last, preferred_element_type=F32)

    def row_col(shape):
        row = lax.broadcasted_iota(jnp.int32, shape, 0)
        col = lax.broadcasted_iota(jnp.int32, shape, 1)
        return (row // n_new) % DA_HEADS, row % n_new, col % DA_HEADS, col // DA_HEADS

    q_head, _, k_head, _ = row_col(s_past.shape)
    s_past = jnp.where(q_head == k_head, s_past, NEG_INF)
    q_head, q_tok, k_head, k_tok = row_col(s_new.shape)
    s_new = jnp.where(q_head == k_head, jnp.where(k_tok <= q_tok, s_new, NEG_INF), NEG_INF)
    m = jnp.maximum(jnp.max(s_past, axis=-1, keepdims=True), jnp.max(s_new, axis=-1, keepdims=True))
    p_past = jnp.exp(s_past - m)
    p_new = jnp.exp(s_new - m)
    inv = 1.0 / (jnp.sum(p_past, axis=-1, keepdims=True) + jnp.sum(p_new, axis=-1, keepdims=True))
    half = nrow // 2
    lam = lam_ref[0]
    a_past = (p_past[:half] * inv[:half] - lam * (p_past[half:] * inv[half:])).astype(BF16)
    a_new = (p_new[:half] * inv[:half] - lam * (p_new[half:] * inv[half:])).astype(BF16)
    o_ref[0] = (jnp.dot(a_past, v_all[...], preferred_element_type=F32)
                + jnp.dot(a_new, vn_ref[0], preferred_element_type=F32))


def _decode_attention(page_table, lam, q_rows, k_new, v_new, cache_k, cache_v):
    nb, n_pages = page_table.shape
    n_new = k_new.shape[1] // DA_HEADS
    page_rows = PAGE_SIZE * DA_HEADS
    page_spec = lambda p: pl.BlockSpec((1, page_rows, HEAD_LANES), lambda b, pt: (pt[b * n_pages + p], 0, 0))
    per_b = lambda r: pl.BlockSpec((1, r, HEAD_LANES), lambda b, pt: (b, 0, 0))
    grid_spec = pltpu.PrefetchScalarGridSpec(
        num_scalar_prefetch=1,
        grid=(nb,),
        in_specs=[pl.BlockSpec(memory_space=pltpu.SMEM), per_b(q_rows.shape[1]),
                  per_b(n_new * DA_HEADS), per_b(n_new * DA_HEADS)]
                 + [page_spec(p) for p in range(n_pages)] * 2,
        out_specs=per_b(n_new * DA_HEADS),
        scratch_shapes=[pltpu.VMEM((n_pages * page_rows, HEAD_LANES), BF16)] * 2,
    )
    return pl.pallas_call(
        functools.partial(_decode_attn_kernel, n_pages=n_pages, n_new=n_new),
        grid_spec=grid_spec,
        out_shape=jax.ShapeDtypeStruct((nb, n_new * DA_HEADS, HEAD_LANES), F32),
        compiler_params=_params("arbitrary"),
        name="decode_attn",
    )(page_table.reshape(-1), lam, q_rows, k_new, v_new,
      *([cache_k] * n_pages), *([cache_v] * n_pages))


def _merge_kernel(ao_ref, gmo_ref, x_ref, sg_ref, wo_ref, g2_ref, wpq_ref, kbig_ref,
                  h_ref, tn_ref, s_ref, *, attn_scale):
    parts = []
    for b in range(DA_HEADS):
        sl = slice(b * HEAD_LANES, (b + 1) * HEAD_LANES)
        parts.append((_rms(ao_ref[:, sl], sg_ref[...]) * attn_scale).astype(BF16))
    cat = jnp.concatenate(parts + [gmo_ref[...]], axis=1)
    h = x_ref[...] + jnp.dot(cat, wo_ref[...], preferred_element_type=F32)
    h_ref[...] = h
    tn = _rms(h, g2_ref[...]).astype(BF16)
    tn_ref[...] = tn
    qp = jnp.dot(tn, wpq_ref[...], preferred_element_type=F32).astype(BF16)
    width = PEER_HEADS * PEER_HALF
    for c in range(2):
        s_ref[c] = lax.dot_general(kbig_ref[c], qp[:, c * width:(c + 1) * width],
                                   (((1,), (1,)), ((), ())), preferred_element_type=F32)


def _merge(ao, gmo, x, subln_g, w_o, g2, wpq, kbig, attn_scale):
    rows = x.shape[0]
    tile = min(ROW_TILE, rows)
    width = PEER_HEADS * PEER_HALF
    row_spec = lambda w: pl.BlockSpec((tile, w), lambda i: (i, 0))
    return pl.pallas_call(
        functools.partial(_merge_kernel, attn_scale=attn_scale),
        grid=(rows // tile,),
        in_specs=[row_spec(DA_WIDTH), row_spec(GM_WIDTH), row_spec(D_MODEL), _const_spec((1, HEAD_LANES)),
                  _const_spec((D_MODEL, D_MODEL)), _const_spec((1, D_MODEL)),
                  _const_spec((D_MODEL, 2 * width)), _const_spec((2, PEER_N_KEYS * PEER_HEADS, width))],
        out_specs=[row_spec(D_MODEL), row_spec(D_MODEL),
                   pl.BlockSpec((2, PEER_N_KEYS * PEER_HEADS, tile), lambda i: (0, 0, i))],
        out_shape=[jax.ShapeDtypeStruct((rows, D_MODEL), F32),
                   jax.ShapeDtypeStruct((rows, D_MODEL), BF16),
                   jax.ShapeDtypeStruct((2, PEER_N_KEYS * PEER_HEADS, rows), F32)],
        compiler_params=_params("parallel"),
        name="merge",
    )(ao, gmo, x, subln_g, w_o, g2, wpq, kbig)


def _young_candidates():
    return [(p, q) for p in range(PEER_TOPK) for q in range(PEER_TOPK) if (p + 1) * (q + 1) <= PEER_TOPK]


def _topk_kernel(s_ref, r2_ref, e2_ref, n_ref, c_ref, work_ref, top_ref, idx_ref, young_ref,
                 rank_ref, exp_ref):
    nk = PEER_N_KEYS
    keys_per_block = n_ref.shape[1]
    tile = s_ref.shape[2:]
    one, zero = jnp.ones(tile, F32), jnp.zeros(tile, F32)

    def extract_distinct(c):
        def round_(r, bound):
            tree = [jnp.where(s_ref[c, j] < bound, s_ref[c, j], NEG_INF) for j in range(nk)]
            while len(tree) > 1:
                tree = [jnp.maximum(tree[t], tree[t + 1]) for t in range(0, len(tree), 2)]
            top_ref[c, r] = tree[0]
            return tree[0]

        last = lax.fori_loop(0, PEER_TOPK, round_, jnp.full(tile, float("inf"), F32))
        reached = zero
        for j in range(nk):
            reached = reached + jnp.where(s_ref[c, j] >= last, one, zero)
        return reached

    def extract(c):
        for j in range(nk):
            work_ref[j] = s_ref[c, j]

        def round_(r, _):
            vals = [work_ref[j] for j in range(nk)]
            idxs = [None] * nk
            width = 1
            while len(vals) > 1:
                nv, ni = [], []
                for t in range(0, len(vals), 2):
                    a, b = vals[t], vals[t + 1]
                    take_b = b > a
                    ia = idxs[t] if idxs[t] is not None else float(t * width)
                    ib = idxs[t + 1] if idxs[t + 1] is not None else float((t + 1) * width)
                    nv.append(jnp.maximum(a, b))
                    ni.append(jnp.where(take_b, ib, ia))
                vals, idxs = nv, ni
                width *= 2
            best, best_idx = vals[0], idxs[0]
            top_ref[c, r] = best
            idx_ref[c, r] = best_idx
            for j in range(nk):
                work_ref[j] = jnp.where(best_idx == float(j), NEG_INF, work_ref[j])
            return 0

        lax.fori_loop(0, PEER_TOPK, round_, 0)

    miscount = jnp.maximum(jnp.abs(extract_distinct(0) - float(PEER_TOPK)),
                           jnp.abs(extract_distinct(1) - float(PEER_TOPK)))
    tied = jnp.max(miscount) > 0.0

    @pl.when(tied)
    def _():
        extract(0)
        extract(1)

    a = [top_ref[0, p] for p in range(PEER_TOPK)]
    b = [top_ref[1, q] for q in range(PEER_TOPK)]
    cands = _young_candidates()
    cand = {pq: a[pq[0]] + b[pq[1]] for pq in cands}
    n_of_p = [zero] * PEER_TOPK
    ea = [jnp.exp(a[p] - a[0]) for p in range(PEER_TOPK)]
    eb = [jnp.exp(b[q] - b[0]) for q in range(PEER_TOPK)]
    z = zero
    for (p, q) in cands:
        fixed = 0
        beaten = zero
        for (p2, q2) in cands:
            if (p2, q2) == (p, q):
                continue
            if p2 <= p and q2 <= q:
                fixed += 1
            elif p2 >= p and q2 >= q:
                continue
            elif p2 * PEER_TOPK + q2 < p * PEER_TOPK + q:
                beaten = beaten + jnp.where(cand[(p2, q2)] >= cand[(p, q)], one, zero)
            else:
                beaten = beaten + jnp.where(cand[(p2, q2)] > cand[(p, q)], one, zero)
        sel = jnp.where(beaten + float(fixed) < float(PEER_TOPK), one, zero)
        n_of_p[p] = n_of_p[p] + sel
        z = z + sel * (ea[p] * eb[q])
    inv_z = 1.0 / z
    for p in range(PEER_TOPK):
        young_ref[p] = n_of_p[p]

    def scatter(hit_a, hit_b):
        for j in range(nk):
            n_j = zero
            r_j = jnp.full(tile, float(nk - 1), F32)
            for p in range(PEER_TOPK):
                n_j = jnp.where(hit_a(p, j), young_ref[p], n_j)
                r_j = jnp.where(hit_b(p, j), float(p), r_j)
            n_ref[j // keys_per_block, j % keys_per_block] = n_j
            rank_ref[j * PEER_HEADS:(j + 1) * PEER_HEADS, :] = r_j

    @pl.when(jnp.logical_not(tied))
    def _():
        scatter(lambda p, j: top_ref[0, p] == s_ref[0, j], lambda p, j: top_ref[1, p] == s_ref[1, j])

    @pl.when(tied)
    def _():
        scatter(lambda p, j: idx_ref[0, p] == float(j), lambda p, j: idx_ref[1, p] == float(j))

    for j in range(nk):
        c_ref[j // keys_per_block, j % keys_per_block] = jnp.exp(s_ref[0, j] - a[0]) * inv_z
        exp_ref[j * PEER_HEADS:(j + 1) * PEER_HEADS, :] = jnp.exp(s_ref[1, j] - b[0])
    for h in range(PEER_HEADS):
        r2_ref[h] = rank_ref[pl.ds(h, nk, stride=PEER_HEADS), :].astype(BF16)
        e2_ref[h] = exp_ref[pl.ds(h, nk, stride=PEER_HEADS), :].astype(BF16)


def _topk(scores):
    tokens = scores.shape[-1]
    lanes = TOPK_LANES
    lane_tiles = min(PEER_TOKEN_TILE, tokens) // lanes
    keys_per_block = PEER_EXPERT_TILE // PEER_N_KEYS
    n_blocks = PEER_N_KEYS // keys_per_block
    tab2 = jax.ShapeDtypeStruct((PEER_HEADS, PEER_N_KEYS, tokens), BF16)
    tab2_spec = pl.BlockSpec((PEER_HEADS, PEER_N_KEYS, lanes), lambda i: (0, 0, i))
    tab1 = jax.ShapeDtypeStruct((tokens // (lane_tiles * lanes), n_blocks, lane_tiles, keys_per_block,
                                 PEER_HEADS, lanes), F32)
    tab1_spec = pl.BlockSpec((None, n_blocks, None, keys_per_block, PEER_HEADS, lanes),
                             lambda i: (i // lane_tiles, 0, i % lane_tiles, 0, 0, 0))
    return pl.pallas_call(
        _topk_kernel,
        grid=(tokens // lanes,),
        in_specs=[pl.BlockSpec((2, PEER_N_KEYS, PEER_HEADS, lanes), lambda i: (0, 0, 0, i))],
        out_specs=[tab2_spec] * 2 + [tab1_spec] * 2,
        out_shape=[tab2] * 2 + [tab1] * 2,
        scratch_shapes=[pltpu.VMEM((PEER_N_KEYS, PEER_HEADS, lanes), F32),
                        pltpu.VMEM((2, PEER_TOPK, PEER_HEADS, lanes), F32),
                        pltpu.VMEM((2, PEER_TOPK, PEER_HEADS, lanes), F32),
                        pltpu.VMEM((PEER_TOPK, PEER_HEADS, lanes), F32),
                        pltpu.VMEM((PEER_N_KEYS * PEER_HEADS, lanes), F32),
                        pltpu.VMEM((PEER_N_KEYS * PEER_HEADS, lanes), F32)],
        compiler_params=_params("parallel"),
        name="peer_topk",
    )(scores)


def _peer_kernel(tn_ref, pu_ref, pvt_ref, r2_ref, e2_ref, n_ref, c_ref, acc_ref,
                 at_ref, gate_ref, hid_ref, tab_ref):
    e = pl.program_id(1)
    tt = tn_ref.shape[0]
    nk = PEER_N_KEYS
    keys_per_step = pu_ref.shape[0] // nk
    lane_tiles = [(t, slice(t * HEAD_LANES, (t + 1) * HEAD_LANES)) for t in range(tt // HEAD_LANES)]
    pack = BF16_TILE_ROWS

    @pl.when(e == 0)
    def _():
        tab_ref[:, 0, :nk, :tt] = r2_ref[...]
        tab_ref[:, 1, :nk, :tt] = e2_ref[...]
        acc_ref[...] = jnp.zeros_like(acc_ref)

    at_ref[...] = lax.dot_general(pu_ref[...], tn_ref[...], (((1,), (1,)), ((), ())),
                                  preferred_element_type=F32)
    for ii in range(keys_per_step):
        for t, lanes in lane_tiles:
            gate = jnp.zeros((nk, HEAD_LANES), BF16)
            for h in range(PEER_HEADS):
                row = ii * PEER_HEADS + h
                n = jnp.broadcast_to(n_ref[t, row:row + 1, :], (pack, HEAD_LANES)).astype(BF16)
                c = jnp.broadcast_to(c_ref[t, row:row + 1, :], (pack, HEAD_LANES)).astype(BF16)
                n = jnp.tile(n, (nk // pack, 1))
                c = jnp.tile(c, (nk // pack, 1))
                gate = gate + jnp.where(tab_ref[h, 0, :nk, lanes] < n, tab_ref[h, 1, :nk, lanes] * c,
                                        jnp.zeros_like(gate))
            gate_ref[ii * nk:(ii + 1) * nk, lanes] = gate
    hid_ref[...] = gate_ref[:, :tt] * _gelu(at_ref[...]).astype(BF16)

    @pl.when(e < pl.num_programs(1))
    def _():
        acc_ref[...] += jnp.dot(pvt_ref[...], hid_ref[...], preferred_element_type=F32)


def _peer(tn, pu, pvt, r2, e2, ntab, ctab):
    tokens = tn.shape[0]
    tt = min(PEER_TOKEN_TILE, tokens)
    et = PEER_EXPERT_TILE
    rows_per_step = (et // PEER_N_KEYS) * PEER_HEADS
    whole_tab = pl.BlockSpec((PEER_HEADS, PEER_N_KEYS, tt), lambda t, e: (0, 0, t))
    step_tab = pl.BlockSpec((None, None, tt // HEAD_LANES, rows_per_step, HEAD_LANES),
                            lambda t, e: (t, e, 0, 0, 0))
    return pl.pallas_call(
        _peer_kernel,
        grid=(tokens // tt, PEER_N_EXPERTS // et),
        in_specs=[pl.BlockSpec((tt, D_MODEL), lambda t, e: (t, 0)),
                  pl.BlockSpec((et, D_MODEL), lambda t, e: (e, 0)),
                  pl.BlockSpec((D_MODEL, et), lambda t, e: (0, e)),
                  whole_tab, whole_tab, step_tab, step_tab],
        out_specs=pl.BlockSpec((D_MODEL, tt), lambda t, e: (0, t)),
        out_shape=jax.ShapeDtypeStruct((D_MODEL, tokens), F32),
        scratch_shapes=[pltpu.VMEM((et, tt), F32), pltpu.VMEM((et, tt + HEAD_LANES), BF16),
                        pltpu.VMEM((et, tt), BF16),
                        pltpu.VMEM((PEER_HEADS, 2, PEER_N_KEYS + 2 * BF16_TILE_ROWS, tt + HEAD_LANES), BF16)],
        compiler_params=_params("parallel", "arbitrary"),
        name="peer_experts",
    )(tn, pu, pvt, r2, e2, ntab, ctab)


def _final_kernel(h_ref, mix_ref, gf_ref, y_ref):
    y_ref[...] = _rms(h_ref[...] + jnp.transpose(mix_ref[...]), gf_ref[...])


def _final(h, mix_t, final_g):
    rows = h.shape[0]
    tile = min(ROW_TILE, rows)
    row_spec = pl.BlockSpec((tile, D_MODEL), lambda i: (i, 0))
    return pl.pallas_call(
        _final_kernel,
        grid=(rows // tile,),
        in_specs=[row_spec, pl.BlockSpec((D_MODEL, tile), lambda i: (0, i)), _const_spec((1, D_MODEL))],
        out_specs=row_spec,
        out_shape=jax.ShapeDtypeStruct((rows, D_MODEL), F32),
        compiler_params=_params("parallel"),
        name="final_norm",
    )(h, mix_t, final_g)


def _rope_tables(pos):
    half = ROPE_DIM // 2
    inv = ROPE_THETA ** (-jnp.arange(half, dtype=F32) * 2.0 / ROPE_DIM)
    d = jnp.arange(HEAD_LANES) % DA_HEAD_DIM
    ang = pos[:, None] * inv[d % half][None, :]
    cos_t = jnp.where(d < ROPE_DIM, jnp.cos(ang), 1.0)
    sin_a = jnp.where(d < half, -jnp.sin(ang), 0.0)
    sin_b = jnp.where((d >= half) & (d < ROPE_DIM), jnp.sin(ang), 0.0)
    return cos_t, sin_a, sin_b


def _row_pipeline(x, rope_tabs, wmix, bias, attend, weights):
    (g1, w_in, gm_g, subln_g, w_o, g2, wpq, kbig, pu, pvt, final_g, attn_scale) = weights
    q1, q2, k32, v32, kb, vb, gmo, gn = _proj(x, g1, w_in, *rope_tabs, gm_g, wmix, bias)
    ao = attend(q1, q2, k32, v32, kb, vb)
    h, tn, scores = _merge(ao, gmo, x, subln_g, w_o, g2, wpq, kbig, attn_scale)
    rows = x.shape[0]
    r2, e2, ntab, ctab = _topk(scores.reshape(2, PEER_N_KEYS, PEER_HEADS, rows))
    flat = lambda t: t.reshape(t.shape[:3] + (t.shape[3] * PEER_HEADS, TOPK_LANES))
    y = _final(h, _peer(tn, pu, pvt, r2, e2, flat(ntab), flat(ctab)), final_g)
    return y, k32, v32, gn


def kernel(x_prompt, x_sample, cache_k, cache_v, page_table, norm1_g, w_in, gm_norm_g, gm_ws, gm_b,
           da_lq1, da_lk1, da_lq2, da_lk2, da_subln_g, w_o, norm2_g, peer_wq, peer_keys, peer_u,
           peer_v, final_g):
    batch, seq, _ = x_prompt.shape
    nb, n_new, _ = x_sample.shape
    n_pages = page_table.shape[1]
    past = n_pages * PAGE_SIZE
    assert w_in.shape[0] == 1 and seq % CHUNK == 0 and seq % min(ROW_TILE, batch * seq) == 0

    lam_init = 0.8 - 0.6 * math.exp(-0.3 * 0)
    dots = lambda a, b: jnp.exp(jnp.sum(a.astype(F32) * b.astype(F32)))
    lam = (dots(da_lq1[0], da_lk1[0]) - dots(da_lq2[0], da_lk2[0]) + lam_init).reshape(1).astype(F32)

    eye = jnp.eye(PEER_HEADS, dtype=F32)
    kbig = jnp.einsum("hcjd,hg->cjhgd", peer_keys[0], eye).reshape(
        2, PEER_N_KEYS * PEER_HEADS, PEER_HEADS * PEER_HALF).astype(BF16)
    wpq = peer_wq[0].reshape(D_MODEL, PEER_HEADS, 2, PEER_HALF).transpose(0, 2, 1, 3).reshape(
        D_MODEL, 2 * PEER_HEADS * PEER_HALF).astype(BF16)
    weights = (norm1_g[0][None], w_in[0].astype(BF16), gm_norm_g[0].reshape(1, GM_WIDTH),
               da_subln_g[0][None], w_o[0].astype(BF16), norm2_g[0][None], wpq, kbig,
               peer_u[0].astype(BF16), peer_v[0].T.astype(BF16), final_g[None], 1.0 - lam_init)

    xp = x_prompt.reshape(batch * seq, D_MODEL)
    tabs_p = _rope_tables(jnp.arange(seq, dtype=F32))
    wmix_p = jnp.tril(gm_ws[0]).astype(BF16)
    bias_p = jnp.repeat(jnp.transpose(gm_b[0]), HEAD_LANES, axis=1)
    attend_p = lambda q1, q2, k32, v32, kb, vb: _prompt_attention(lam, q1, q2, kb, vb, batch, seq)
    y_p, k_p, v_p, _ = _row_pipeline(xp, tabs_p, wmix_p, bias_p, attend_p, weights)

    xs = x_sample.reshape(nb * n_new, D_MODEL)
    tabs_s = _rope_tables(jnp.tile(past + jnp.arange(n_new, dtype=F32), nb))
    rows_s = min(ROW_TILE, nb * n_new)
    w_new = jnp.tril(gm_ws[0][:, :n_new, :n_new])
    r = jnp.arange(rows_s)
    pick = (r[:, None] % n_new == jnp.arange(n_new)[None, :]).astype(F32)
    w_rows = jnp.einsum("ri,hij,cj->hrc", pick, w_new, pick, precision=lax.Precision.HIGHEST)
    wmix_s = jnp.where(r[:, None] // n_new == r[None, :] // n_new, w_rows, 0.0).astype(BF16)
    bias_s = jnp.tile(jnp.repeat(jnp.transpose(gm_b[0][:, :n_new]), HEAD_LANES, axis=1), (rows_s // n_new, 1))
    ck = cache_k[0].reshape(-1, PAGE_SIZE * DA_HEADS, HEAD_LANES)
    cv = cache_v[0].reshape(-1, PAGE_SIZE * DA_HEADS, HEAD_LANES)

    def attend_s(q1, q2, k32, v32, kb, vb):
        per_head = lambda t: jnp.transpose(t.reshape(nb, n_new, DA_HEADS, HEAD_LANES), (0, 2, 1, 3))
        q = jnp.stack([per_head(q1), per_head(q2)], axis=1).reshape(nb, 2 * DA_HEADS * n_new, HEAD_LANES)
        new_rows = lambda t: t.reshape(nb, n_new * DA_HEADS, HEAD_LANES).astype(BF16)
        ao = _decode_attention(page_table, lam, q, new_rows(k32), new_rows(v32), ck, cv)
        ao = jnp.transpose(ao.reshape(nb, DA_HEADS, n_new, HEAD_LANES), (0, 2, 1, 3))
        return ao.reshape(nb * n_new, DA_WIDTH)

    y_s, k_s, v_s, gn_s = _row_pipeline(xs, tabs_s, wmix_s, bias_s, attend_s, weights)

    head_shape = lambda b, s: (1, b, s, DA_HEADS, HEAD_LANES)
    return (y_p.reshape(batch, seq, D_MODEL), y_s.reshape(nb, n_new, D_MODEL),
            k_p.reshape(head_shape(batch, seq)), v_p.reshape(head_shape(batch, seq)),
            k_s.reshape(head_shape(nb, n_new)), v_s.reshape(head_shape(nb, n_new)),
            gn_s.reshape(1, nb, n_new, GM_HEADS, HEAD_LANES))
```

```python
import functools
import math

import jax
import jax.numpy as jnp
from jax import lax
from jax.experimental import pallas as pl
from jax.experimental.pallas import tpu as pltpu

F32 = jnp.float32
BF16 = jnp.bfloat16

D_MODEL = 1024
DA_WIDTH = 512
DA_HEADS = 4
HEAD_LANES = 128
DA_HEAD_DIM = 64
GM_WIDTH = 512
GM_HEADS = 4
CHUNK = 128
IN_WIDTH = 3 * DA_WIDTH + 2 * GM_WIDTH
ROPE_THETA = 500000.0
ROPE_DIM = 16
PAGE_SIZE = 128
PEER_HEADS = 8
PEER_N_KEYS = 128
PEER_N_EXPERTS = PEER_N_KEYS * PEER_N_KEYS
PEER_HALF = 128
PEER_TOPK = 16
NORM_EPS = 1e-6
NEG_INF = float("-inf")

VMEM_LIMIT_BYTES = 56 * 1024 * 1024

ROW_TILE = 512
ATTN_Q_TILE = 256
ATTN_HEADS_PER_STEP = 2
TOPK_LANES = 128
PEER_EXPERT_TILE = 1024
PEER_TOKEN_TILE = 1024


def _params(*semantics):
    return pltpu.CompilerParams(dimension_semantics=semantics, vmem_limit_bytes=VMEM_LIMIT_BYTES)


def _rms(x, g):
    return x * lax.rsqrt(jnp.mean(x * x, axis=-1, keepdims=True) + NORM_EPS) * g


def _gelu(x):
    return x * (0.5 * (1.0 + jnp.tanh(math.sqrt(2.0 / math.pi) * (x + 0.044715 * (x * x * x)))))


def _const_spec(shape):
    zeros = (0,) * len(shape)
    return pl.BlockSpec(shape, lambda *_: zeros)


def _proj_kernel(x_ref, g1_ref, win_ref, cos_ref, sa_ref, sb_ref, gmg_ref, wmix_ref, bias_ref,
                 q1_ref, q2_ref, k32_ref, v32_ref, kb_ref, vb_ref, gmo_ref, gn_ref, *, mix_rows):
    rows = x_ref.shape[0]
    xn = _rms(x_ref[...], g1_ref[...]).astype(BF16)
    z = jnp.dot(xn, win_ref[...], preferred_element_type=F32)
    cos_t, sin_a, sin_b = cos_ref[...], sa_ref[...], sb_ref[...]
    first_half = lax.broadcasted_iota(jnp.int32, (rows, HEAD_LANES), 1) < DA_HEAD_DIM

    def rope(t):
        return t * cos_t + pltpu.roll(t, HEAD_LANES - 8, 1) * sin_a + pltpu.roll(t, 8, 1) * sin_b

    for b in range(DA_HEADS):
        sl = slice(b * HEAD_LANES, (b + 1) * HEAD_LANES)
        q = rope(z[:, sl]) * (1.0 / math.sqrt(DA_HEAD_DIM))
        q1_ref[:, sl] = jnp.where(first_half, q, 0.0).astype(BF16)
        q2_ref[:, sl] = jnp.where(first_half, 0.0, q).astype(BF16)
        k = rope(z[:, DA_WIDTH + b * HEAD_LANES:DA_WIDTH + (b + 1) * HEAD_LANES])
        v = z[:, 2 * DA_WIDTH + b * HEAD_LANES:2 * DA_WIDTH + (b + 1) * HEAD_LANES]
        k32_ref[pl.ds(b, rows, stride=DA_HEADS), :] = k
        v32_ref[pl.ds(b, rows, stride=DA_HEADS), :] = v
        kb_ref[:, sl] = k.astype(BF16)
        vb_ref[:, sl] = v.astype(BF16)

    u = _gelu(z[:, 3 * DA_WIDTH:3 * DA_WIDTH + GM_WIDTH])
    g = _gelu(z[:, 3 * DA_WIDTH + GM_WIDTH:])
    for b in range(GM_HEADS):
        sl = slice(b * HEAD_LANES, (b + 1) * HEAD_LANES)
        gn = _rms(g[:, sl], gmg_ref[:, sl])
        gn_ref[:, sl] = gn
        gnb = gn.astype(BF16)
        for r in range(rows // mix_rows):
            rs = slice(r * mix_rows, (r + 1) * mix_rows)
            s = jnp.dot(wmix_ref[b], gnb[rs], preferred_element_type=F32) + bias_ref[:, sl]
            gmo_ref[rs, sl] = (u[rs, sl] * s).astype(BF16)


def _proj(x, g1, w_in, cos_t, sin_a, sin_b, gm_g, wmix, bias):
    rows = x.shape[0]
    tile = min(ROW_TILE, rows)
    mix_rows = wmix.shape[-1]
    row_spec = lambda w: pl.BlockSpec((tile, w), lambda i: (i, 0))
    table_blocks = cos_t.shape[0] // tile
    table_spec = pl.BlockSpec((tile, HEAD_LANES), lambda i: (i % table_blocks, 0))
    head_rows_spec = pl.BlockSpec((tile * DA_HEADS, HEAD_LANES), lambda i: (i, 0))
    out_shape = [jax.ShapeDtypeStruct((rows, DA_WIDTH), BF16),
                 jax.ShapeDtypeStruct((rows, DA_WIDTH), BF16),
                 jax.ShapeDtypeStruct((rows * DA_HEADS, HEAD_LANES), F32),
                 jax.ShapeDtypeStruct((rows * DA_HEADS, HEAD_LANES), F32),
                 jax.ShapeDtypeStruct((rows, DA_WIDTH), BF16),
                 jax.ShapeDtypeStruct((rows, DA_WIDTH), BF16),
                 jax.ShapeDtypeStruct((rows, GM_WIDTH), BF16),
                 jax.ShapeDtypeStruct((rows, GM_WIDTH), F32)]
    return pl.pallas_call(
        functools.partial(_proj_kernel, mix_rows=mix_rows),
        grid=(rows // tile,),
        in_specs=[row_spec(D_MODEL), _const_spec((1, D_MODEL)), _const_spec((D_MODEL, IN_WIDTH)),
                  table_spec, table_spec, table_spec,
                  _const_spec((1, GM_WIDTH)), _const_spec(wmix.shape), _const_spec(bias.shape)],
        out_specs=[row_spec(DA_WIDTH)] * 2 + [head_rows_spec] * 2 + [row_spec(DA_WIDTH)] * 2
                  + [row_spec(GM_WIDTH)] * 2,
        out_shape=out_shape,
        compiler_params=_params("parallel"),
        name="proj",
    )(x, g1, w_in, cos_t, sin_a, sin_b, gm_g, wmix, bias)


def _attn_kernel(lam_ref, q1_ref, q2_ref, k_ref, v_ref, o_ref):
    tq = q1_ref.shape[0]
    heads = q1_ref.shape[1] // HEAD_LANES
    i = pl.program_id(2)
    head_lanes = [slice(h * HEAD_LANES, (h + 1) * HEAD_LANES) for h in range(heads)]
    qs = [jnp.concatenate([q1_ref[:, sl], q2_ref[:, sl]], axis=0) for sl in head_lanes]

    def step(j, carry, masked):
        start = pl.multiple_of(j * tq, tq)
        new = []
        for q, sl, (m, l, acc) in zip(qs, head_lanes, carry):
            kb = k_ref[pl.ds(start, tq), sl]
            vb = v_ref[pl.ds(start, tq), sl]
            s = lax.dot_general(q, kb, (((1,), (1,)), ((), ())), preferred_element_type=F32)
            if masked:
                row = lax.broadcasted_iota(jnp.int32, (2 * tq, tq), 0)
                col = lax.broadcasted_iota(jnp.int32, (2 * tq, tq), 1)
                row = jnp.where(row >= tq, row - tq, row)
                s = jnp.where(col <= row, s, NEG_INF)
            m_new = jnp.maximum(m, jnp.max(s, axis=-1, keepdims=True))
            alpha = jnp.exp(m - m_new)
            p = jnp.exp(s - m_new)
            l = alpha * l + jnp.sum(p, axis=-1, keepdims=True)
            acc = alpha * acc + jnp.dot(p.astype(BF16), vb, preferred_element_type=F32)
            new.append((m_new, l, acc))
        return tuple(new)

    init = tuple((jnp.full((2 * tq, 1), NEG_INF, F32), jnp.zeros((2 * tq, 1), F32),
                  jnp.zeros((2 * tq, HEAD_LANES), F32)) for _ in range(heads))
    carry = lax.fori_loop(0, i, lambda j, c: step(j, c, False), init)
    for sl, (_, l, acc) in zip(head_lanes, step(i, carry, True)):
        o = acc / l
        o_ref[:, sl] = o[:tq] - lam_ref[0] * o[tq:]


def _prompt_attention(lam, q1, q2, kb, vb, batch, seq):
    tq = min(ATTN_Q_TILE, seq)
    nq = seq // tq
    width = ATTN_HEADS_PER_STEP * HEAD_LANES
    q_spec = pl.BlockSpec((tq, width), lambda b, h, i: (b * nq + i, h))
    kv_spec = pl.BlockSpec((seq, width), lambda b, h, i: (b, h))
    return pl.pallas_call(
        _attn_kernel,
        grid=(batch, DA_HEADS // ATTN_HEADS_PER_STEP, nq),
        in_specs=[pl.BlockSpec(memory_space=pltpu.SMEM), q_spec, q_spec, kv_spec, kv_spec],
        out_specs=q_spec,
        out_shape=jax.ShapeDtypeStruct((batch * seq, DA_WIDTH), F32),
        compiler_params=_params("parallel", "parallel", "arbitrary"),
        name="prompt_attn",
    )(lam, q1, q2, kb, vb)


def _decode_attn_kernel(pt_ref, lam_ref, q_ref, kn_ref, vn_ref, *rest, n_pages, n_new):
    del pt_ref
    k_pages, v_pages = rest[:n_pages], rest[n_pages:2 * n_pages]
    o_ref, k_all, v_all = rest[2 * n_pages:]
    page_rows = PAGE_SIZE * DA_HEADS
    for p in range(n_pages):
        rs = slice(p * page_rows, (p + 1) * page_rows)
        k_all[rs, :] = k_pages[p][0].astype(BF16)
        v_all[rs, :] = v_pages[p][0].astype(BF16)
    q = q_ref[0]
    nrow = q.shape[0]
    contract_last = (((1,), (1,)), ((), ()))
    s_past = lax.dot_general(q, k_all[...], contract_last, preferred_element_type=F32)
    s_new = lax.dot_general(q, kn_ref[0], contract_last, preferred_element_type=F32)

    def row_col(shape):
        row = lax.broadcasted_iota(jnp.int32, shape, 0)
        col = lax.broadcasted_iota(jnp.int32, shape, 1)
        return (row // n_new) % DA_HEADS, row % n_new, col % DA_HEADS, col // DA_HEADS

    q_head, _, k_head, _ = row_col(s_past.shape)
    s_past = jnp.where(q_head == k_head, s_past, NEG_INF)
    q_head, q_tok, k_head, k_tok = row_col(s_new.shape)
    s_new = jnp.where(q_head == k_head, jnp.where(k_tok <= q_tok, s_new, NEG_INF), NEG_INF)
    m = jnp.maximum(jnp.max(s_past, axis=-1, keepdims=True), jnp.max(s_new, axis=-1, keepdims=True))
    p_past = jnp.exp(s_past - m)
    p_new = jnp.exp(s_new - m)
    inv = 1.0 / (jnp.sum(p_past, axis=-1, keepdims=True) + jnp.sum(p_new, axis=-1, keepdims=True))
    half = nrow // 2
    lam = lam_ref[0]
    a_past = (p_past[:half] * inv[:half] - lam * (p_past[half:] * inv[half:])).astype(BF16)
    a_new = (p_new[:half] * inv[:half] - lam * (p_new[half:] * inv[half:])).astype(BF16)
    o_ref[0] = (jnp.dot(a_past, v_all[...], preferred_element_type=F32)
                + jnp.dot(a_new, vn_ref[0], preferred_element_type=F32))


def _decode_attention(page_table, lam, q_rows, k_new, v_new, cache_k, cache_v):
    nb, n_pages = page_table.shape
    n_new = k_new.shape[1] // DA_HEADS
    page_rows = PAGE_SIZE * DA_HEADS
    page_spec = lambda p: pl.BlockSpec((1, page_rows, HEAD_LANES), lambda b, pt: (pt[b * n_pages + p], 0, 0))
    per_b = lambda r: pl.BlockSpec((1, r, HEAD_LANES), lambda b, pt: (b, 0, 0))
    grid_spec = pltpu.PrefetchScalarGridSpec(
        num_scalar_prefetch=1,
        grid=(nb,),
        in_specs=[pl.BlockSpec(memory_space=pltpu.SMEM), per_b(q_rows.shape[1]),
                  per_b(n_new * DA_HEADS), per_b(n_new * DA_HEADS)]
                 + [page_spec(p) for p in range(n_pages)] * 2,
        out_specs=per_b(n_new * DA_HEADS),
        scratch_shapes=[pltpu.VMEM((n_pages * page_rows, HEAD_LANES), BF16)] * 2,
    )
    return pl.pallas_call(
        functools.partial(_decode_attn_kernel, n_pages=n_pages, n_new=n_new),
        grid_spec=grid_spec,
        out_shape=jax.ShapeDtypeStruct((nb, n_new * DA_HEADS, HEAD_LANES), F32),
        compiler_params=_params("arbitrary"),
        name="decode_attn",
    )(page_table.reshape(-1), lam, q_rows, k_new, v_new,
      *([cache_k] * n_pages), *([cache_v] * n_pages))


def _merge_kernel(ao_ref, gmo_ref, x_ref, sg_ref, wo_ref, g2_ref, wpq_ref, kbig_ref,
                  h_ref, tn_ref, s_ref, *, attn_scale):
    parts = []
    for b in range(DA_HEADS):
        sl = slice(b * HEAD_LANES, (b + 1) * HEAD_LANES)
        parts.append((_rms(ao_ref[:, sl], sg_ref[...]) * attn_scale).astype(BF16))
    cat = jnp.concatenate(parts + [gmo_ref[...]], axis=1)
    h = x_ref[...] + jnp.dot(cat, wo_ref[...], preferred_element_type=F32)
    h_ref[...] = h
    tn = _rms(h, g2_ref[...]).astype(BF16)
    tn_ref[...] = tn
    qp = jnp.dot(tn, wpq_ref[...], preferred_element_type=F32).astype(BF16)
    width = PEER_HEADS * PEER_HALF
    for c in range(2):
        s_ref[c] = lax.dot_general(kbig_ref[c], qp[:, c * width:(c + 1) * width],
                                   (((1,), (1,)), ((), ())), preferred_element_type=F32)


def _merge(ao, gmo, x, subln_g, w_o, g2, wpq, kbig, attn_scale):
    rows = x.shape[0]
    tile = min(ROW_TILE, rows)
    width = PEER_HEADS * PEER_HALF
    row_spec = lambda w: pl.BlockSpec((tile, w), lambda i: (i, 0))
    return pl.pallas_call(
        functools.partial(_merge_kernel, attn_scale=attn_scale),
        grid=(rows // tile,),
        in_specs=[row_spec(DA_WIDTH), row_spec(GM_WIDTH), row_spec(D_MODEL), _const_spec((1, HEAD_LANES)),
                  _const_spec((D_MODEL, D_MODEL)), _const_spec((1, D_MODEL)),
                  _const_spec((D_MODEL, 2 * width)), _const_spec((2, PEER_N_KEYS * PEER_HEADS, width))],
        out_specs=[row_spec(D_MODEL), row_spec(D_MODEL),
                   pl.BlockSpec((2, PEER_N_KEYS * PEER_HEADS, tile), lambda i: (0, 0, i))],
        out_shape=[jax.ShapeDtypeStruct((rows, D_MODEL), F32),
                   jax.ShapeDtypeStruct((rows, D_MODEL), BF16),
                   jax.ShapeDtypeStruct((2, PEER_N_KEYS * PEER_HEADS, rows), F32)],
        compiler_params=_params("parallel"),
        name="merge",
    )(ao, gmo, x, subln_g, w_o, g2, wpq, kbig)


def _young_candidates():
    return [(p, q) for p in range(PEER_TOPK) for q in range(PEER_TOPK) if (p + 1) * (q + 1) <= PEER_TOPK]


def _topk_kernel(s_ref, r2_ref, e2_ref, n_ref, c_ref, work_ref, top_ref, idx_ref, young_ref, stage_ref):
    nk = PEER_N_KEYS
    tile = s_ref.shape[2:]
    one, zero = jnp.ones(tile, F32), jnp.zeros(tile, F32)

    def extract_distinct(c):
        def round_(r, bound):
            tree = [jnp.where(s_ref[c, j] < bound, s_ref[c, j], NEG_INF) for j in range(nk)]
            while len(tree) > 1:
                tree = [jnp.maximum(tree[t], tree[t + 1]) for t in range(0, len(tree), 2)]
            top_ref[c, r] = tree[0]
            return tree[0]

        last = lax.fori_loop(0, PEER_TOPK, round_, jnp.full(tile, float("inf"), F32))
        reached = zero
        for j in range(nk):
            reached = reached + jnp.where(s_ref[c, j] >= last, one, zero)
        return reached

    def extract(c):
        for j in range(nk):
            work_ref[j] = s_ref[c, j]

        def round_(r, _):
            vals = [work_ref[j] for j in range(nk)]
            idxs = [None] * nk
            width = 1
            while len(vals) > 1:
                nv, ni = [], []
                for t in range(0, len(vals), 2):
                    a, b = vals[t], vals[t + 1]
                    take_b = b > a
                    ia = idxs[t] if idxs[t] is not None else float(t * width)
                    ib = idxs[t + 1] if idxs[t + 1] is not None else float((t + 1) * width)
                    nv.append(jnp.maximum(a, b))
                    ni.append(jnp.where(take_b, ib, ia))
                vals, idxs = nv, ni
                width *= 2
            best, best_idx = vals[0], idxs[0]
            top_ref[c, r] = best
            idx_ref[c, r] = best_idx
            for j in range(nk):
                work_ref[j] = jnp.where(best_idx == float(j), NEG_INF, work_ref[j])
            return 0

        lax.fori_loop(0, PEER_TOPK, round_, 0)

    miscount = jnp.maximum(jnp.abs(extract_distinct(0) - float(PEER_TOPK)),
                           jnp.abs(extract_distinct(1) - float(PEER_TOPK)))
    tied = jnp.max(miscount) > 0.0

    @pl.when(tied)
    def _():
        extract(0)
        extract(1)

    a = [top_ref[0, p] for p in range(PEER_TOPK)]
    b = [top_ref[1, q] for q in range(PEER_TOPK)]
    cands = _young_candidates()
    cand = {pq: a[pq[0]] + b[pq[1]] for pq in cands}
    n_of_p = [zero] * PEER_TOPK
    ea = [jnp.exp(a[p] - a[0]) for p in range(PEER_TOPK)]
    eb = [jnp.exp(b[q] - b[0]) for q in range(PEER_TOPK)]
    z = zero
    for (p, q) in cands:
        fixed = 0
        beaten = zero
        for (p2, q2) in cands:
            if (p2, q2) == (p, q):
                continue
            if p2 <= p and q2 <= q:
                fixed += 1
            elif p2 >= p and q2 >= q:
                continue
            elif p2 * PEER_TOPK + q2 < p * PEER_TOPK + q:
                beaten = beaten + jnp.where(cand[(p2, q2)] >= cand[(p, q)], one, zero)
            else:
                beaten = beaten + jnp.where(cand[(p2, q2)] > cand[(p, q)], one, zero)
        sel = jnp.where(beaten + float(fixed) < float(PEER_TOPK), one, zero)
        n_of_p[p] = n_of_p[p] + sel
        z = z + sel * (ea[p] * eb[q])
    inv_z = 1.0 / z
    for p in range(PEER_TOPK):
        young_ref[p] = n_of_p[p]

    def scatter(hit_a, hit_b):
        for j in range(nk):
            n_j = zero
            r_j = jnp.full(tile, float(nk - 1), F32)
            for p in range(PEER_TOPK):
                n_j = jnp.where(hit_a(p, j), young_ref[p], n_j)
                r_j = jnp.where(hit_b(p, j), float(p), r_j)
            stage_ref[0, j * PEER_HEADS:(j + 1) * PEER_HEADS, :] = n_j
            stage_ref[1, j * PEER_HEADS:(j + 1) * PEER_HEADS, :] = r_j

    @pl.when(jnp.logical_not(tied))
    def _():
        scatter(lambda p, j: top_ref[0, p] == s_ref[0, j], lambda p, j: top_ref[1, p] == s_ref[1, j])

    @pl.when(tied)
    def _():
        scatter(lambda p, j: idx_ref[0, p] == float(j), lambda p, j: idx_ref[1, p] == float(j))

    for j in range(nk):
        stage_ref[2, j * PEER_HEADS:(j + 1) * PEER_HEADS, :] = jnp.exp(s_ref[0, j] - a[0]) * inv_z
        stage_ref[3, j * PEER_HEADS:(j + 1) * PEER_HEADS, :] = jnp.exp(s_ref[1, j] - b[0])
    for h in range(PEER_HEADS):
        of_head = pl.ds(h, nk, stride=PEER_HEADS)
        n_ref[h] = stage_ref[0, of_head, :]
        r2_ref[h] = stage_ref[1, of_head, :].astype(BF16)
        c_ref[h] = stage_ref[2, of_head, :]
        e2_ref[h] = stage_ref[3, of_head, :].astype(BF16)


def _topk(scores):
    tokens = scores.shape[-1]
    lanes = TOPK_LANES
    tab_spec = pl.BlockSpec((PEER_HEADS, PEER_N_KEYS, lanes), lambda i: (0, 0, i))
    tab = lambda dt: jax.ShapeDtypeStruct((PEER_HEADS, PEER_N_KEYS, tokens), dt)
    return pl.pallas_call(
        _topk_kernel,
        grid=(tokens // lanes,),
        in_specs=[pl.BlockSpec((2, PEER_N_KEYS, PEER_HEADS, lanes), lambda i: (0, 0, 0, i))],
        out_specs=[tab_spec] * 4,
        out_shape=[tab(BF16), tab(BF16), tab(F32), tab(F32)],
        scratch_shapes=[pltpu.VMEM((PEER_N_KEYS, PEER_HEADS, lanes), F32),
                        pltpu.VMEM((2, PEER_TOPK, PEER_HEADS, lanes), F32),
                        pltpu.VMEM((2, PEER_TOPK, PEER_HEADS, lanes), F32),
                        pltpu.VMEM((PEER_TOPK, PEER_HEADS, lanes), F32),
                        pltpu.VMEM((4, PEER_N_KEYS * PEER_HEADS, lanes), F32)],
        compiler_params=_params("parallel"),
        name="peer_topk",
    )(scores)


def _peer_kernel(tn_ref, pu_ref, pvt_ref, r2_ref, e2_ref, n_ref, c_ref, acc_ref, at_ref, hid_ref):
    nk = PEER_N_KEYS
    keys_per_step = pu_ref.shape[0] // nk

    @pl.when(pl.program_id(1) == 0)
    def _():
        acc_ref[...] = jnp.zeros_like(acc_ref)

    at_ref[...] = lax.dot_general(pu_ref[...], tn_ref[...], (((1,), (1,)), ((), ())),
                                  preferred_element_type=F32)
    for ii in range(keys_per_step):
        rs = slice(ii * nk, (ii + 1) * nk)
        gate = jnp.zeros((nk, tn_ref.shape[0]), BF16)
        for h in range(PEER_HEADS):
            n = n_ref[h, ii:ii + 1, :].astype(BF16)
            c = c_ref[h, ii:ii + 1, :].astype(BF16)
            gate = gate + jnp.where(r2_ref[h] < n, e2_ref[h] * c, jnp.zeros_like(gate))
        hid_ref[rs, :] = gate * _gelu(at_ref[rs, :]).astype(BF16)
    acc_ref[...] += jnp.dot(pvt_ref[...], hid_ref[...], preferred_element_type=F32)


def _peer(tn, pu, pvt, r2, e2, ntab, ctab):
    tokens = tn.shape[0]
    tt = min(PEER_TOKEN_TILE, tokens)
    et = PEER_EXPERT_TILE
    whole_tab = pl.BlockSpec((PEER_HEADS, PEER_N_KEYS, tt), lambda t, e: (0, 0, t))
    step_tab = pl.BlockSpec((PEER_HEADS, et // PEER_N_KEYS, tt), lambda t, e: (0, e, t))
    return pl.pallas_call(
        _peer_kernel,
        grid=(tokens // tt, PEER_N_EXPERTS // et),
        in_specs=[pl.BlockSpec((tt, D_MODEL), lambda t, e: (t, 0)),
                  pl.BlockSpec((et, D_MODEL), lambda t, e: (e, 0)),
                  pl.BlockSpec((D_MODEL, et), lambda t, e: (0, e)),
                  whole_tab, whole_tab, step_tab, step_tab],
        out_specs=pl.BlockSpec((D_MODEL, tt), lambda t, e: (0, t)),
        out_shape=jax.ShapeDtypeStruct((D_MODEL, tokens), F32),
        scratch_shapes=[pltpu.VMEM((et, tt), F32), pltpu.VMEM((et, tt), BF16)],
        compiler_params=_params("parallel", "arbitrary"),
        name="peer_experts",
    )(tn, pu, pvt, r2, e2, ntab, ctab)


def _final_kernel(h_ref, mix_ref, gf_ref, y_ref):
    y_ref[...] = _rms(h_ref[...] + jnp.transpose(mix_ref[...]), gf_ref[...])


def _final(h, mix_t, final_g):
    rows = h.shape[0]
    tile = min(ROW_TILE, rows)
    row_spec = pl.BlockSpec((tile, D_MODEL), lambda i: (i, 0))
    return pl.pallas_call(
        _final_kernel,
        grid=(rows // tile,),
        in_specs=[row_spec, pl.BlockSpec((D_MODEL, tile), lambda i: (0, i)), _const_spec((1, D_MODEL))],
        out_specs=row_spec,
        out_shape=jax.ShapeDtypeStruct((rows, D_MODEL), F32),
        compiler_params=_params("parallel"),
        name="final_norm",
    )(h, mix_t, final_g)


def _rope_tables(pos):
    half = ROPE_DIM // 2
    inv = ROPE_THETA ** (-jnp.arange(half, dtype=F32) * 2.0 / ROPE_DIM)
    d = jnp.arange(HEAD_LANES) % DA_HEAD_DIM
    ang = pos[:, None] * inv[d % half][None, :]
    cos_t = jnp.where(d < ROPE_DIM, jnp.cos(ang), 1.0)
    sin_a = jnp.where(d < half, -jnp.sin(ang), 0.0)
    sin_b = jnp.where((d >= half) & (d < ROPE_DIM), jnp.sin(ang), 0.0)
    return cos_t, sin_a, sin_b


def _row_pipeline(x, rope_tabs, wmix, bias, attend, weights):
    (g1, w_in, gm_g, subln_g, w_o, g2, wpq, kbig, pu, pvt, final_g, attn_scale) = weights
    q1, q2, k32, v32, kb, vb, gmo, gn = _proj(x, g1, w_in, *rope_tabs, gm_g, wmix, bias)
    ao = attend(q1, q2, k32, v32, kb, vb)
    h, tn, scores = _merge(ao, gmo, x, subln_g, w_o, g2, wpq, kbig, attn_scale)
    rows = x.shape[0]
    r2, e2, ntab, ctab = _topk(scores.reshape(2, PEER_N_KEYS, PEER_HEADS, rows))
    y = _final(h, _peer(tn, pu, pvt, r2, e2, ntab, ctab), final_g)
    return y, k32, v32, gn


def kernel(x_prompt, x_sample, cache_k, cache_v, page_table, norm1_g, w_in, gm_norm_g, gm_ws, gm_b,
           da_lq1, da_lk1, da_lq2, da_lk2, da_subln_g, w_o, norm2_g, peer_wq, peer_keys, peer_u,
           peer_v, final_g):
    batch, seq, _ = x_prompt.shape
    nb, n_new, _ = x_sample.shape
    n_pages = page_table.shape[1]
    past = n_pages * PAGE_SIZE
    assert w_in.shape[0] == 1 and seq % CHUNK == 0 and seq % min(ROW_TILE, batch * seq) == 0

    lam_init = 0.8 - 0.6 * math.exp(-0.3 * 0)
    dots = lambda a, b: jnp.exp(jnp.sum(a.astype(F32) * b.astype(F32)))
    lam = (dots(da_lq1[0], da_lk1[0]) - dots(da_lq2[0], da_lk2[0]) + lam_init).reshape(1).astype(F32)

    eye = jnp.eye(PEER_HEADS, dtype=F32)
    kbig = jnp.einsum("hcjd,hg->cjhgd", peer_keys[0], eye).reshape(
        2, PEER_N_KEYS * PEER_HEADS, PEER_HEADS * PEER_HALF).astype(BF16)
    wpq = peer_wq[0].reshape(D_MODEL, PEER_HEADS, 2, PEER_HALF).transpose(0, 2, 1, 3).reshape(
        D_MODEL, 2 * PEER_HEADS * PEER_HALF).astype(BF16)
    weights = (norm1_g[0][None], w_in[0].astype(BF16), gm_norm_g[0].reshape(1, GM_WIDTH),
               da_subln_g[0][None], w_o[0].astype(BF16), norm2_g[0][None], wpq, kbig,
               peer_u[0].astype(BF16), peer_v[0].T.astype(BF16), final_g[None], 1.0 - lam_init)

    xp = x_prompt.reshape(batch * seq, D_MODEL)
    tabs_p = _rope_tables(jnp.arange(seq, dtype=F32))
    wmix_p = jnp.tril(gm_ws[0]).astype(BF16)
    bias_p = jnp.repeat(jnp.transpose(gm_b[0]), HEAD_LANES, axis=1)
    attend_p = lambda q1, q2, k32, v32, kb, vb: _prompt_attention(lam, q1, q2, kb, vb, batch, seq)
    y_p, k_p, v_p, _ = _row_pipeline(xp, tabs_p, wmix_p, bias_p, attend_p, weights)

    xs = x_sample.reshape(nb * n_new, D_MODEL)
    tabs_s = _rope_tables(jnp.tile(past + jnp.arange(n_new, dtype=F32), nb))
    rows_s = min(ROW_TILE, nb * n_new)
    w_new = jnp.tril(gm_ws[0][:, :n_new, :n_new])
    r = jnp.arange(rows_s)
    pick = (r[:, None] % n_new == jnp.arange(n_new)[None, :]).astype(F32)
    w_rows = jnp.einsum("ri,hij,cj->hrc", pick, w_new, pick, precision=lax.Precision.HIGHEST)
    wmix_s = jnp.where(r[:, None] // n_new == r[None, :] // n_new, w_rows, 0.0).astype(BF16)
    bias_s = jnp.tile(jnp.repeat(jnp.transpose(gm_b[0][:, :n_new]), HEAD_LANES, axis=1), (rows_s // n_new, 1))
    ck = cache_k[0].reshape(-1, PAGE_SIZE * DA_HEADS, HEAD_LANES)
    cv = cache_v[0].reshape(-1, PAGE_SIZE * DA_HEADS, HEAD_LANES)

    def attend_s(q1, q2, k32, v32, kb, vb):
        per_head = lambda t: jnp.transpose(t.reshape(nb, n_new, DA_HEADS, HEAD_LANES), (0, 2, 1, 3))
        q = jnp.stack([per_head(q1), per_head(q2)], axis=1).reshape(nb, 2 * DA_HEADS * n_new, HEAD_LANES)
        new_rows = lambda t: t.reshape(nb, n_new * DA_HEADS, HEAD_LANES).astype(BF16)
        ao = _decode_attention(page_table, lam, q, new_rows(k32), new_rows(v32), ck, cv)
        ao = jnp.transpose(ao.reshape(nb, DA_HEADS, n_new, HEAD_LANES), (0, 2, 1, 3))
        return ao.reshape(nb * n_new, DA_WIDTH)

    y_s, k_s, v_s, gn_s = _row_pipeline(xs, tabs_s, wmix_s, bias_s, attend_s, weights)

    head_shape = lambda b, s: (1, b, s, DA_HEADS, HEAD_LANES)
    return (y_p.reshape(batch, seq, D_MODEL), y_s.reshape(nb, n_new, D_MODEL),
            k_p.reshape(head_shape(batch, seq)), v_p.reshape(head_shape(batch, seq)),
            k_s.reshape(head_shape(nb, n_new)), v_s.reshape(head_shape(nb, n_new)),
            gn_s.reshape(1, nb, n_new, GM_HEADS, HEAD_LANES))
```

```python
import functools
import math

import jax
import jax.numpy as jnp
from jax import lax
from jax.experimental import pallas as pl
from jax.experimental.pallas import tpu as pltpu

F32 = jnp.float32
BF16 = jnp.bfloat16

D_MODEL = 1024
DA_WIDTH = 512
DA_HEADS = 4
HEAD_LANES = 128
DA_HEAD_DIM = 64
GM_WIDTH = 512
GM_HEADS = 4
CHUNK = 128
IN_WIDTH = 3 * DA_WIDTH + 2 * GM_WIDTH
ROPE_THETA = 500000.0
ROPE_DIM = 16
PAGE_SIZE = 128
PEER_HEADS = 8
PEER_N_KEYS = 128
PEER_N_EXPERTS = PEER_N_KEYS * PEER_N_KEYS
PEER_HALF = 128
PEER_TOPK = 16
NORM_EPS = 1e-6
NEG_INF = float("-inf")

VMEM_LIMIT_BYTES = 56 * 1024 * 1024

ROW_TILE = 512
ATTN_Q_TILE = 256
ATTN_HEADS_PER_STEP = 4
TOPK_LANES = 128
PEER_EXPERT_TILE = 1024
PEER_TOKEN_TILE = 1024


def _params(*semantics):
    return pltpu.CompilerParams(dimension_semantics=semantics, vmem_limit_bytes=VMEM_LIMIT_BYTES)


def _rms(x, g):
    return x * lax.rsqrt(jnp.mean(x * x, axis=-1, keepdims=True) + NORM_EPS) * g


def _gelu(x):
    return x * (0.5 * (1.0 + jnp.tanh(math.sqrt(2.0 / math.pi) * (x + 0.044715 * (x * x * x)))))


def _const_spec(shape):
    zeros = (0,) * len(shape)
    return pl.BlockSpec(shape, lambda *_: zeros)


def _proj_kernel(x_ref, g1_ref, win_ref, cos_ref, sa_ref, sb_ref, gmg_ref, wmix_ref, bias_ref,
                 q1_ref, q2_ref, k32_ref, v32_ref, kb_ref, vb_ref, gmo_ref, gn_ref, *, mix_rows):
    rows = x_ref.shape[0]
    xn = _rms(x_ref[...], g1_ref[...]).astype(BF16)
    z = jnp.dot(xn, win_ref[...], preferred_element_type=F32)
    cos_t, sin_a, sin_b = cos_ref[...], sa_ref[...], sb_ref[...]
    first_half = lax.broadcasted_iota(jnp.int32, (rows, HEAD_LANES), 1) < DA_HEAD_DIM

    def rope(t):
        return t * cos_t + pltpu.roll(t, HEAD_LANES - 8, 1) * sin_a + pltpu.roll(t, 8, 1) * sin_b

    for b in range(DA_HEADS):
        sl = slice(b * HEAD_LANES, (b + 1) * HEAD_LANES)
        q = rope(z[:, sl]) * (1.0 / math.sqrt(DA_HEAD_DIM))
        q1_ref[:, sl] = jnp.where(first_half, q, 0.0).astype(BF16)
        q2_ref[:, sl] = jnp.where(first_half, 0.0, q).astype(BF16)
        k = rope(z[:, DA_WIDTH + b * HEAD_LANES:DA_WIDTH + (b + 1) * HEAD_LANES])
        v = z[:, 2 * DA_WIDTH + b * HEAD_LANES:2 * DA_WIDTH + (b + 1) * HEAD_LANES]
        k32_ref[pl.ds(b, rows, stride=DA_HEADS), :] = k
        v32_ref[pl.ds(b, rows, stride=DA_HEADS), :] = v
        kb_ref[:, sl] = k.astype(BF16)
        vb_ref[:, sl] = v.astype(BF16)

    u = _gelu(z[:, 3 * DA_WIDTH:3 * DA_WIDTH + GM_WIDTH])
    g = _gelu(z[:, 3 * DA_WIDTH + GM_WIDTH:])
    for b in range(GM_HEADS):
        sl = slice(b * HEAD_LANES, (b + 1) * HEAD_LANES)
        gn = _rms(g[:, sl], gmg_ref[:, sl])
        gn_ref[:, sl] = gn
        gnb = gn.astype(BF16)
        for r in range(rows // mix_rows):
            rs = slice(r * mix_rows, (r + 1) * mix_rows)
            s = jnp.dot(wmix_ref[b], gnb[rs], preferred_element_type=F32) + bias_ref[:, sl]
            gmo_ref[rs, sl] = (u[rs, sl] * s).astype(BF16)


def _proj(x, g1, w_in, cos_t, sin_a, sin_b, gm_g, wmix, bias):
    rows = x.shape[0]
    tile = min(ROW_TILE, rows)
    mix_rows = wmix.shape[-1]
    row_spec = lambda w: pl.BlockSpec((tile, w), lambda i: (i, 0))
    table_blocks = cos_t.shape[0] // tile
    table_spec = pl.BlockSpec((tile, HEAD_LANES), lambda i: (i % table_blocks, 0))
    head_rows_spec = pl.BlockSpec((tile * DA_HEADS, HEAD_LANES), lambda i: (i, 0))
    out_shape = [jax.ShapeDtypeStruct((rows, DA_WIDTH), BF16),
                 jax.ShapeDtypeStruct((rows, DA_WIDTH), BF16),
                 jax.ShapeDtypeStruct((rows * DA_HEADS, HEAD_LANES), F32),
                 jax.ShapeDtypeStruct((rows * DA_HEADS, HEAD_LANES), F32),
                 jax.ShapeDtypeStruct((rows, DA_WIDTH), BF16),
                 jax.ShapeDtypeStruct((rows, DA_WIDTH), BF16),
                 jax.ShapeDtypeStruct((rows, GM_WIDTH), BF16),
                 jax.ShapeDtypeStruct((rows, GM_WIDTH), F32)]
    return pl.pallas_call(
        functools.partial(_proj_kernel, mix_rows=mix_rows),
        grid=(rows // tile,),
        in_specs=[row_spec(D_MODEL), _const_spec((1, D_MODEL)), _const_spec((D_MODEL, IN_WIDTH)),
                  table_spec, table_spec, table_spec,
                  _const_spec((1, GM_WIDTH)), _const_spec(wmix.shape), _const_spec(bias.shape)],
        out_specs=[row_spec(DA_WIDTH)] * 2 + [head_rows_spec] * 2 + [row_spec(DA_WIDTH)] * 2
                  + [row_spec(GM_WIDTH)] * 2,
        out_shape=out_shape,
        compiler_params=_params("parallel"),
        name="proj",
    )(x, g1, w_in, cos_t, sin_a, sin_b, gm_g, wmix, bias)


def _attn_kernel(lam_ref, q1_ref, q2_ref, k_ref, v_ref, o_ref):
    tq = q1_ref.shape[0]
    heads = q1_ref.shape[1] // HEAD_LANES
    i = pl.program_id(2)
    head_lanes = [slice(h * HEAD_LANES, (h + 1) * HEAD_LANES) for h in range(heads)]
    qs = [jnp.concatenate([q1_ref[:, sl], q2_ref[:, sl]], axis=0) for sl in head_lanes]

    def step(j, carry, masked):
        start = pl.multiple_of(j * tq, tq)
        new = []
        for q, sl, (m, l, acc) in zip(qs, head_lanes, carry):
            kb = k_ref[pl.ds(start, tq), sl]
            vb = v_ref[pl.ds(start, tq), sl]
            s = lax.dot_general(q, kb, (((1,), (1,)), ((), ())), preferred_element_type=F32)
            if masked:
                row = lax.broadcasted_iota(jnp.int32, (2 * tq, tq), 0)
                col = lax.broadcasted_iota(jnp.int32, (2 * tq, tq), 1)
                row = jnp.where(row >= tq, row - tq, row)
                s = jnp.where(col <= row, s, NEG_INF)
            m_new = jnp.maximum(m, jnp.max(s, axis=-1, keepdims=True))
            alpha = jnp.exp(m - m_new)
            p = jnp.exp(s - m_new)
            l = alpha * l + jnp.sum(p, axis=-1, keepdims=True)
            acc = alpha * acc + jnp.dot(p.astype(BF16), vb, preferred_element_type=F32)
            new.append((m_new, l, acc))
        return tuple(new)

    init = tuple((jnp.full((2 * tq, 1), NEG_INF, F32), jnp.zeros((2 * tq, 1), F32),
                  jnp.zeros((2 * tq, HEAD_LANES), F32)) for _ in range(heads))
    carry = lax.fori_loop(0, i, lambda j, c: step(j, c, False), init)
    for sl, (_, l, acc) in zip(head_lanes, step(i, carry, True)):
        o = acc / l
        o_ref[:, sl] = o[:tq] - lam_ref[0] * o[tq:]


def _prompt_attention(lam, q1, q2, kb, vb, batch, seq):
    tq = min(ATTN_Q_TILE, seq)
    nq = seq // tq
    width = ATTN_HEADS_PER_STEP * HEAD_LANES
    q_spec = pl.BlockSpec((tq, width), lambda b, h, i: (b * nq + i, h))
    kv_spec = pl.BlockSpec((seq, width), lambda b, h, i: (b, h))
    return pl.pallas_call(
        _attn_kernel,
        grid=(batch, DA_HEADS // ATTN_HEADS_PER_STEP, nq),
        in_specs=[pl.BlockSpec(memory_space=pltpu.SMEM), q_spec, q_spec, kv_spec, kv_spec],
        out_specs=q_spec,
        out_shape=jax.ShapeDtypeStruct((batch * seq, DA_WIDTH), F32),
        compiler_params=_params("parallel", "parallel", "arbitrary"),
        name="prompt_attn",
    )(lam, q1, q2, kb, vb)


def _decode_attn_kernel(pt_ref, lam_ref, q_ref, kn_ref, vn_ref, *rest, n_pages, n_new):
    del pt_ref
    k_pages, v_pages = rest[:n_pages], rest[n_pages:2 * n_pages]
    o_ref, k_all, v_all = rest[2 * n_pages:]
    page_rows = PAGE_SIZE * DA_HEADS
    for p in range(n_pages):
        rs = slice(p * page_rows, (p + 1) * page_rows)
        k_all[rs, :] = k_pages[p][0].astype(BF16)
        v_all[rs, :] = v_pages[p][0].astype(BF16)
    q = q_ref[0]
    nrow = q.shape[0]
    contract_last = (((1,), (1,)), ((), ()))
    s_past = lax.dot_general(q, k_all[...], contract_last, preferred_element_type=F32)
    s_new = lax.dot_general(q, kn_ref[0], contract_last, preferred_element_type=F32)

    def row_col(shape):
        row = lax.broadcasted_iota(jnp.int32, shape, 0)
        col = lax.broadcasted_iota(jnp.int32, shape, 1)
        return (row // n_new) % DA_HEADS, row % n_new, col % DA_HEADS, col // DA_HEADS

    q_head, _, k_head, _ = row_col(s_past.shape)
    s_past = jnp.where(q_head == k_head, s_past, NEG_INF)
    q_head, q_tok, k_head, k_tok = row_col(s_new.shape)
    s_new = jnp.where(q_head == k_head, jnp.where(k_tok <= q_tok, s_new, NEG_INF), NEG_INF)
    m = jnp.maximum(jnp.max(s_past, axis=-1, keepdims=True), jnp.max(s_new, axis=-1, keepdims=True))
    p_past = jnp.exp(s_past - m)
    p_new = jnp.exp(s_new - m)
    inv = 1.0 / (jnp.sum(p_past, axis=-1, keepdims=True) + jnp.sum(p_new, axis=-1, keepdims=True))
    half = nrow // 2
    lam = lam_ref[0]
    a_past = (p_past[:half] * inv[:half] - lam * (p_past[half:] * inv[half:])).astype(BF16)
    a_new = (p_new[:half] * inv[:half] - lam * (p_new[half:] * inv[half:])).astype(BF16)
    o_ref[0] = (jnp.dot(a_past, v_all[...], preferred_element_type=F32)
                + jnp.dot(a_new, vn_ref[0], preferred_element_type=F32))


def _decode_attention(page_table, lam, q_rows, k_new, v_new, cache_k, cache_v):
    nb, n_pages = page_table.shape
    n_new = k_new.shape[1] // DA_HEADS
    page_rows = PAGE_SIZE * DA_HEADS
    page_spec = lambda p: pl.BlockSpec((1, page_rows, HEAD_LANES), lambda b, pt: (pt[b * n_pages + p], 0, 0))
    per_b = lambda r: pl.BlockSpec((1, r, HEAD_LANES), lambda b, pt: (b, 0, 0))
    grid_spec = pltpu.PrefetchScalarGridSpec(
        num_scalar_prefetch=1,
        grid=(nb,),
        in_specs=[pl.BlockSpec(memory_space=pltpu.SMEM), per_b(q_rows.shape[1]),
                  per_b(n_new * DA_HEADS), per_b(n_new * DA_HEADS)]
                 + [page_spec(p) for p in range(n_pages)] * 2,
        out_specs=per_b(n_new * DA_HEADS),
        scratch_shapes=[pltpu.VMEM((n_pages * page_rows, HEAD_LANES), BF16)] * 2,
    )
    return pl.pallas_call(
        functools.partial(_decode_attn_kernel, n_pages=n_pages, n_new=n_new),
        grid_spec=grid_spec,
        out_shape=jax.ShapeDtypeStruct((nb, n_new * DA_HEADS, HEAD_LANES), F32),
        compiler_params=_params("arbitrary"),
        name="decode_attn",
    )(page_table.reshape(-1), lam, q_rows, k_new, v_new,
      *([cache_k] * n_pages), *([cache_v] * n_pages))


def _merge_kernel(ao_ref, gmo_ref, x_ref, sg_ref, wo_ref, g2_ref, wpq_ref, kbig_ref,
                  h_ref, tn_ref, s_ref, *, attn_scale):
    parts = []
    for b in range(DA_HEADS):
        sl = slice(b * HEAD_LANES, (b + 1) * HEAD_LANES)
        parts.append((_rms(ao_ref[:, sl], sg_ref[...]) * attn_scale).astype(BF16))
    cat = jnp.concatenate(parts + [gmo_ref[...]], axis=1)
    h = x_ref[...] + jnp.dot(cat, wo_ref[...], preferred_element_type=F32)
    h_ref[...] = h
    tn = _rms(h, g2_ref[...]).astype(BF16)
    tn_ref[...] = tn
    qp = jnp.dot(tn, wpq_ref[...], preferred_element_type=F32).astype(BF16)
    width = PEER_HEADS * PEER_HALF
    for c in range(2):
        s_ref[c] = lax.dot_general(kbig_ref[c], qp[:, c * width:(c + 1) * width],
                                   (((1,), (1,)), ((), ())), preferred_element_type=F32)


def _merge(ao, gmo, x, subln_g, w_o, g2, wpq, kbig, attn_scale):
    rows = x.shape[0]
    tile = min(ROW_TILE, rows)
    width = PEER_HEADS * PEER_HALF
    row_spec = lambda w: pl.BlockSpec((tile, w), lambda i: (i, 0))
    return pl.pallas_call(
        functools.partial(_merge_kernel, attn_scale=attn_scale),
        grid=(rows // tile,),
        in_specs=[row_spec(DA_WIDTH), row_spec(GM_WIDTH), row_spec(D_MODEL), _const_spec((1, HEAD_LANES)),
                  _const_spec((D_MODEL, D_MODEL)), _const_spec((1, D_MODEL)),
                  _const_spec((D_MODEL, 2 * width)), _const_spec((2, PEER_N_KEYS * PEER_HEADS, width))],
        out_specs=[row_spec(D_MODEL), row_spec(D_MODEL),
                   pl.BlockSpec((2, PEER_N_KEYS * PEER_HEADS, tile), lambda i: (0, 0, i))],
        out_shape=[jax.ShapeDtypeStruct((rows, D_MODEL), F32),
                   jax.ShapeDtypeStruct((rows, D_MODEL), BF16),
                   jax.ShapeDtypeStruct((2, PEER_N_KEYS * PEER_HEADS, rows), F32)],
        compiler_params=_params("parallel"),
        name="merge",
    )(ao, gmo, x, subln_g, w_o, g2, wpq, kbig)


def _young_candidates():
    return [(p, q) for p in range(PEER_TOPK) for q in range(PEER_TOPK) if (p + 1) * (q + 1) <= PEER_TOPK]


def _topk_kernel(s_ref, r2_ref, e2_ref, n_ref, c_ref, work_ref, top_ref, idx_ref, young_ref, stage_ref):
    nk = PEER_N_KEYS
    tile = s_ref.shape[2:]
    one, zero = jnp.ones(tile, F32), jnp.zeros(tile, F32)

    def extract_distinct(c):
        def round_(r, bound):
            tree = [jnp.where(s_ref[c, j] < bound, s_ref[c, j], NEG_INF) for j in range(nk)]
            while len(tree) > 1:
                tree = [jnp.maximum(tree[t], tree[t + 1]) for t in range(0, len(tree), 2)]
            top_ref[c, r] = tree[0]
            return tree[0]

        last = lax.fori_loop(0, PEER_TOPK, round_, jnp.full(tile, float("inf"), F32))
        reached = zero
        for j in range(nk):
            reached = reached + jnp.where(s_ref[c, j] >= last, one, zero)
        return reached

    def extract(c):
        for j in range(nk):
            work_ref[j] = s_ref[c, j]

        def round_(r, _):
            vals = [work_ref[j] for j in range(nk)]
            idxs = [None] * nk
            width = 1
            while len(vals) > 1:
                nv, ni = [], []
                for t in range(0, len(vals), 2):
                    a, b = vals[t], vals[t + 1]
                    take_b = b > a
                    ia = idxs[t] if idxs[t] is not None else float(t * width)
                    ib = idxs[t + 1] if idxs[t + 1] is not None else float((t + 1) * width)
                    nv.append(jnp.maximum(a, b))
                    ni.append(jnp.where(take_b, ib, ia))
                vals, idxs = nv, ni
                width *= 2
            best, best_idx = vals[0], idxs[0]
            top_ref[c, r] = best
            idx_ref[c, r] = best_idx
            for j in range(nk):
                work_ref[j] = jnp.where(best_idx == float(j), NEG_INF, work_ref[j])
            return 0

        lax.fori_loop(0, PEER_TOPK, round_, 0)

    miscount = jnp.maximum(jnp.abs(extract_distinct(0) - float(PEER_TOPK)),
                           jnp.abs(extract_distinct(1) - float(PEER_TOPK)))
    tied = jnp.max(miscount) > 0.0

    @pl.when(tied)
    def _():
        extract(0)
        extract(1)

    a = [top_ref[0, p] for p in range(PEER_TOPK)]
    b = [top_ref[1, q] for q in range(PEER_TOPK)]
    cands = _young_candidates()
    cand = {pq: a[pq[0]] + b[pq[1]] for pq in cands}
    n_of_p = [zero] * PEER_TOPK
    ea = [jnp.exp(a[p] - a[0]) for p in range(PEER_TOPK)]
    eb = [jnp.exp(b[q] - b[0]) for q in range(PEER_TOPK)]
    z = zero
    for (p, q) in cands:
        fixed = 0
        beaten = zero
        for (p2, q2) in cands:
            if (p2, q2) == (p, q):
                continue
            if p2 <= p and q2 <= q:
                fixed += 1
            elif p2 >= p and q2 >= q:
                continue
            elif p2 * PEER_TOPK + q2 < p * PEER_TOPK + q:
                beaten = beaten + jnp.where(cand[(p2, q2)] >= cand[(p, q)], one, zero)
            else:
                beaten = beaten + jnp.where(cand[(p2, q2)] > cand[(p, q)], one, zero)
        sel = jnp.where(beaten + float(fixed) < float(PEER_TOPK), one, zero)
        n_of_p[p] = n_of_p[p] + sel
        z = z + sel * (ea[p] * eb[q])
    inv_z = 1.0 / z
    for p in range(PEER_TOPK):
        young_ref[p] = n_of_p[p]

    def scatter(hit_a, hit_b):
        for j in range(nk):
            n_j = zero
            r_j = jnp.full(tile, float(nk - 1), F32)
            for p in range(PEER_TOPK):
                n_j = jnp.where(hit_a(p, j), young_ref[p], n_j)
                r_j = jnp.where(hit_b(p, j), float(p), r_j)
            stage_ref[0, j * PEER_HEADS:(j + 1) * PEER_HEADS, :] = n_j
            stage_ref[1, j * PEER_HEADS:(j + 1) * PEER_HEADS, :] = r_j

    @pl.when(jnp.logical_not(tied))
    def _():
        scatter(lambda p, j: top_ref[0, p] == s_ref[0, j], lambda p, j: top_ref[1, p] == s_ref[1, j])

    @pl.when(tied)
    def _():
        scatter(lambda p, j: idx_ref[0, p] == float(j), lambda p, j: idx_ref[1, p] == float(j))

    for j in range(nk):
        stage_ref[2, j * PEER_HEADS:(j + 1) * PEER_HEADS, :] = jnp.exp(s_ref[0, j] - a[0]) * inv_z
        stage_ref[3, j * PEER_HEADS:(j + 1) * PEER_HEADS, :] = jnp.exp(s_ref[1, j] - b[0])
    for h in range(PEER_HEADS):
        of_head = pl.ds(h, nk, stride=PEER_HEADS)
        n_ref[h] = stage_ref[0, of_head, :]
        r2_ref[h] = stage_ref[1, of_head, :].astype(BF16)
        c_ref[h] = stage_ref[2, of_head, :]
        e2_ref[h] = stage_ref[3, of_head, :].astype(BF16)


def _topk(scores):
    tokens = scores.shape[-1]
    lanes = TOPK_LANES
    tab_spec = pl.BlockSpec((PEER_HEADS, PEER_N_KEYS, lanes), lambda i: (0, 0, i))
    tab = lambda dt: jax.ShapeDtypeStruct((PEER_HEADS, PEER_N_KEYS, tokens), dt)
    return pl.pallas_call(
        _topk_kernel,
        grid=(tokens // lanes,),
        in_specs=[pl.BlockSpec((2, PEER_N_KEYS, PEER_HEADS, lanes), lambda i: (0, 0, 0, i))],
        out_specs=[tab_spec] * 4,
        out_shape=[tab(BF16), tab(BF16), tab(F32), tab(F32)],
        scratch_shapes=[pltpu.VMEM((PEER_N_KEYS, PEER_HEADS, lanes), F32),
                        pltpu.VMEM((2, PEER_TOPK, PEER_HEADS, lanes), F32),
                        pltpu.VMEM((2, PEER_TOPK, PEER_HEADS, lanes), F32),
                        pltpu.VMEM((PEER_TOPK, PEER_HEADS, lanes), F32),
                        pltpu.VMEM((4, PEER_N_KEYS * PEER_HEADS, lanes), F32)],
        compiler_params=_params("parallel"),
        name="peer_topk",
    )(scores)


def _peer_kernel(tn_ref, pu_ref, pvt_ref, r2_ref, e2_ref, n_ref, c_ref, acc_ref, at_ref, hid_ref):
    nk = PEER_N_KEYS
    keys_per_step = pu_ref.shape[0] // nk

    @pl.when(pl.program_id(1) == 0)
    def _():
        acc_ref[...] = jnp.zeros_like(acc_ref)

    at_ref[...] = lax.dot_general(pu_ref[...], tn_ref[...], (((1,), (1,)), ((), ())),
                                  preferred_element_type=F32)
    for ii in range(keys_per_step):
        rs = slice(ii * nk, (ii + 1) * nk)
        gate = jnp.zeros((nk, tn_ref.shape[0]), BF16)
        for h in range(PEER_HEADS):
            n = n_ref[h, ii:ii + 1, :].astype(BF16)
            c = c_ref[h, ii:ii + 1, :].astype(BF16)
            gate = gate + jnp.where(r2_ref[h] < n, e2_ref[h] * c, jnp.zeros_like(gate))
        hid_ref[rs, :] = gate * _gelu(at_ref[rs, :].astype(BF16))
    acc_ref[...] += jnp.dot(pvt_ref[...], hid_ref[...], preferred_element_type=F32)


def _peer(tn, pu, pvt, r2, e2, ntab, ctab):
    tokens = tn.shape[0]
    tt = min(PEER_TOKEN_TILE, tokens)
    et = PEER_EXPERT_TILE
    whole_tab = pl.BlockSpec((PEER_HEADS, PEER_N_KEYS, tt), lambda t, e: (0, 0, t))
    step_tab = pl.BlockSpec((PEER_HEADS, et // PEER_N_KEYS, tt), lambda t, e: (0, e, t))
    return pl.pallas_call(
        _peer_kernel,
        grid=(tokens // tt, PEER_N_EXPERTS // et),
        in_specs=[pl.BlockSpec((tt, D_MODEL), lambda t, e: (t, 0)),
                  pl.BlockSpec((et, D_MODEL), lambda t, e: (e, 0)),
                  pl.BlockSpec((D_MODEL, et), lambda t, e: (0, e)),
                  whole_tab, whole_tab, step_tab, step_tab],
        out_specs=pl.BlockSpec((D_MODEL, tt), lambda t, e: (0, t)),
        out_shape=jax.ShapeDtypeStruct((D_MODEL, tokens), F32),
        scratch_shapes=[pltpu.VMEM((et, tt), F32), pltpu.VMEM((et, tt), BF16)],
        compiler_params=_params("parallel", "arbitrary"),
        name="peer_experts",
    )(tn, pu, pvt, r2, e2, ntab, ctab)


def _final_kernel(h_ref, mix_ref, gf_ref, y_ref):
    y_ref[...] = _rms(h_ref[...] + jnp.transpose(mix_ref[...]), gf_ref[...])


def _final(h, mix_t, final_g):
    rows = h.shape[0]
    tile = min(ROW_TILE, rows)
    row_spec = pl.BlockSpec((tile, D_MODEL), lambda i: (i, 0))
    return pl.pallas_call(
        _final_kernel,
        grid=(rows // tile,),
        in_specs=[row_spec, pl.BlockSpec((D_MODEL, tile), lambda i: (0, i)), _const_spec((1, D_MODEL))],
        out_specs=row_spec,
        out_shape=jax.ShapeDtypeStruct((rows, D_MODEL), F32),
        compiler_params=_params("parallel"),
        name="final_norm",
    )(h, mix_t, final_g)


def _rope_tables(pos):
    half = ROPE_DIM // 2
    inv = ROPE_THETA ** (-jnp.arange(half, dtype=F32) * 2.0 / ROPE_DIM)
    d = jnp.arange(HEAD_LANES) % DA_HEAD_DIM
    ang = pos[:, None] * inv[d % half][None, :]
    cos_t = jnp.where(d < ROPE_DIM, jnp.cos(ang), 1.0)
    sin_a = jnp.where(d < half, -jnp.sin(ang), 0.0)
    sin_b = jnp.where((d >= half) & (d < ROPE_DIM), jnp.sin(ang), 0.0)
    return cos_t, sin_a, sin_b


def _row_pipeline(x, rope_tabs, wmix, bias, attend, weights):
    (g1, w_in, gm_g, subln_g, w_o, g2, wpq, kbig, pu, pvt, final_g, attn_scale) = weights
    q1, q2, k32, v32, kb, vb, gmo, gn = _proj(x, g1, w_in, *rope_tabs, gm_g, wmix, bias)
    ao = attend(q1, q2, k32, v32, kb, vb)
    h, tn, scores = _merge(ao, gmo, x, subln_g, w_o, g2, wpq, kbig, attn_scale)
    rows = x.shape[0]
    r2, e2, ntab, ctab = _topk(scores.reshape(2, PEER_N_KEYS, PEER_HEADS, rows))
    y = _final(h, _peer(tn, pu, pvt, r2, e2, ntab, ctab), final_g)
    return y, k32, v32, gn


def kernel(x_prompt, x_sample, cache_k, cache_v, page_table, norm1_g, w_in, gm_norm_g, gm_ws, gm_b,
           da_lq1, da_lk1, da_lq2, da_lk2, da_subln_g, w_o, norm2_g, peer_wq, peer_keys, peer_u,
           peer_v, final_g):
    batch, seq, _ = x_prompt.shape
    nb, n_new, _ = x_sample.shape
    n_pages = page_table.shape[1]
    past = n_pages * PAGE_SIZE
    assert w_in.shape[0] == 1 and seq % CHUNK == 0 and seq % min(ROW_TILE, batch * seq) == 0

    lam_init = 0.8 - 0.6 * math.exp(-0.3 * 0)
    dots = lambda a, b: jnp.exp(jnp.sum(a.astype(F32) * b.astype(F32)))
    lam = (dots(da_lq1[0], da_lk1[0]) - dots(da_lq2[0], da_lk2[0]) + lam_init).reshape(1).astype(F32)

    eye = jnp.eye(PEER_HEADS, dtype=F32)
    kbig = jnp.einsum("hcjd,hg->cjhgd", peer_keys[0], eye).reshape(
        2, PEER_N_KEYS * PEER_HEADS, PEER_HEADS * PEER_HALF).astype(BF16)
    wpq = peer_wq[0].reshape(D_MODEL, PEER_HEADS, 2, PEER_HALF).transpose(0, 2, 1, 3).reshape(
        D_MODEL, 2 * PEER_HEADS * PEER_HALF).astype(BF16)
    weights = (norm1_g[0][None], w_in[0].astype(BF16), gm_norm_g[0].reshape(1, GM_WIDTH),
               da_subln_g[0][None], w_o[0].astype(BF16), norm2_g[0][None], wpq, kbig,
               peer_u[0].astype(BF16), peer_v[0].T.astype(BF16), final_g[None], 1.0 - lam_init)

    xp = x_prompt.reshape(batch * seq, D_MODEL)
    tabs_p = _rope_tables(jnp.arange(seq, dtype=F32))
    wmix_p = jnp.tril(gm_ws[0]).astype(BF16)
    bias_p = jnp.repeat(jnp.transpose(gm_b[0]), HEAD_LANES, axis=1)
    attend_p = lambda q1, q2, k32, v32, kb, vb: _prompt_attention(lam, q1, q2, kb, vb, batch, seq)
    y_p, k_p, v_p, _ = _row_pipeline(xp, tabs_p, wmix_p, bias_p, attend_p, weights)

    xs = x_sample.reshape(nb * n_new, D_MODEL)
    tabs_s = _rope_tables(jnp.tile(past + jnp.arange(n_new, dtype=F32), nb))
    rows_s = min(ROW_TILE, nb * n_new)
    w_new = jnp.tril(gm_ws[0][:, :n_new, :n_new])
    r = jnp.arange(rows_s)
    pick = (r[:, None] % n_new == jnp.arange(n_new)[None, :]).astype(F32)
    w_rows = jnp.einsum("ri,hij,cj->hrc", pick, w_new, pick, precision=lax.Precision.HIGHEST)
    wmix_s = jnp.where(r[:, None] // n_new == r[None, :] // n_new, w_rows, 0.0).astype(BF16)
    bias_s = jnp.tile(jnp.repeat(jnp.transpose(gm_b[0][:, :n_new]), HEAD_LANES, axis=1), (rows_s // n_new, 1))
    ck = cache_k[0].reshape(-1, PAGE_SIZE * DA_HEADS, HEAD_LANES)
    cv = cache_v[0].reshape(-1, PAGE_SIZE * DA_HEADS, HEAD_LANES)

    def attend_s(q1, q2, k32, v32, kb, vb):
        per_head = lambda t: jnp.transpose(t.reshape(nb, n_new, DA_HEADS, HEAD_LANES), (0, 2, 1, 3))
        q = jnp.stack([per_head(q1), per_head(q2)], axis=1).reshape(nb, 2 * DA_HEADS * n_new, HEAD_LANES)
        new_rows = lambda t: t.reshape(nb, n_new * DA_HEADS, HEAD_LANES).astype(BF16)
        ao = _decode_attention(page_table, lam, q, new_rows(k32), new_rows(v32), ck, cv)
        ao = jnp.transpose(ao.reshape(nb, DA_HEADS, n_new, HEAD_LANES), (0, 2, 1, 3))
        return ao.reshape(nb * n_new, DA_WIDTH)

    y_s, k_s, v_s, gn_s = _row_pipeline(xs, tabs_s, wmix_s, bias_s, attend_s, weights)

    head_shape = lambda b, s: (1, b, s, DA_HEADS, HEAD_LANES)
    return (y_p.reshape(batch, seq, D_MODEL), y_s.reshape(nb, n_new, D_MODEL),
            k_p.reshape(head_shape(batch, seq)), v_p.reshape(head_shape(batch, seq)),
            k_s.reshape(head_shape(nb, n_new)), v_s.reshape(head_shape(nb, n_new)),
            gn_s.reshape(1, nb, n_new, GM_HEADS, HEAD_LANES))
```

```python
import functools
import math

import jax
import jax.numpy as jnp
from jax import lax
from jax.experimental import pallas as pl
from jax.experimental.pallas import tpu as pltpu

F32 = jnp.float32
BF16 = jnp.bfloat16

D_MODEL = 1024
DA_WIDTH = 512
DA_HEADS = 4
HEAD_LANES = 128
DA_HEAD_DIM = 64
GM_WIDTH = 512
GM_HEADS = 4
CHUNK = 128
IN_WIDTH = 3 * DA_WIDTH + 2 * GM_WIDTH
ROPE_THETA = 500000.0
ROPE_DIM = 16
PAGE_SIZE = 128
PEER_HEADS = 8
PEER_N_KEYS = 128
PEER_N_EXPERTS = PEER_N_KEYS * PEER_N_KEYS
PEER_HALF = 128
PEER_TOPK = 16
NORM_EPS = 1e-6
NEG_INF = float("-inf")

VMEM_LIMIT_BYTES = 56 * 1024 * 1024

ROW_TILE = 512
ATTN_Q_TILE = 256
ATTN_HEADS_PER_STEP = 4
TOPK_LANES = 128
PEER_EXPERT_TILE = 2048
PEER_TOKEN_TILE = 1024


def _params(*semantics):
    return pltpu.CompilerParams(dimension_semantics=semantics, vmem_limit_bytes=VMEM_LIMIT_BYTES)


def _rms(x, g):
    return x * lax.rsqrt(jnp.mean(x * x, axis=-1, keepdims=True) + NORM_EPS) * g


def _gelu(x):
    return x * (0.5 * (1.0 + jnp.tanh(math.sqrt(2.0 / math.pi) * (x + 0.044715 * (x * x * x)))))


def _const_spec(shape):
    zeros = (0,) * len(shape)
    return pl.BlockSpec(shape, lambda *_: zeros)


def _proj_kernel(x_ref, g1_ref, win_ref, cos_ref, sa_ref, sb_ref, gmg_ref, wmix_ref, bias_ref,
                 q1_ref, q2_ref, k32_ref, v32_ref, kb_ref, vb_ref, gmo_ref, gn_ref, *, mix_rows):
    rows = x_ref.shape[0]
    xn = _rms(x_ref[...], g1_ref[...]).astype(BF16)
    z = jnp.dot(xn, win_ref[...], preferred_element_type=F32)
    cos_t, sin_a, sin_b = cos_ref[...], sa_ref[...], sb_ref[...]
    first_half = lax.broadcasted_iota(jnp.int32, (rows, HEAD_LANES), 1) < DA_HEAD_DIM

    def rope(t):
        return t * cos_t + pltpu.roll(t, HEAD_LANES - 8, 1) * sin_a + pltpu.roll(t, 8, 1) * sin_b

    for b in range(DA_HEADS):
        sl = slice(b * HEAD_LANES, (b + 1) * HEAD_LANES)
        q = rope(z[:, sl]) * (1.0 / math.sqrt(DA_HEAD_DIM))
        q1_ref[:, sl] = jnp.where(first_half, q, 0.0).astype(BF16)
        q2_ref[:, sl] = jnp.where(first_half, 0.0, q).astype(BF16)
        k = rope(z[:, DA_WIDTH + b * HEAD_LANES:DA_WIDTH + (b + 1) * HEAD_LANES])
        v = z[:, 2 * DA_WIDTH + b * HEAD_LANES:2 * DA_WIDTH + (b + 1) * HEAD_LANES]
        k32_ref[pl.ds(b, rows, stride=DA_HEADS), :] = k
        v32_ref[pl.ds(b, rows, stride=DA_HEADS), :] = v
        kb_ref[:, sl] = k.astype(BF16)
        vb_ref[:, sl] = v.astype(BF16)

    u = _gelu(z[:, 3 * DA_WIDTH:3 * DA_WIDTH + GM_WIDTH])
    g = _gelu(z[:, 3 * DA_WIDTH + GM_WIDTH:])
    for b in range(GM_HEADS):
        sl = slice(b * HEAD_LANES, (b + 1) * HEAD_LANES)
        gn = _rms(g[:, sl], gmg_ref[:, sl])
        gn_ref[:, sl] = gn
        gnb = gn.astype(BF16)
        for r in range(rows // mix_rows):
            rs = slice(r * mix_rows, (r + 1) * mix_rows)
            s = jnp.dot(wmix_ref[b], gnb[rs], preferred_element_type=F32) + bias_ref[:, sl]
            gmo_ref[rs, sl] = (u[rs, sl] * s).astype(BF16)


def _proj(x, g1, w_in, cos_t, sin_a, sin_b, gm_g, wmix, bias):
    rows = x.shape[0]
    tile = min(ROW_TILE, rows)
    mix_rows = wmix.shape[-1]
    row_spec = lambda w: pl.BlockSpec((tile, w), lambda i: (i, 0))
    table_blocks = cos_t.shape[0] // tile
    table_spec = pl.BlockSpec((tile, HEAD_LANES), lambda i: (i % table_blocks, 0))
    head_rows_spec = pl.BlockSpec((tile * DA_HEADS, HEAD_LANES), lambda i: (i, 0))
    out_shape = [jax.ShapeDtypeStruct((rows, DA_WIDTH), BF16),
                 jax.ShapeDtypeStruct((rows, DA_WIDTH), BF16),
                 jax.ShapeDtypeStruct((rows * DA_HEADS, HEAD_LANES), F32),
                 jax.ShapeDtypeStruct((rows * DA_HEADS, HEAD_LANES), F32),
                 jax.ShapeDtypeStruct((rows, DA_WIDTH), BF16),
                 jax.ShapeDtypeStruct((rows, DA_WIDTH), BF16),
                 jax.ShapeDtypeStruct((rows, GM_WIDTH), BF16),
                 jax.ShapeDtypeStruct((rows, GM_WIDTH), F32)]
    return pl.pallas_call(
        functools.partial(_proj_kernel, mix_rows=mix_rows),
        grid=(rows // tile,),
        in_specs=[row_spec(D_MODEL), _const_spec((1, D_MODEL)), _const_spec((D_MODEL, IN_WIDTH)),
                  table_spec, table_spec, table_spec,
                  _const_spec((1, GM_WIDTH)), _const_spec(wmix.shape), _const_spec(bias.shape)],
        out_specs=[row_spec(DA_WIDTH)] * 2 + [head_rows_spec] * 2 + [row_spec(DA_WIDTH)] * 2
                  + [row_spec(GM_WIDTH)] * 2,
        out_shape=out_shape,
        compiler_params=_params("parallel"),
        name="proj",
    )(x, g1, w_in, cos_t, sin_a, sin_b, gm_g, wmix, bias)


def _attn_kernel(lam_ref, q1_ref, q2_ref, k_ref, v_ref, o_ref):
    tq = q1_ref.shape[0]
    heads = q1_ref.shape[1] // HEAD_LANES
    i = pl.program_id(2)
    head_lanes = [slice(h * HEAD_LANES, (h + 1) * HEAD_LANES) for h in range(heads)]
    qs = [jnp.concatenate([q1_ref[:, sl], q2_ref[:, sl]], axis=0) for sl in head_lanes]

    def step(j, carry, masked):
        start = pl.multiple_of(j * tq, tq)
        new = []
        for q, sl, (m, l, acc) in zip(qs, head_lanes, carry):
            kb = k_ref[pl.ds(start, tq), sl]
            vb = v_ref[pl.ds(start, tq), sl]
            s = lax.dot_general(q, kb, (((1,), (1,)), ((), ())), preferred_element_type=F32)
            if masked:
                row = lax.broadcasted_iota(jnp.int32, (2 * tq, tq), 0)
                col = lax.broadcasted_iota(jnp.int32, (2 * tq, tq), 1)
                row = jnp.where(row >= tq, row - tq, row)
                s = jnp.where(col <= row, s, NEG_INF)
            m_new = jnp.maximum(m, jnp.max(s, axis=-1, keepdims=True))
            alpha = jnp.exp(m - m_new)
            p = jnp.exp(s - m_new)
            l = alpha * l + jnp.sum(p, axis=-1, keepdims=True)
            acc = alpha * acc + jnp.dot(p.astype(BF16), vb, preferred_element_type=F32)
            new.append((m_new, l, acc))
        return tuple(new)

    init = tuple((jnp.full((2 * tq, 1), NEG_INF, F32), jnp.zeros((2 * tq, 1), F32),
                  jnp.zeros((2 * tq, HEAD_LANES), F32)) for _ in range(heads))
    carry = lax.fori_loop(0, i, lambda j, c: step(j, c, False), init)
    for sl, (_, l, acc) in zip(head_lanes, step(i, carry, True)):
        o = acc / l
        o_ref[:, sl] = o[:tq] - lam_ref[0] * o[tq:]


def _prompt_attention(lam, q1, q2, kb, vb, batch, seq):
    tq = min(ATTN_Q_TILE, seq)
    nq = seq // tq
    width = ATTN_HEADS_PER_STEP * HEAD_LANES
    q_spec = pl.BlockSpec((tq, width), lambda b, h, i: (b * nq + i, h))
    kv_spec = pl.BlockSpec((seq, width), lambda b, h, i: (b, h))
    return pl.pallas_call(
        _attn_kernel,
        grid=(batch, DA_HEADS // ATTN_HEADS_PER_STEP, nq),
        in_specs=[pl.BlockSpec(memory_space=pltpu.SMEM), q_spec, q_spec, kv_spec, kv_spec],
        out_specs=q_spec,
        out_shape=jax.ShapeDtypeStruct((batch * seq, DA_WIDTH), F32),
        compiler_params=_params("parallel", "parallel", "arbitrary"),
        name="prompt_attn",
    )(lam, q1, q2, kb, vb)


def _decode_attn_kernel(pt_ref, lam_ref, q_ref, kn_ref, vn_ref, *rest, n_pages, n_new):
    del pt_ref
    k_pages, v_pages = rest[:n_pages], rest[n_pages:2 * n_pages]
    o_ref, k_all, v_all = rest[2 * n_pages:]
    page_rows = PAGE_SIZE * DA_HEADS
    for p in range(n_pages):
        rs = slice(p * page_rows, (p + 1) * page_rows)
        k_all[rs, :] = k_pages[p][0].astype(BF16)
        v_all[rs, :] = v_pages[p][0].astype(BF16)
    q = q_ref[0]
    nrow = q.shape[0]
    contract_last = (((1,), (1,)), ((), ()))
    s_past = lax.dot_general(q, k_all[...], contract_last, preferred_element_type=F32)
    s_new = lax.dot_general(q, kn_ref[0], contract_last, preferred_element_type=F32)

    def row_col(shape):
        row = lax.broadcasted_iota(jnp.int32, shape, 0)
        col = lax.broadcasted_iota(jnp.int32, shape, 1)
        return (row // n_new) % DA_HEADS, row % n_new, col % DA_HEADS, col // DA_HEADS

    q_head, _, k_head, _ = row_col(s_past.shape)
    s_past = jnp.where(q_head == k_head, s_past, NEG_INF)
    q_head, q_tok, k_head, k_tok = row_col(s_new.shape)
    s_new = jnp.where(q_head == k_head, jnp.where(k_tok <= q_tok, s_new, NEG_INF), NEG_INF)
    m = jnp.maximum(jnp.max(s_past, axis=-1, keepdims=True), jnp.max(s_new, axis=-1, keepdims=True))
    p_past = jnp.exp(s_past - m)
    p_new = jnp.exp(s_new - m)
    inv = 1.0 / (jnp.sum(p_past, axis=-1, keepdims=True) + jnp.sum(p_new, axis=-1, keepdims=True))
    half = nrow // 2
    lam = lam_ref[0]
    a_past = (p_past[:half] * inv[:half] - lam * (p_past[half:] * inv[half:])).astype(BF16)
    a_new = (p_new[:half] * inv[:half] - lam * (p_new[half:] * inv[half:])).astype(BF16)
    o_ref[0] = (jnp.dot(a_past, v_all[...], preferred_element_type=F32)
                + jnp.dot(a_new, vn_ref[0], preferred_element_type=F32))


def _decode_attention(page_table, lam, q_rows, k_new, v_new, cache_k, cache_v):
    nb, n_pages = page_table.shape
    n_new = k_new.shape[1] // DA_HEADS
    page_rows = PAGE_SIZE * DA_HEADS
    page_spec = lambda p: pl.BlockSpec((1, page_rows, HEAD_LANES), lambda b, pt: (pt[b * n_pages + p], 0, 0))
    per_b = lambda r: pl.BlockSpec((1, r, HEAD_LANES), lambda b, pt: (b, 0, 0))
    grid_spec = pltpu.PrefetchScalarGridSpec(
        num_scalar_prefetch=1,
        grid=(nb,),
        in_specs=[pl.BlockSpec(memory_space=pltpu.SMEM), per_b(q_rows.shape[1]),
                  per_b(n_new * DA_HEADS), per_b(n_new * DA_HEADS)]
                 + [page_spec(p) for p in range(n_pages)] * 2,
        out_specs=per_b(n_new * DA_HEADS),
        scratch_shapes=[pltpu.VMEM((n_pages * page_rows, HEAD_LANES), BF16)] * 2,
    )
    return pl.pallas_call(
        functools.partial(_decode_attn_kernel, n_pages=n_pages, n_new=n_new),
        grid_spec=grid_spec,
        out_shape=jax.ShapeDtypeStruct((nb, n_new * DA_HEADS, HEAD_LANES), F32),
        compiler_params=_params("arbitrary"),
        name="decode_attn",
    )(page_table.reshape(-1), lam, q_rows, k_new, v_new,
      *([cache_k] * n_pages), *([cache_v] * n_pages))


def _merge_kernel(ao_ref, gmo_ref, x_ref, sg_ref, wo_ref, g2_ref, wpq_ref, kbig_ref,
                  h_ref, tn_ref, s_ref, *, attn_scale):
    parts = []
    for b in range(DA_HEADS):
        sl = slice(b * HEAD_LANES, (b + 1) * HEAD_LANES)
        parts.append((_rms(ao_ref[:, sl], sg_ref[...]) * attn_scale).astype(BF16))
    cat = jnp.concatenate(parts + [gmo_ref[...]], axis=1)
    h = x_ref[...] + jnp.dot(cat, wo_ref[...], preferred_element_type=F32)
    h_ref[...] = h
    tn = _rms(h, g2_ref[...]).astype(BF16)
    tn_ref[...] = tn
    qp = jnp.dot(tn, wpq_ref[...], preferred_element_type=F32).astype(BF16)
    width = PEER_HEADS * PEER_HALF
    for c in range(2):
        s_ref[c] = lax.dot_general(kbig_ref[c], qp[:, c * width:(c + 1) * width],
                                   (((1,), (1,)), ((), ())), preferred_element_type=F32)


def _merge(ao, gmo, x, subln_g, w_o, g2, wpq, kbig, attn_scale):
    rows = x.shape[0]
    tile = min(ROW_TILE, rows)
    width = PEER_HEADS * PEER_HALF
    row_spec = lambda w: pl.BlockSpec((tile, w), lambda i: (i, 0))
    return pl.pallas_call(
        functools.partial(_merge_kernel, attn_scale=attn_scale),
        grid=(rows // tile,),
        in_specs=[row_spec(DA_WIDTH), row_spec(GM_WIDTH), row_spec(D_MODEL), _const_spec((1, HEAD_LANES)),
                  _const_spec((D_MODEL, D_MODEL)), _const_spec((1, D_MODEL)),
                  _const_spec((D_MODEL, 2 * width)), _const_spec((2, PEER_N_KEYS * PEER_HEADS, width))],
        out_specs=[row_spec(D_MODEL), row_spec(D_MODEL),
                   pl.BlockSpec((2, PEER_N_KEYS * PEER_HEADS, tile), lambda i: (0, 0, i))],
        out_shape=[jax.ShapeDtypeStruct((rows, D_MODEL), F32),
                   jax.ShapeDtypeStruct((rows, D_MODEL), BF16),
                   jax.ShapeDtypeStruct((2, PEER_N_KEYS * PEER_HEADS, rows), F32)],
        compiler_params=_params("parallel"),
        name="merge",
    )(ao, gmo, x, subln_g, w_o, g2, wpq, kbig)


def _young_candidates():
    return [(p, q) for p in range(PEER_TOPK) for q in range(PEER_TOPK) if (p + 1) * (q + 1) <= PEER_TOPK]


def _topk_kernel(s_ref, r2_ref, e2_ref, n_ref, c_ref, work_ref, top_ref, idx_ref, young_ref, stage_ref):
    nk = PEER_N_KEYS
    tile = s_ref.shape[2:]
    one, zero = jnp.ones(tile, F32), jnp.zeros(tile, F32)

    def extract_distinct(c):
        def round_(r, bound):
            tree = [jnp.where(s_ref[c, j] < bound, s_ref[c, j], NEG_INF) for j in range(nk)]
            while len(tree) > 1:
                tree = [jnp.maximum(tree[t], tree[t + 1]) for t in range(0, len(tree), 2)]
            top_ref[c, r] = tree[0]
            return tree[0]

        last = lax.fori_loop(0, PEER_TOPK, round_, jnp.full(tile, float("inf"), F32))
        reached = zero
        for j in range(nk):
            reached = reached + jnp.where(s_ref[c, j] >= last, one, zero)
        return reached

    def extract(c):
        for j in range(nk):
            work_ref[j] = s_ref[c, j]

        def round_(r, _):
            vals = [work_ref[j] for j in range(nk)]
            idxs = [None] * nk
            width = 1
            while len(vals) > 1:
                nv, ni = [], []
                for t in range(0, len(vals), 2):
                    a, b = vals[t], vals[t + 1]
                    take_b = b > a
                    ia = idxs[t] if idxs[t] is not None else float(t * width)
                    ib = idxs[t + 1] if idxs[t + 1] is not None else float((t + 1) * width)
                    nv.append(jnp.maximum(a, b))
                    ni.append(jnp.where(take_b, ib, ia))
                vals, idxs = nv, ni
                width *= 2
            best, best_idx = vals[0], idxs[0]
            top_ref[c, r] = best
            idx_ref[c, r] = best_idx
            for j in range(nk):
                work_ref[j] = jnp.where(best_idx == float(j), NEG_INF, work_ref[j])
            return 0

        lax.fori_loop(0, PEER_TOPK, round_, 0)

    miscount = jnp.maximum(jnp.abs(extract_distinct(0) - float(PEER_TOPK)),
                           jnp.abs(extract_distinct(1) - float(PEER_TOPK)))
    tied = jnp.max(miscount) > 0.0

    @pl.when(tied)
    def _():
        extract(0)
        extract(1)

    a = [top_ref[0, p] for p in range(PEER_TOPK)]
    b = [top_ref[1, q] for q in range(PEER_TOPK)]
    cands = _young_candidates()
    cand = {pq: a[pq[0]] + b[pq[1]] for pq in cands}
    n_of_p = [zero] * PEER_TOPK
    ea = [jnp.exp(a[p] - a[0]) for p in range(PEER_TOPK)]
    eb = [jnp.exp(b[q] - b[0]) for q in range(PEER_TOPK)]
    z = zero
    for (p, q) in cands:
        fixed = 0
        beaten = zero
        for (p2, q2) in cands:
            if (p2, q2) == (p, q):
                continue
            if p2 <= p and q2 <= q:
                fixed += 1
            elif p2 >= p and q2 >= q:
                continue
            elif p2 * PEER_TOPK + q2 < p * PEER_TOPK + q:
                beaten = beaten + jnp.where(cand[(p2, q2)] >= cand[(p, q)], one, zero)
            else:
                beaten = beaten + jnp.where(cand[(p2, q2)] > cand[(p, q)], one, zero)
        sel = jnp.where(beaten + float(fixed) < float(PEER_TOPK), one, zero)
        n_of_p[p] = n_of_p[p] + sel
        z = z + sel * (ea[p] * eb[q])
    inv_z = 1.0 / z
    for p in range(PEER_TOPK):
        young_ref[p] = n_of_p[p]

    def scatter(hit_a, hit_b):
        for j in range(nk):
            n_j = zero
            r_j = jnp.full(tile, float(nk - 1), F32)
            for p in range(PEER_TOPK):
                n_j = jnp.where(hit_a(p, j), young_ref[p], n_j)
                r_j = jnp.where(hit_b(p, j), float(p), r_j)
            stage_ref[0, j * PEER_HEADS:(j + 1) * PEER_HEADS, :] = n_j
            stage_ref[1, j * PEER_HEADS:(j + 1) * PEER_HEADS, :] = r_j

    @pl.when(jnp.logical_not(tied))
    def _():
        scatter(lambda p, j: top_ref[0, p] == s_ref[0, j], lambda p, j: top_ref[1, p] == s_ref[1, j])

    @pl.when(tied)
    def _():
        scatter(lambda p, j: idx_ref[0, p] == float(j), lambda p, j: idx_ref[1, p] == float(j))

    for j in range(nk):
        stage_ref[2, j * PEER_HEADS:(j + 1) * PEER_HEADS, :] = jnp.exp(s_ref[0, j] - a[0]) * inv_z
        stage_ref[3, j * PEER_HEADS:(j + 1) * PEER_HEADS, :] = jnp.exp(s_ref[1, j] - b[0])
    for h in range(PEER_HEADS):
        of_head = pl.ds(h, nk, stride=PEER_HEADS)
        n_ref[h] = stage_ref[0, of_head, :]
        r2_ref[h] = stage_ref[1, of_head, :].astype(BF16)
        c_ref[h] = stage_ref[2, of_head, :]
        e2_ref[h] = stage_ref[3, of_head, :].astype(BF16)


def _topk(scores):
    tokens = scores.shape[-1]
    lanes = TOPK_LANES
    tab_spec = pl.BlockSpec((PEER_HEADS, PEER_N_KEYS, lanes), lambda i: (0, 0, i))
    tab = lambda dt: jax.ShapeDtypeStruct((PEER_HEADS, PEER_N_KEYS, tokens), dt)
    return pl.pallas_call(
        _topk_kernel,
        grid=(tokens // lanes,),
        in_specs=[pl.BlockSpec((2, PEER_N_KEYS, PEER_HEADS, lanes), lambda i: (0, 0, 0, i))],
        out_specs=[tab_spec] * 4,
        out_shape=[tab(BF16), tab(BF16), tab(F32), tab(F32)],
        scratch_shapes=[pltpu.VMEM((PEER_N_KEYS, PEER_HEADS, lanes), F32),
                        pltpu.VMEM((2, PEER_TOPK, PEER_HEADS, lanes), F32),
                        pltpu.VMEM((2, PEER_TOPK, PEER_HEADS, lanes), F32),
                        pltpu.VMEM((PEER_TOPK, PEER_HEADS, lanes), F32),
                        pltpu.VMEM((4, PEER_N_KEYS * PEER_HEADS, lanes), F32)],
        compiler_params=_params("parallel"),
        name="peer_topk",
    )(scores)


def _peer_kernel(tn_ref, pu_ref, pvt_ref, r2_ref, e2_ref, n_ref, c_ref, acc_ref, at_ref, hid_ref):
    nk = PEER_N_KEYS
    keys_per_step = pu_ref.shape[0] // nk

    @pl.when(pl.program_id(1) == 0)
    def _():
        acc_ref[...] = jnp.zeros_like(acc_ref)

    at_ref[...] = lax.dot_general(pu_ref[...], tn_ref[...], (((1,), (1,)), ((), ())),
                                  preferred_element_type=F32).astype(BF16)
    for ii in range(keys_per_step):
        rs = slice(ii * nk, (ii + 1) * nk)
        gate = jnp.zeros((nk, tn_ref.shape[0]), BF16)
        for h in range(PEER_HEADS):
            n = n_ref[h, ii:ii + 1, :].astype(BF16)
            c = c_ref[h, ii:ii + 1, :].astype(BF16)
            gate = gate + jnp.where(r2_ref[h] < n, e2_ref[h] * c, jnp.zeros_like(gate))
        hid_ref[rs, :] = gate * _gelu(at_ref[rs, :])
    acc_ref[...] += jnp.dot(pvt_ref[...], hid_ref[...], preferred_element_type=F32)


def _peer(tn, pu, pvt, r2, e2, ntab, ctab):
    tokens = tn.shape[0]
    tt = min(PEER_TOKEN_TILE, tokens)
    et = PEER_EXPERT_TILE
    whole_tab = pl.BlockSpec((PEER_HEADS, PEER_N_KEYS, tt), lambda t, e: (0, 0, t))
    step_tab = pl.BlockSpec((PEER_HEADS, et // PEER_N_KEYS, tt), lambda t, e: (0, e, t))
    return pl.pallas_call(
        _peer_kernel,
        grid=(tokens // tt, PEER_N_EXPERTS // et),
        in_specs=[pl.BlockSpec((tt, D_MODEL), lambda t, e: (t, 0)),
                  pl.BlockSpec((et, D_MODEL), lambda t, e: (e, 0)),
                  pl.BlockSpec((D_MODEL, et), lambda t, e: (0, e)),
                  whole_tab, whole_tab, step_tab, step_tab],
        out_specs=pl.BlockSpec((D_MODEL, tt), lambda t, e: (0, t)),
        out_shape=jax.ShapeDtypeStruct((D_MODEL, tokens), F32),
        scratch_shapes=[pltpu.VMEM((et, tt), BF16), pltpu.VMEM((et, tt), BF16)],
        compiler_params=_params("parallel", "arbitrary"),
        name="peer_experts",
    )(tn, pu, pvt, r2, e2, ntab, ctab)


def _final_kernel(h_ref, mix_ref, gf_ref, y_ref):
    y_ref[...] = _rms(h_ref[...] + jnp.transpose(mix_ref[...]), gf_ref[...])


def _final(h, mix_t, final_g):
    rows = h.shape[0]
    tile = min(ROW_TILE, rows)
    row_spec = pl.BlockSpec((tile, D_MODEL), lambda i: (i, 0))
    return pl.pallas_call(
        _final_kernel,
        grid=(rows // tile,),
        in_specs=[row_spec, pl.BlockSpec((D_MODEL, tile), lambda i: (0, i)), _const_spec((1, D_MODEL))],
        out_specs=row_spec,
        out_shape=jax.ShapeDtypeStruct((rows, D_MODEL), F32),
        compiler_params=_params("parallel"),
        name="final_norm",
    )(h, mix_t, final_g)


def _rope_tables(pos):
    half = ROPE_DIM // 2
    inv = ROPE_THETA ** (-jnp.arange(half, dtype=F32) * 2.0 / ROPE_DIM)
    d = jnp.arange(HEAD_LANES) % DA_HEAD_DIM
    ang = pos[:, None] * inv[d % half][None, :]
    cos_t = jnp.where(d < ROPE_DIM, jnp.cos(ang), 1.0)
    sin_a = jnp.where(d < half, -jnp.sin(ang), 0.0)
    sin_b = jnp.where((d >= half) & (d < ROPE_DIM), jnp.sin(ang), 0.0)
    return cos_t, sin_a, sin_b


def _row_pipeline(x, rope_tabs, wmix, bias, attend, weights):
    (g1, w_in, gm_g, subln_g, w_o, g2, wpq, kbig, pu, pvt, final_g, attn_scale) = weights
    q1, q2, k32, v32, kb, vb, gmo, gn = _proj(x, g1, w_in, *rope_tabs, gm_g, wmix, bias)
    ao = attend(q1, q2, k32, v32, kb, vb)
    h, tn, scores = _merge(ao, gmo, x, subln_g, w_o, g2, wpq, kbig, attn_scale)
    rows = x.shape[0]
    r2, e2, ntab, ctab = _topk(scores.reshape(2, PEER_N_KEYS, PEER_HEADS, rows))
    y = _final(h, _peer(tn, pu, pvt, r2, e2, ntab, ctab), final_g)
    return y, k32, v32, gn


def kernel(x_prompt, x_sample, cache_k, cache_v, page_table, norm1_g, w_in, gm_norm_g, gm_ws, gm_b,
           da_lq1, da_lk1, da_lq2, da_lk2, da_subln_g, w_o, norm2_g, peer_wq, peer_keys, peer_u,
           peer_v, final_g):
    batch, seq, _ = x_prompt.shape
    nb, n_new, _ = x_sample.shape
    n_pages = page_table.shape[1]
    past = n_pages * PAGE_SIZE
    assert w_in.shape[0] == 1 and seq % CHUNK == 0 and seq % min(ROW_TILE, batch * seq) == 0

    lam_init = 0.8 - 0.6 * math.exp(-0.3 * 0)
    dots = lambda a, b: jnp.exp(jnp.sum(a.astype(F32) * b.astype(F32)))
    lam = (dots(da_lq1[0], da_lk1[0]) - dots(da_lq2[0], da_lk2[0]) + lam_init).reshape(1).astype(F32)

    eye = jnp.eye(PEER_HEADS, dtype=F32)
    kbig = jnp.einsum("hcjd,hg->cjhgd", peer_keys[0], eye).reshape(
        2, PEER_N_KEYS * PEER_HEADS, PEER_HEADS * PEER_HALF).astype(BF16)
    wpq = peer_wq[0].reshape(D_MODEL, PEER_HEADS, 2, PEER_HALF).transpose(0, 2, 1, 3).reshape(
        D_MODEL, 2 * PEER_HEADS * PEER_HALF).astype(BF16)
    weights = (norm1_g[0][None], w_in[0].astype(BF16), gm_norm_g[0].reshape(1, GM_WIDTH),
               da_subln_g[0][None], w_o[0].astype(BF16), norm2_g[0][None], wpq, kbig,
               peer_u[0].astype(BF16), peer_v[0].T.astype(BF16), final_g[None], 1.0 - lam_init)

    xp = x_prompt.reshape(batch * seq, D_MODEL)
    tabs_p = _rope_tables(jnp.arange(seq, dtype=F32))
    wmix_p = jnp.tril(gm_ws[0]).astype(BF16)
    bias_p = jnp.repeat(jnp.transpose(gm_b[0]), HEAD_LANES, axis=1)
    attend_p = lambda q1, q2, k32, v32, kb, vb: _prompt_attention(lam, q1, q2, kb, vb, batch, seq)
    y_p, k_p, v_p, _ = _row_pipeline(xp, tabs_p, wmix_p, bias_p, attend_p, weights)

    xs = x_sample.reshape(nb * n_new, D_MODEL)
    tabs_s = _rope_tables(jnp.tile(past + jnp.arange(n_new, dtype=F32), nb))
    rows_s = min(ROW_TILE, nb * n_new)
    w_new = jnp.tril(gm_ws[0][:, :n_new, :n_new])
    r = jnp.arange(rows_s)
    pick = (r[:, None] % n_new == jnp.arange(n_new)[None, :]).astype(F32)
    w_rows = jnp.einsum("ri,hij,cj->hrc", pick, w_new, pick, precision=lax.Precision.HIGHEST)
    wmix_s = jnp.where(r[:, None] // n_new == r[None, :] // n_new, w_rows, 0.0).astype(BF16)
    bias_s = jnp.tile(jnp.repeat(jnp.transpose(gm_b[0][:, :n_new]), HEAD_LANES, axis=1), (rows_s // n_new, 1))
    ck = cache_k[0].reshape(-1, PAGE_SIZE * DA_HEADS, HEAD_LANES)
    cv = cache_v[0].reshape(-1, PAGE_SIZE * DA_HEADS, HEAD_LANES)

    def attend_s(q1, q2, k32, v32, kb, vb):
        per_head = lambda t: jnp.transpose(t.reshape(nb, n_new, DA_HEADS, HEAD_LANES), (0, 2, 1, 3))
        q = jnp.stack([per_head(q1), per_head(q2)], axis=1).reshape(nb, 2 * DA_HEADS * n_new, HEAD_LANES)
        new_rows = lambda t: t.reshape(nb, n_new * DA_HEADS, HEAD_LANES).astype(BF16)
        ao = _decode_attention(page_table, lam, q, new_rows(k32), new_rows(v32), ck, cv)
        ao = jnp.transpose(ao.reshape(nb, DA_HEADS, n_new, HEAD_LANES), (0, 2, 1, 3))
        return ao.reshape(nb * n_new, DA_WIDTH)

    y_s, k_s, v_s, gn_s = _row_pipeline(xs, tabs_s, wmix_s, bias_s, attend_s, weights)

    head_shape = lambda b, s: (1, b, s, DA_HEADS, HEAD_LANES)
    return (y_p.reshape(batch, seq, D_MODEL), y_s.reshape(nb, n_new, D_MODEL),
            k_p.reshape(head_shape(batch, seq)), v_p.reshape(head_shape(batch, seq)),
            k_s.reshape(head_shape(nb, n_new)), v_s.reshape(head_shape(nb, n_new)),
            gn_s.reshape(1, nb, n_new, GM_HEADS, HEAD_LANES))
```

```python
import functools
import math

import jax
import jax.numpy as jnp
from jax import lax
from jax.experimental import pallas as pl
from jax.experimental.pallas import tpu as pltpu

F32 = jnp.float32
BF16 = jnp.bfloat16

D_MODEL = 1024
DA_WIDTH = 512
DA_HEADS = 4
HEAD_LANES = 128
DA_HEAD_DIM = 64
GM_WIDTH = 512
GM_HEADS = 4
CHUNK = 128
IN_WIDTH = 3 * DA_WIDTH + 2 * GM_WIDTH
ROPE_THETA = 500000.0
ROPE_DIM = 16
PAGE_SIZE = 128
PEER_HEADS = 8
PEER_N_KEYS = 128
PEER_N_EXPERTS = PEER_N_KEYS * PEER_N_KEYS
PEER_HALF = 128
PEER_TOPK = 16
NORM_EPS = 1e-6
NEG_INF = float("-inf")

VMEM_LIMIT_BYTES = 56 * 1024 * 1024

ROW_TILE = 512
ATTN_Q_TILE = 256
ATTN_HEADS_PER_STEP = 4
TOPK_LANES = 128
PEER_EXPERT_TILE = 1024
PEER_TOKEN_TILE = 1024


def _params(*semantics):
    return pltpu.CompilerParams(dimension_semantics=semantics, vmem_limit_bytes=VMEM_LIMIT_BYTES)


def _rms(x, g):
    return x * lax.rsqrt(jnp.mean(x * x, axis=-1, keepdims=True) + NORM_EPS) * g


def _gelu(x):
    c = math.sqrt(2.0 / math.pi)
    return x * (0.5 + 0.5 * jnp.tanh(x * (c + (c * 0.044715) * (x * x))))


def _const_spec(shape):
    zeros = (0,) * len(shape)
    return pl.BlockSpec(shape, lambda *_: zeros)


def _proj_kernel(x_ref, g1_ref, win_ref, cos_ref, sa_ref, sb_ref, gmg_ref, wmix_ref, bias_ref,
                 q1_ref, q2_ref, k32_ref, v32_ref, kb_ref, vb_ref, gmo_ref, gn_ref, *, mix_rows):
    rows = x_ref.shape[0]
    xn = _rms(x_ref[...], g1_ref[...]).astype(BF16)
    z = jnp.dot(xn, win_ref[...], preferred_element_type=F32)
    cos_t, sin_a, sin_b = cos_ref[...], sa_ref[...], sb_ref[...]
    first_half = lax.broadcasted_iota(jnp.int32, (rows, HEAD_LANES), 1) < DA_HEAD_DIM

    def rope(t):
        return t * cos_t + pltpu.roll(t, HEAD_LANES - 8, 1) * sin_a + pltpu.roll(t, 8, 1) * sin_b

    for b in range(DA_HEADS):
        sl = slice(b * HEAD_LANES, (b + 1) * HEAD_LANES)
        q = rope(z[:, sl]) * (1.0 / math.sqrt(DA_HEAD_DIM))
        q1_ref[:, sl] = jnp.where(first_half, q, 0.0).astype(BF16)
        q2_ref[:, sl] = jnp.where(first_half, 0.0, q).astype(BF16)
        k = rope(z[:, DA_WIDTH + b * HEAD_LANES:DA_WIDTH + (b + 1) * HEAD_LANES])
        v = z[:, 2 * DA_WIDTH + b * HEAD_LANES:2 * DA_WIDTH + (b + 1) * HEAD_LANES]
        k32_ref[pl.ds(b, rows, stride=DA_HEADS), :] = k
        v32_ref[pl.ds(b, rows, stride=DA_HEADS), :] = v
        kb_ref[:, sl] = k.astype(BF16)
        vb_ref[:, sl] = v.astype(BF16)

    u = _gelu(z[:, 3 * DA_WIDTH:3 * DA_WIDTH + GM_WIDTH])
    g = _gelu(z[:, 3 * DA_WIDTH + GM_WIDTH:])
    for b in range(GM_HEADS):
        sl = slice(b * HEAD_LANES, (b + 1) * HEAD_LANES)
        gn = _rms(g[:, sl], gmg_ref[:, sl])
        gn_ref[:, sl] = gn
        gnb = gn.astype(BF16)
        for r in range(rows // mix_rows):
            rs = slice(r * mix_rows, (r + 1) * mix_rows)
            s = jnp.dot(wmix_ref[b], gnb[rs], preferred_element_type=F32) + bias_ref[:, sl]
            gmo_ref[rs, sl] = (u[rs, sl] * s).astype(BF16)


def _proj(x, g1, w_in, cos_t, sin_a, sin_b, gm_g, wmix, bias):
    rows = x.shape[0]
    tile = min(ROW_TILE, rows)
    mix_rows = wmix.shape[-1]
    row_spec = lambda w: pl.BlockSpec((tile, w), lambda i: (i, 0))
    table_blocks = cos_t.shape[0] // tile
    table_spec = pl.BlockSpec((tile, HEAD_LANES), lambda i: (i % table_blocks, 0))
    head_rows_spec = pl.BlockSpec((tile * DA_HEADS, HEAD_LANES), lambda i: (i, 0))
    out_shape = [jax.ShapeDtypeStruct((rows, DA_WIDTH), BF16),
                 jax.ShapeDtypeStruct((rows, DA_WIDTH), BF16),
                 jax.ShapeDtypeStruct((rows * DA_HEADS, HEAD_LANES), F32),
                 jax.ShapeDtypeStruct((rows * DA_HEADS, HEAD_LANES), F32),
                 jax.ShapeDtypeStruct((rows, DA_WIDTH), BF16),
                 jax.ShapeDtypeStruct((rows, DA_WIDTH), BF16),
                 jax.ShapeDtypeStruct((rows, GM_WIDTH), BF16),
                 jax.ShapeDtypeStruct((rows, GM_WIDTH), F32)]
    return pl.pallas_call(
        functools.partial(_proj_kernel, mix_rows=mix_rows),
        grid=(rows // tile,),
        in_specs=[row_spec(D_MODEL), _const_spec((1, D_MODEL)), _const_spec((D_MODEL, IN_WIDTH)),
                  table_spec, table_spec, table_spec,
                  _const_spec((1, GM_WIDTH)), _const_spec(wmix.shape), _const_spec(bias.shape)],
        out_specs=[row_spec(DA_WIDTH)] * 2 + [head_rows_spec] * 2 + [row_spec(DA_WIDTH)] * 2
                  + [row_spec(GM_WIDTH)] * 2,
        out_shape=out_shape,
        compiler_params=_params("parallel"),
        name="proj",
    )(x, g1, w_in, cos_t, sin_a, sin_b, gm_g, wmix, bias)


def _attn_kernel(lam_ref, q1_ref, q2_ref, k_ref, v_ref, o_ref):
    tq = q1_ref.shape[0]
    heads = q1_ref.shape[1] // HEAD_LANES
    i = pl.program_id(2)
    head_lanes = [slice(h * HEAD_LANES, (h + 1) * HEAD_LANES) for h in range(heads)]
    qs = [jnp.concatenate([q1_ref[:, sl], q2_ref[:, sl]], axis=0) for sl in head_lanes]

    def step(j, carry, masked):
        start = pl.multiple_of(j * tq, tq)
        new = []
        for q, sl, (m, l, acc) in zip(qs, head_lanes, carry):
            kb = k_ref[pl.ds(start, tq), sl]
            vb = v_ref[pl.ds(start, tq), sl]
            s = lax.dot_general(q, kb, (((1,), (1,)), ((), ())), preferred_element_type=F32)
            if masked:
                row = lax.broadcasted_iota(jnp.int32, (2 * tq, tq), 0)
                col = lax.broadcasted_iota(jnp.int32, (2 * tq, tq), 1)
                row = jnp.where(row >= tq, row - tq, row)
                s = jnp.where(col <= row, s, NEG_INF)
            m_new = jnp.maximum(m, jnp.max(s, axis=-1, keepdims=True))
            alpha = jnp.exp(m - m_new)
            p = jnp.exp(s - m_new)
            l = alpha * l + jnp.sum(p, axis=-1, keepdims=True)
            acc = alpha * acc + jnp.dot(p.astype(BF16), vb, preferred_element_type=F32)
            new.append((m_new, l, acc))
        return tuple(new)

    init = tuple((jnp.full((2 * tq, 1), NEG_INF, F32), jnp.zeros((2 * tq, 1), F32),
                  jnp.zeros((2 * tq, HEAD_LANES), F32)) for _ in range(heads))
    carry = lax.fori_loop(0, i, lambda j, c: step(j, c, False), init)
    for sl, (_, l, acc) in zip(head_lanes, step(i, carry, True)):
        o = acc / l
        o_ref[:, sl] = o[:tq] - lam_ref[0] * o[tq:]


def _prompt_attention(lam, q1, q2, kb, vb, batch, seq):
    tq = min(ATTN_Q_TILE, seq)
    nq = seq // tq
    width = ATTN_HEADS_PER_STEP * HEAD_LANES
    q_spec = pl.BlockSpec((tq, width), lambda b, h, i: (b * nq + i, h))
    kv_spec = pl.BlockSpec((seq, width), lambda b, h, i: (b, h))
    return pl.pallas_call(
        _attn_kernel,
        grid=(batch, DA_HEADS // ATTN_HEADS_PER_STEP, nq),
        in_specs=[pl.BlockSpec(memory_space=pltpu.SMEM), q_spec, q_spec, kv_spec, kv_spec],
        out_specs=q_spec,
        out_shape=jax.ShapeDtypeStruct((batch * seq, DA_WIDTH), F32),
        compiler_params=_params("parallel", "parallel", "arbitrary"),
        name="prompt_attn",
    )(lam, q1, q2, kb, vb)


def _decode_attn_kernel(pt_ref, lam_ref, q_ref, kn_ref, vn_ref, *rest, n_pages, n_new):
    del pt_ref
    k_pages, v_pages = rest[:n_pages], rest[n_pages:2 * n_pages]
    o_ref, k_all, v_all = rest[2 * n_pages:]
    page_rows = PAGE_SIZE * DA_HEADS
    for p in range(n_pages):
        rs = slice(p * page_rows, (p + 1) * page_rows)
        k_all[rs, :] = k_pages[p][0].astype(BF16)
        v_all[rs, :] = v_pages[p][0].astype(BF16)
    q = q_ref[0]
    nrow = q.shape[0]
    contract_last = (((1,), (1,)), ((), ()))
    s_past = lax.dot_general(q, k_all[...], contract_last, preferred_element_type=F32)
    s_new = lax.dot_general(q, kn_ref[0], contract_last, preferred_element_type=F32)

    def row_col(shape):
        row = lax.broadcasted_iota(jnp.int32, shape, 0)
        col = lax.broadcasted_iota(jnp.int32, shape, 1)
        return (row // n_new) % DA_HEADS, row % n_new, col % DA_HEADS, col // DA_HEADS

    q_head, _, k_head, _ = row_col(s_past.shape)
    s_past = jnp.where(q_head == k_head, s_past, NEG_INF)
    q_head, q_tok, k_head, k_tok = row_col(s_new.shape)
    s_new = jnp.where(q_head == k_head, jnp.where(k_tok <= q_tok, s_new, NEG_INF), NEG_INF)
    m = jnp.maximum(jnp.max(s_past, axis=-1, keepdims=True), jnp.max(s_new, axis=-1, keepdims=True))
    p_past = jnp.exp(s_past - m)
    p_new = jnp.exp(s_new - m)
    inv = 1.0 / (jnp.sum(p_past, axis=-1, keepdims=True) + jnp.sum(p_new, axis=-1, keepdims=True))
    half = nrow // 2
    lam = lam_ref[0]
    a_past = (p_past[:half] * inv[:half] - lam * (p_past[half:] * inv[half:])).astype(BF16)
    a_new = (p_new[:half] * inv[:half] - lam * (p_new[half:] * inv[half:])).astype(BF16)
    o_ref[0] = (jnp.dot(a_past, v_all[...], preferred_element_type=F32)
                + jnp.dot(a_new, vn_ref[0], preferred_element_type=F32))


def _decode_attention(page_table, lam, q_rows, k_new, v_new, cache_k, cache_v):
    nb, n_pages = page_table.shape
    n_new = k_new.shape[1] // DA_HEADS
    page_rows = PAGE_SIZE * DA_HEADS
    page_spec = lambda p: pl.BlockSpec((1, page_rows, HEAD_LANES), lambda b, pt: (pt[b * n_pages + p], 0, 0))
    per_b = lambda r: pl.BlockSpec((1, r, HEAD_LANES), lambda b, pt: (b, 0, 0))
    grid_spec = pltpu.PrefetchScalarGridSpec(
        num_scalar_prefetch=1,
        grid=(nb,),
        in_specs=[pl.BlockSpec(memory_space=pltpu.SMEM), per_b(q_rows.shape[1]),
                  per_b(n_new * DA_HEADS), per_b(n_new * DA_HEADS)]
                 + [page_spec(p) for p in range(n_pages)] * 2,
        out_specs=per_b(n_new * DA_HEADS),
        scratch_shapes=[pltpu.VMEM((n_pages * page_rows, HEAD_LANES), BF16)] * 2,
    )
    return pl.pallas_call(
        functools.partial(_decode_attn_kernel, n_pages=n_pages, n_new=n_new),
        grid_spec=grid_spec,
        out_shape=jax.ShapeDtypeStruct((nb, n_new * DA_HEADS, HEAD_LANES), F32),
        compiler_params=_params("arbitrary"),
        name="decode_attn",
    )(page_table.reshape(-1), lam, q_rows, k_new, v_new,
      *([cache_k] * n_pages), *([cache_v] * n_pages))


def _merge_kernel(ao_ref, gmo_ref, x_ref, sg_ref, wo_ref, g2_ref, wpq_ref, kbig_ref,
                  h_ref, tn_ref, s_ref, *, attn_scale):
    parts = []
    for b in range(DA_HEADS):
        sl = slice(b * HEAD_LANES, (b + 1) * HEAD_LANES)
        parts.append((_rms(ao_ref[:, sl], sg_ref[...]) * attn_scale).astype(BF16))
    cat = jnp.concatenate(parts + [gmo_ref[...]], axis=1)
    h = x_ref[...] + jnp.dot(cat, wo_ref[...], preferred_element_type=F32)
    h_ref[...] = h
    tn = _rms(h, g2_ref[...]).astype(BF16)
    tn_ref[...] = tn
    qp = jnp.dot(tn, wpq_ref[...], preferred_element_type=F32).astype(BF16)
    width = PEER_HEADS * PEER_HALF
    for c in range(2):
        s_ref[c] = lax.dot_general(kbig_ref[c], qp[:, c * width:(c + 1) * width],
                                   (((1,), (1,)), ((), ())), preferred_element_type=F32)


def _merge(ao, gmo, x, subln_g, w_o, g2, wpq, kbig, attn_scale):
    rows = x.shape[0]
    tile = min(ROW_TILE, rows)
    width = PEER_HEADS * PEER_HALF
    row_spec = lambda w: pl.BlockSpec((tile, w), lambda i: (i, 0))
    return pl.pallas_call(
        functools.partial(_merge_kernel, attn_scale=attn_scale),
        grid=(rows // tile,),
        in_specs=[row_spec(DA_WIDTH), row_spec(GM_WIDTH), row_spec(D_MODEL), _const_spec((1, HEAD_LANES)),
                  _const_spec((D_MODEL, D_MODEL)), _const_spec((1, D_MODEL)),
                  _const_spec((D_MODEL, 2 * width)), _const_spec((2, PEER_N_KEYS * PEER_HEADS, width))],
        out_specs=[row_spec(D_MODEL), row_spec(D_MODEL),
                   pl.BlockSpec((2, PEER_N_KEYS * PEER_HEADS, tile), lambda i: (0, 0, i))],
        out_shape=[jax.ShapeDtypeStruct((rows, D_MODEL), F32),
                   jax.ShapeDtypeStruct((rows, D_MODEL), BF16),
                   jax.ShapeDtypeStruct((2, PEER_N_KEYS * PEER_HEADS, rows), F32)],
        compiler_params=_params("parallel"),
        name="merge",
    )(ao, gmo, x, subln_g, w_o, g2, wpq, kbig)


def _young_candidates():
    return [(p, q) for p in range(PEER_TOPK) for q in range(PEER_TOPK) if (p + 1) * (q + 1) <= PEER_TOPK]


def _topk_kernel(s_ref, r2_ref, e2_ref, n_ref, c_ref, work_ref, top_ref, idx_ref, young_ref, stage_ref):
    nk = PEER_N_KEYS
    tile = s_ref.shape[2:]
    one, zero = jnp.ones(tile, F32), jnp.zeros(tile, F32)

    def extract_distinct(c):
        def round_(r, bound):
            tree = [jnp.where(s_ref[c, j] < bound, s_ref[c, j], NEG_INF) for j in range(nk)]
            while len(tree) > 1:
                tree = [jnp.maximum(tree[t], tree[t + 1]) for t in range(0, len(tree), 2)]
            top_ref[c, r] = tree[0]
            return tree[0]

        last = lax.fori_loop(0, PEER_TOPK, round_, jnp.full(tile, float("inf"), F32))
        reached = zero
        for j in range(nk):
            reached = reached + jnp.where(s_ref[c, j] >= last, one, zero)
        return reached

    def extract(c):
        for j in range(nk):
            work_ref[j] = s_ref[c, j]

        def round_(r, _):
            vals = [work_ref[j] for j in range(nk)]
            idxs = [None] * nk
            width = 1
            while len(vals) > 1:
                nv, ni = [], []
                for t in range(0, len(vals), 2):
                    a, b = vals[t], vals[t + 1]
                    take_b = b > a
                    ia = idxs[t] if idxs[t] is not None else float(t * width)
                    ib = idxs[t + 1] if idxs[t + 1] is not None else float((t + 1) * width)
                    nv.append(jnp.maximum(a, b))
                    ni.append(jnp.where(take_b, ib, ia))
                vals, idxs = nv, ni
                width *= 2
            best, best_idx = vals[0], idxs[0]
            top_ref[c, r] = best
            idx_ref[c, r] = best_idx
            for j in range(nk):
                work_ref[j] = jnp.where(best_idx == float(j), NEG_INF, work_ref[j])
            return 0

        lax.fori_loop(0, PEER_TOPK, round_, 0)

    miscount = jnp.maximum(jnp.abs(extract_distinct(0) - float(PEER_TOPK)),
                           jnp.abs(extract_distinct(1) - float(PEER_TOPK)))
    tied = jnp.max(miscount) > 0.0

    @pl.when(tied)
    def _():
        extract(0)
        extract(1)

    a = [top_ref[0, p] for p in range(PEER_TOPK)]
    b = [top_ref[1, q] for q in range(PEER_TOPK)]
    cands = _young_candidates()
    cand = {pq: a[pq[0]] + b[pq[1]] for pq in cands}
    n_of_p = [zero] * PEER_TOPK
    ea = [jnp.exp(a[p] - a[0]) for p in range(PEER_TOPK)]
    eb = [jnp.exp(b[q] - b[0]) for q in range(PEER_TOPK)]
    z = zero
    for (p, q) in cands:
        fixed = 0
        beaten = zero
        for (p2, q2) in cands:
            if (p2, q2) == (p, q):
                continue
            if p2 <= p and q2 <= q:
                fixed += 1
            elif p2 >= p and q2 >= q:
                continue
            elif p2 * PEER_TOPK + q2 < p * PEER_TOPK + q:
                beaten = beaten + jnp.where(cand[(p2, q2)] >= cand[(p, q)], one, zero)
            else:
                beaten = beaten + jnp.where(cand[(p2, q2)] > cand[(p, q)], one, zero)
        sel = jnp.where(beaten + float(fixed) < float(PEER_TOPK), one, zero)
        n_of_p[p] = n_of_p[p] + sel
        z = z + sel * (ea[p] * eb[q])
    inv_z = 1.0 / z
    for p in range(PEER_TOPK):
        young_ref[p] = n_of_p[p]

    def scatter(hit_a, hit_b):
        for j in range(nk):
            n_j = zero
            r_j = jnp.full(tile, float(nk - 1), F32)
            for p in range(PEER_TOPK):
                n_j = jnp.where(hit_a(p, j), young_ref[p], n_j)
                r_j = jnp.where(hit_b(p, j), float(p), r_j)
            stage_ref[0, j * PEER_HEADS:(j + 1) * PEER_HEADS, :] = n_j
            stage_ref[1, j * PEER_HEADS:(j + 1) * PEER_HEADS, :] = r_j

    @pl.when(jnp.logical_not(tied))
    def _():
        scatter(lambda p, j: top_ref[0, p] == s_ref[0, j], lambda p, j: top_ref[1, p] == s_ref[1, j])

    @pl.when(tied)
    def _():
        scatter(lambda p, j: idx_ref[0, p] == float(j), lambda p, j: idx_ref[1, p] == float(j))

    for j in range(nk):
        stage_ref[2, j * PEER_HEADS:(j + 1) * PEER_HEADS, :] = jnp.exp(s_ref[0, j] - a[0]) * inv_z
        stage_ref[3, j * PEER_HEADS:(j + 1) * PEER_HEADS, :] = jnp.exp(s_ref[1, j] - b[0])
    for h in range(PEER_HEADS):
        of_head = pl.ds(h, nk, stride=PEER_HEADS)
        n_ref[h] = stage_ref[0, of_head, :]
        r2_ref[h] = stage_ref[1, of_head, :].astype(BF16)
        c_ref[h] = stage_ref[2, of_head, :]
        e2_ref[h] = stage_ref[3, of_head, :].astype(BF16)


def _topk(scores):
    tokens = scores.shape[-1]
    lanes = TOPK_LANES
    tab_spec = pl.BlockSpec((PEER_HEADS, PEER_N_KEYS, lanes), lambda i: (0, 0, i))
    tab = lambda dt: jax.ShapeDtypeStruct((PEER_HEADS, PEER_N_KEYS, tokens), dt)
    return pl.pallas_call(
        _topk_kernel,
        grid=(tokens // lanes,),
        in_specs=[pl.BlockSpec((2, PEER_N_KEYS, PEER_HEADS, lanes), lambda i: (0, 0, 0, i))],
        out_specs=[tab_spec] * 4,
        out_shape=[tab(BF16), tab(BF16), tab(F32), tab(F32)],
        scratch_shapes=[pltpu.VMEM((PEER_N_KEYS, PEER_HEADS, lanes), F32),
                        pltpu.VMEM((2, PEER_TOPK, PEER_HEADS, lanes), F32),
                        pltpu.VMEM((2, PEER_TOPK, PEER_HEADS, lanes), F32),
                        pltpu.VMEM((PEER_TOPK, PEER_HEADS, lanes), F32),
                        pltpu.VMEM((4, PEER_N_KEYS * PEER_HEADS, lanes), F32)],
        compiler_params=_params("parallel"),
        name="peer_topk",
    )(scores)


def _peer_kernel(tn_ref, pu_ref, pvt_ref, r2_ref, e2_ref, n_ref, c_ref, acc_ref, at_ref, hid_ref):
    nk = PEER_N_KEYS
    keys_per_step = pu_ref.shape[0] // nk

    @pl.when(pl.program_id(1) == 0)
    def _():
        acc_ref[...] = jnp.zeros_like(acc_ref)

    at_ref[...] = lax.dot_general(pu_ref[...], tn_ref[...], (((1,), (1,)), ((), ())),
                                  preferred_element_type=F32)
    for ii in range(keys_per_step):
        rs = slice(ii * nk, (ii + 1) * nk)
        gate = jnp.zeros((nk, tn_ref.shape[0]), BF16)
        for h in range(PEER_HEADS):
            n = n_ref[h, ii:ii + 1, :].astype(BF16)
            c = c_ref[h, ii:ii + 1, :].astype(BF16)
            gate = gate + jnp.where(r2_ref[h] < n, e2_ref[h] * c, jnp.zeros_like(gate))
        hid_ref[rs, :] = gate * _gelu(at_ref[rs, :]).astype(BF16)
    acc_ref[...] += jnp.dot(pvt_ref[...], hid_ref[...], preferred_element_type=F32)


def _peer(tn, pu, pvt, r2, e2, ntab, ctab):
    tokens = tn.shape[0]
    tt = min(PEER_TOKEN_TILE, tokens)
    et = PEER_EXPERT_TILE
    whole_tab = pl.BlockSpec((PEER_HEADS, PEER_N_KEYS, tt), lambda t, e: (0, 0, t))
    step_tab = pl.BlockSpec((PEER_HEADS, et // PEER_N_KEYS, tt), lambda t, e: (0, e, t))
    return pl.pallas_call(
        _peer_kernel,
        grid=(tokens // tt, PEER_N_EXPERTS // et),
        in_specs=[pl.BlockSpec((tt, D_MODEL), lambda t, e: (t, 0)),
                  pl.BlockSpec((et, D_MODEL), lambda t, e: (e, 0)),
                  pl.BlockSpec((D_MODEL, et), lambda t, e: (0, e)),
                  whole_tab, whole_tab, step_tab, step_tab],
        out_specs=pl.BlockSpec((D_MODEL, tt), lambda t, e: (0, t)),
        out_shape=jax.ShapeDtypeStruct((D_MODEL, tokens), F32),
        scratch_shapes=[pltpu.VMEM((et, tt), F32), pltpu.VMEM((et, tt), BF16)],
        compiler_params=_params("parallel", "arbitrary"),
        name="peer_experts",
    )(tn, pu, pvt, r2, e2, ntab, ctab)


def _final_kernel(h_ref, mix_ref, gf_ref, y_ref):
    y_ref[...] = _rms(h_ref[...] + jnp.transpose(mix_ref[...]), gf_ref[...])


def _final(h, mix_t, final_g):
    rows = h.shape[0]
    tile = min(ROW_TILE, rows)
    row_spec = pl.BlockSpec((tile, D_MODEL), lambda i: (i, 0))
    return pl.pallas_call(
        _final_kernel,
        grid=(rows // tile,),
        in_specs=[row_spec, pl.BlockSpec((D_MODEL, tile), lambda i: (0, i)), _const_spec((1, D_MODEL))],
        out_specs=row_spec,
        out_shape=jax.ShapeDtypeStruct((rows, D_MODEL), F32),
        compiler_params=_params("parallel"),
        name="final_norm",
    )(h, mix_t, final_g)


def _rope_tables(pos):
    half = ROPE_DIM // 2
    inv = ROPE_THETA ** (-jnp.arange(half, dtype=F32) * 2.0 / ROPE_DIM)
    d = jnp.arange(HEAD_LANES) % DA_HEAD_DIM
    ang = pos[:, None] * inv[d % half][None, :]
    cos_t = jnp.where(d < ROPE_DIM, jnp.cos(ang), 1.0)
    sin_a = jnp.where(d < half, -jnp.sin(ang), 0.0)
    sin_b = jnp.where((d >= half) & (d < ROPE_DIM), jnp.sin(ang), 0.0)
    return cos_t, sin_a, sin_b


def _row_pipeline(x, rope_tabs, wmix, bias, attend, weights):
    (g1, w_in, gm_g, subln_g, w_o, g2, wpq, kbig, pu, pvt, final_g, attn_scale) = weights
    q1, q2, k32, v32, kb, vb, gmo, gn = _proj(x, g1, w_in, *rope_tabs, gm_g, wmix, bias)
    ao = attend(q1, q2, k32, v32, kb, vb)
    h, tn, scores = _merge(ao, gmo, x, subln_g, w_o, g2, wpq, kbig, attn_scale)
    rows = x.shape[0]
    r2, e2, ntab, ctab = _topk(scores.reshape(2, PEER_N_KEYS, PEER_HEADS, rows))
    y = _final(h, _peer(tn, pu, pvt, r2, e2, ntab, ctab), final_g)
    return y, k32, v32, gn


def kernel(x_prompt, x_sample, cache_k, cache_v, page_table, norm1_g, w_in, gm_norm_g, gm_ws, gm_b,
           da_lq1, da_lk1, da_lq2, da_lk2, da_subln_g, w_o, norm2_g, peer_wq, peer_keys, peer_u,
           peer_v, final_g):
    batch, seq, _ = x_prompt.shape
    nb, n_new, _ = x_sample.shape
    n_pages = page_table.shape[1]
    past = n_pages * PAGE_SIZE
    assert w_in.shape[0] == 1 and seq % CHUNK == 0 and seq % min(ROW_TILE, batch * seq) == 0

    lam_init = 0.8 - 0.6 * math.exp(-0.3 * 0)
    dots = lambda a, b: jnp.exp(jnp.sum(a.astype(F32) * b.astype(F32)))
    lam = (dots(da_lq1[0], da_lk1[0]) - dots(da_lq2[0], da_lk2[0]) + lam_init).reshape(1).astype(F32)

    eye = jnp.eye(PEER_HEADS, dtype=F32)
    kbig = jnp.einsum("hcjd,hg->cjhgd", peer_keys[0], eye).reshape(
        2, PEER_N_KEYS * PEER_HEADS, PEER_HEADS * PEER_HALF).astype(BF16)
    wpq = peer_wq[0].reshape(D_MODEL, PEER_HEADS, 2, PEER_HALF).transpose(0, 2, 1, 3).reshape(
        D_MODEL, 2 * PEER_HEADS * PEER_HALF).astype(BF16)
    weights = (norm1_g[0][None], w_in[0].astype(BF16), gm_norm_g[0].reshape(1, GM_WIDTH),
               da_subln_g[0][None], w_o[0].astype(BF16), norm2_g[0][None], wpq, kbig,
               peer_u[0].astype(BF16), peer_v[0].T.astype(BF16), final_g[None], 1.0 - lam_init)

    xp = x_prompt.reshape(batch * seq, D_MODEL)
    tabs_p = _rope_tables(jnp.arange(seq, dtype=F32))
    wmix_p = jnp.tril(gm_ws[0]).astype(BF16)
    bias_p = jnp.repeat(jnp.transpose(gm_b[0]), HEAD_LANES, axis=1)
    attend_p = lambda q1, q2, k32, v32, kb, vb: _prompt_attention(lam, q1, q2, kb, vb, batch, seq)
    y_p, k_p, v_p, _ = _row_pipeline(xp, tabs_p, wmix_p, bias_p, attend_p, weights)

    xs = x_sample.reshape(nb * n_new, D_MODEL)
    tabs_s = _rope_tables(jnp.tile(past + jnp.arange(n_new, dtype=F32), nb))
    rows_s = min(ROW_TILE, nb * n_new)
    w_new = jnp.tril(gm_ws[0][:, :n_new, :n_new])
    r = jnp.arange(rows_s)
    pick = (r[:, None] % n_new == jnp.arange(n_new)[None, :]).astype(F32)
    w_rows = jnp.einsum("ri,hij,cj->hrc", pick, w_new, pick, precision=lax.Precision.HIGHEST)
    wmix_s = jnp.where(r[:, None] // n_new == r[None, :] // n_new, w_rows, 0.0).astype(BF16)
    bias_s = jnp.tile(jnp.repeat(jnp.transpose(gm_b[0][:, :n_new]), HEAD_LANES, axis=1), (rows_s // n_new, 1))
    ck = cache_k[0].reshape(-1, PAGE_SIZE * DA_HEADS, HEAD_LANES)
    cv = cache_v[0].reshape(-1, PAGE_SIZE * DA_HEADS, HEAD_LANES)

    def attend_s(q1, q2, k32, v32, kb, vb):
        per_head = lambda t: jnp.transpose(t.reshape(nb, n_new, DA_HEADS, HEAD_LANES), (0, 2, 1, 3))
        q = jnp.stack([per_head(q1), per_head(q2)], axis=1).reshape(nb, 2 * DA_HEADS * n_new, HEAD_LANES)
        new_rows = lambda t: t.reshape(nb, n_new * DA_HEADS, HEAD_LANES).astype(BF16)
        ao = _decode_attention(page_table, lam, q, new_rows(k32), new_rows(v32), ck, cv)
        ao = jnp.transpose(ao.reshape(nb, DA_HEADS, n_new, HEAD_LANES), (0, 2, 1, 3))
        return ao.reshape(nb * n_new, DA_WIDTH)

    y_s, k_s, v_s, gn_s = _row_pipeline(xs, tabs_s, wmix_s, bias_s, attend_s, weights)

    head_shape = lambda b, s: (1, b, s, DA_HEADS, HEAD_LANES)
    return (y_p.reshape(batch, seq, D_MODEL), y_s.reshape(nb, n_new, D_MODEL),
            k_p.reshape(head_shape(batch, seq)), v_p.reshape(head_shape(batch, seq)),
            k_s.reshape(head_shape(nb, n_new)), v_s.reshape(head_shape(nb, n_new)),
            gn_s.reshape(1, nb, n_new, GM_HEADS, HEAD_LANES))
```
